```python
import jax
import jax.numpy as jnp
from jax import lax
import numpy as np


D_MODEL = 4096
BATCH = 8
SEQ = 2048
DEPTH = 2

N_MIXERS = 2
D_FF = 11008
FFN_RES_WEIGHT = 0.5
RMS_EPS = 1e-6
NEG_INF = -1e30
FORCED_SCORE = 1e30

NSA_HEADS = 32
NSA_GROUPS = 4
NSA_HPG = NSA_HEADS // NSA_GROUPS
NSA_HEAD_DIM = D_MODEL // NSA_HEADS
NSA_QD = NSA_HEADS * NSA_HEAD_DIM
NSA_KVD = NSA_GROUPS * NSA_HEAD_DIM
NSA_IN = NSA_QD + 6 * NSA_KVD + 3 * NSA_HEADS
CMP_BLOCK = 32
CMP_STRIDE = 16
SLC_BLOCK = 64
SLC_TOPN = 16
SLC_QCHUNK = 16
WINDOW = 512
WIN_QBLOCK = 128
ROPE_THETA = 10000.0

HGRN_HEADS = 32
HGRN_KEY_DIM = D_MODEL // HGRN_HEADS
HGRN_VAL_DIM = D_MODEL // HGRN_HEADS
HGRN_IN = 2 * HGRN_HEADS * HGRN_KEY_DIM + 2 * HGRN_HEADS * HGRN_VAL_DIM
HGRN_CHUNK = 64

N_NSA_LAYERS = (DEPTH + N_MIXERS - 1) // N_MIXERS
N_HGRN_LAYERS = DEPTH // N_MIXERS

kernel_name = 'nsa_hgrn2_macaron_hybrid'


def rms_norm(x, g):
    xf = x.astype(jnp.float32)
    y = xf * lax.rsqrt(jnp.mean(xf * xf, axis=-1, keepdims=True) + RMS_EPS)
    return (y * g.astype(jnp.float32)).astype(x.dtype)


def swiglu(x, w_gate, w_up, w_down):
    return (jax.nn.silu(x @ w_gate) * (x @ w_up)) @ w_down


def rope_tables(seq, dim):
    inv = ROPE_THETA ** (-jnp.arange(0, dim, 2, dtype=jnp.float32) / dim)
    ang = jnp.arange(seq, dtype=jnp.float32)[:, None] * inv[None, :]
    ang = jnp.concatenate([ang, ang], axis=-1)
    return jnp.cos(ang), jnp.sin(ang)


def apply_rope(x, cos, sin):
    xf = x.astype(jnp.float32)
    x1, x2 = jnp.split(xf, 2, axis=-1)
    rot = jnp.concatenate([-x2, x1], axis=-1)
    return (xf * cos + rot * sin).astype(x.dtype)


def nsa_mixer(h, w_in, q_norm, k_norm, cmp_pos, cmp_w1, cmp_w2, w_o, cos, sin):
    B, S, _ = h.shape
    dt = h.dtype
    f32 = jnp.float32
    G, HPG, DH = NSA_GROUPS, NSA_HPG, NSA_HEAD_DIM
    cuts = [NSA_QD + i * NSA_KVD for i in range(7)]
    q, kc, vc, ks, vs, kw, vw, gl = jnp.split(h @ w_in, cuts, axis=-1)

    q = q.reshape(B, S, NSA_HEADS, DH).transpose(0, 2, 1, 3)
    q = apply_rope(rms_norm(q, q_norm), cos, sin) * (DH ** -0.5)
    q = q.reshape(B, G, HPG, S, DH)

    def kv_heads(t):
        return t.reshape(B, S, G, DH).transpose(0, 2, 1, 3)

    kc = apply_rope(rms_norm(kv_heads(kc), k_norm[0]), cos, sin)
    ks = apply_rope(rms_norm(kv_heads(ks), k_norm[1]), cos, sin)
    kw = apply_rope(rms_norm(kv_heads(kw), k_norm[2]), cos, sin)
    vc, vs, vw = kv_heads(vc), kv_heads(vs), kv_heads(vw)
    pos = jnp.arange(S)

    n_cmp = (S - CMP_BLOCK) // CMP_STRIDE + 1
    cmp_start = np.arange(n_cmp) * CMP_STRIDE
    cmp_idx = cmp_start[:, None] + np.arange(CMP_BLOCK)[None, :]

    def compress(t, pe, w1, w2):
        blocks = t[:, :, cmp_idx] + pe
        flat = blocks.reshape(B, G, n_cmp, CMP_BLOCK * DH)
        return jax.nn.gelu(flat @ w1) @ w2

    k_cmp = compress(kc, cmp_pos[0], cmp_w1[0], cmp_w2[0])
    v_cmp = compress(vc, cmp_pos[1], cmp_w1[1], cmp_w2[1])
    s_cmp = jnp.einsum('bghtd,bgnd->bghtn', q, k_cmp, preferred_element_type=f32)
    cmp_valid = (cmp_start + CMP_BLOCK - 1)[None, :] <= pos[:, None]
    p_cmp = jax.nn.softmax(jnp.where(cmp_valid, s_cmp, NEG_INF), axis=-1) * cmp_valid
    o_cmp = jnp.einsum('bghtn,bgnd->bghtd', p_cmp.astype(dt), v_cmp)

    n_slc = S // SLC_BLOCK
    slc_start = np.arange(n_slc) * SLC_BLOCK
    overlap = ((cmp_start[:, None] < slc_start[None, :] + SLC_BLOCK)
               & (cmp_start[:, None] + CMP_BLOCK > slc_start[None, :])).astype(np.float32)
    imp = jnp.einsum('bghtn,nj->bgtj', p_cmp, jnp.asarray(overlap))
    blk = jnp.arange(n_slc)
    cur = pos // SLC_BLOCK
    forced = (blk[None, :] == 0) | (blk[None, :] == cur[:, None]) | (blk[None, :] == cur[:, None] - 1)
    causal_blk = (blk * SLC_BLOCK)[None, :] <= pos[:, None]
    imp = jnp.where(forced, FORCED_SCORE, jnp.where(causal_blk, imp, NEG_INF))
    top_n = min(SLC_TOPN, n_slc)
    _, sel = lax.top_k(imp, top_n)

    k_blk = ks.reshape(B, G, n_slc, SLC_BLOCK, DH)
    v_blk = vs.reshape(B, G, n_slc, SLC_BLOCK, DH)
    n_ch = S // SLC_QCHUNK
    q_ch = jnp.moveaxis(q.reshape(B, G, HPG, n_ch, SLC_QCHUNK, DH), 3, 0)
    sel_ch = jnp.moveaxis(sel.reshape(B, G, n_ch, SLC_QCHUNK, top_n), 2, 0)
    pos_ch = pos.reshape(n_ch, SLC_QCHUNK)
    gather = jax.vmap(jax.vmap(lambda blocks, ids: blocks[ids]))
    within = jnp.arange(SLC_BLOCK)

    def slc_step(args):
        qc, ic, tc = args
        kg = gather(k_blk, ic)
        vg = gather(v_blk, ic)
        sc = jnp.einsum('bghcd,bgcnsd->bghcns', qc, kg, preferred_element_type=f32)
        kpos = ic[..., None] * SLC_BLOCK + within
        ok = (kpos <= tc[:, None, None])[:, :, None]
        p = jax.nn.softmax(jnp.where(ok, sc, NEG_INF), axis=(-2, -1))
        return jnp.einsum('bghcns,bgcnsd->bghcd', p.astype(dt), vg)

    o_slc = lax.map(slc_step, (q_ch, sel_ch, pos_ch))
    o_slc = jnp.moveaxis(o_slc, 0, 3).reshape(B, G, HPG, S, DH)

    n_wb = S // WIN_QBLOCK
    span = WINDOW + WIN_QBLOCK
    kw_pad = jnp.pad(kw, ((0, 0), (0, 0), (WINDOW, 0), (0, 0)))
    vw_pad = jnp.pad(vw, ((0, 0), (0, 0), (WINDOW, 0), (0, 0)))
    q_wb = jnp.moveaxis(q.reshape(B, G, HPG, n_wb, WIN_QBLOCK, DH), 3, 0)
    qoff = jnp.arange(WIN_QBLOCK)
    koff = jnp.arange(span) - WINDOW

    def win_step(args):
        qb, n = args
        start = n * WIN_QBLOCK
        kb = lax.dynamic_slice_in_dim(kw_pad, start, span, axis=2)
        vb = lax.dynamic_slice_in_dim(vw_pad, start, span, axis=2)
        sc = jnp.einsum('bghqd,bgkd->bghqk', qb, kb, preferred_element_type=f32)
        qp = start + qoff
        kp = start + koff
        diff = qp[:, None] - kp[None, :]
        ok = (diff >= 0) & (diff < WINDOW) & (kp >= 0)[None, :]
        p = jax.nn.softmax(jnp.where(ok, sc, NEG_INF), axis=-1)
        return jnp.einsum('bghqk,bgkd->bghqd', p.astype(dt), vb)

    o_win = lax.map(win_step, (q_wb, jnp.arange(n_wb)))
    o_win = jnp.moveaxis(o_win, 0, 3).reshape(B, G, HPG, S, DH)

    gate = jax.nn.sigmoid(gl.astype(f32)).reshape(B, S, 3, G, HPG)
    gate = jnp.moveaxis(gate, 1, -1)[..., None]
    o = gate[:, 0] * o_cmp + gate[:, 1] * o_slc + gate[:, 2] * o_win
    o = o.astype(dt).reshape(B, NSA_HEADS, S, DH).transpose(0, 2, 1, 3).reshape(B, S, NSA_QD)
    return o @ w_o


def hgrn2_mixer(h, w_in, lb, o_norm, w_o):
    B, S, _ = h.shape
    dt = h.dtype
    f32 = jnp.float32
    H, DK, DV, C = HGRN_HEADS, HGRN_KEY_DIM, HGRN_VAL_DIM, HGRN_CHUNK
    N = S // C
    q, fz, i, gz = jnp.split(h @ w_in, [H * DK, 2 * H * DK, 2 * H * DK + H * DV], axis=-1)
    f = lb + (1.0 - lb) * jax.nn.sigmoid(fz.astype(f32))

    def chunked(t, d):
        return t.astype(f32).reshape(B, N, C, H, d).transpose(0, 3, 1, 2, 4)

    qc = chunked(q, DK)
    kc = chunked(1.0 - f, DK)
    logf = chunked(jnp.log(f), DK)
    vc = chunked(i, DV)
    g_cum = jnp.cumsum(logf, axis=3)
    g_last = g_cum[:, :, :, -1:, :]
    q_dec = qc * jnp.exp(g_cum)
    k_inv = kc * jnp.exp(-g_cum)
    k_tail = kc * jnp.exp(g_last - g_cum)
    causal = jnp.tril(jnp.ones((C, C), dtype=bool))
    a = jnp.where(causal, jnp.einsum('bhnid,bhnjd->bhnij', q_dec, k_inv), 0.0)
    o_intra = jnp.einsum('bhnij,bhnjv->bhniv', a, vc)
    ds = jnp.einsum('bhncd,bhncv->bhndv', k_tail, vc)
    decay = jnp.exp(g_last[:, :, :, 0, :])

    def step(state, inp):
        d, dsn = inp
        return state * d[..., None] + dsn, state

    s0 = jnp.zeros((B, H, DK, DV), f32)
    _, s_prev = lax.scan(step, s0, (jnp.moveaxis(decay, 2, 0), jnp.moveaxis(ds, 2, 0)))
    o_inter = jnp.einsum('bhnid,bhndv->bhniv', q_dec, jnp.moveaxis(s_prev, 0, 2))
    o = (o_intra + o_inter).transpose(0, 2, 3, 1, 4).reshape(B, S, H, DV)
    o = rms_norm(o, o_norm) * jax.nn.silu(gz.astype(f32).reshape(B, S, H, DV))
    return o.reshape(B, S, H * DV).astype(dt) @ w_o


def setup_inputs(seed: int = 0) -> dict:
    key = jax.random.key(seed)
    k = jax.random.split(key, 20)
    f32 = jnp.float32

    def nrm(kk, shape, scale):
        return jax.random.normal(kk, shape, f32) * scale

    def gain(kk, shape):
        return 1.0 + 0.02 * jax.random.normal(kk, shape, f32)

    return {
        'x': nrm(k[0], (BATCH, SEQ, D_MODEL), 1.0),
        'ffn_norm': gain(k[1], (DEPTH, 2, D_MODEL)),
        'ffn_w_gate': nrm(k[2], (DEPTH, 2, D_MODEL, D_FF), D_MODEL ** -0.5),
        'ffn_w_up': nrm(k[3], (DEPTH, 2, D_MODEL, D_FF), D_MODEL ** -0.5),
        'ffn_w_down': nrm(k[4], (DEPTH, 2, D_FF, D_MODEL), D_FF ** -0.5),
        'mix_norm': gain(k[5], (DEPTH, D_MODEL)),
        'nsa_w_in': nrm(k[6], (N_NSA_LAYERS, D_MODEL, NSA_IN), D_MODEL ** -0.5),
        'nsa_q_norm': gain(k[7], (N_NSA_LAYERS, NSA_HEAD_DIM)),
        'nsa_k_norm': gain(k[8], (N_NSA_LAYERS, 3, NSA_HEAD_DIM)),
        'nsa_cmp_pos': nrm(k[9], (N_NSA_LAYERS, 2, CMP_BLOCK, NSA_HEAD_DIM), 0.1),
        'nsa_cmp_w1': nrm(k[10], (N_NSA_LAYERS, 2, CMP_BLOCK * NSA_HEAD_DIM, NSA_HEAD_DIM), (CMP_BLOCK * NSA_HEAD_DIM) ** -0.5),
        'nsa_cmp_w2': nrm(k[11], (N_NSA_LAYERS, 2, NSA_HEAD_DIM, NSA_HEAD_DIM), NSA_HEAD_DIM ** -0.5),
        'nsa_w_o': nrm(k[12], (N_NSA_LAYERS, NSA_QD, D_MODEL), NSA_QD ** -0.5),
        'hgrn_w_in': nrm(k[13], (N_HGRN_LAYERS, D_MODEL, HGRN_IN), D_MODEL ** -0.5),
        'hgrn_lb_logits': nrm(k[14], (DEPTH, HGRN_HEADS * HGRN_KEY_DIM), 0.1),
        'hgrn_o_norm': gain(k[15], (N_HGRN_LAYERS, HGRN_VAL_DIM)),
        'hgrn_w_o': nrm(k[16], (N_HGRN_LAYERS, HGRN_HEADS * HGRN_VAL_DIM, D_MODEL), (HGRN_HEADS * HGRN_VAL_DIM) ** -0.5),
    }


def reference(x, ffn_norm, ffn_w_gate, ffn_w_up, ffn_w_down, mix_norm, nsa_w_in, nsa_q_norm, nsa_k_norm,
              nsa_cmp_pos, nsa_cmp_w1, nsa_cmp_w2, nsa_w_o, hgrn_w_in, hgrn_lb_logits, hgrn_o_norm, hgrn_w_o):
    cos, sin = rope_tables(x.shape[1], NSA_HEAD_DIM)
    p_lb = jax.nn.softmax(hgrn_lb_logits.astype(jnp.float32), axis=0)
    lower_bounds = jnp.cumsum(p_lb, axis=0) - p_lb[0:1]
    for layer in range(DEPTH):
        slot = layer // N_MIXERS
        x = x + FFN_RES_WEIGHT * swiglu(rms_norm(x, ffn_norm[layer, 0]), ffn_w_gate[layer, 0],
                                        ffn_w_up[layer, 0], ffn_w_down[layer, 0])
        h = rms_norm(x, mix_norm[layer])
        if layer % N_MIXERS == 0:
            x = x + nsa_mixer(h, nsa_w_in[slot], nsa_q_norm[slot], nsa_k_norm[slot], nsa_cmp_pos[slot],
                              nsa_cmp_w1[slot], nsa_cmp_w2[slot], nsa_w_o[slot], cos, sin)
        else:
            x = x + hgrn2_mixer(h, hgrn_w_in[slot], lower_bounds[layer], hgrn_o_norm[slot], hgrn_w_o[slot])
        x = x + FFN_RES_WEIGHT * swiglu(rms_norm(x, ffn_norm[layer, 1]), ffn_w_gate[layer, 1],
                                        ffn_w_up[layer, 1], ffn_w_down[layer, 1])
    return x
```

```python
import functools
import math

import jax
import jax.numpy as jnp
import numpy as np
from jax import lax
from jax.experimental import pallas as pl
from jax.experimental.pallas import tpu as pltpu

F32 = jnp.float32
BF16 = jnp.bfloat16

RMS_EPS = 1e-6
NEG_INF = -1e30
FORCED_SCORE = 1e30
FFN_RES_WEIGHT = 0.5
N_MIXERS = 2

NSA_HEADS = 32
NSA_GROUPS = 4
CMP_BLOCK = 32
CMP_STRIDE = 16
SLC_BLOCK = 64
SLC_TOPN = 16
_SLC_SHIFT = SLC_BLOCK.bit_length() - 1
assert 1 << _SLC_SHIFT == SLC_BLOCK
WINDOW = 512
ROPE_THETA = 10000.0

HGRN_HEADS = 32
HGRN_CHUNK = 64

LANES = 128
VMEM_LIMIT_BYTES = 56 * 1024 * 1024

NT_DIMS = (((1,), (1,)), ((), ()))
TN_DIMS = (((0,), (0,)), ((), ()))


def _params(*sem):
    return pltpu.CompilerParams(dimension_semantics=sem, vmem_limit_bytes=VMEM_LIMIT_BYTES)


def _pick(n, prefs):
    for p in prefs:
        if n % p == 0:
            return p
    raise ValueError(f"no tile in {prefs} divides {n}")


def _rmsnorm_kernel(x_ref, g_ref, o_ref):
    x = x_ref[...]
    y = x * lax.rsqrt(jnp.mean(x * x, axis=-1, keepdims=True) + RMS_EPS)
    o_ref[...] = (y * g_ref[...]).astype(o_ref.dtype)


def _rmsnorm(x, g):
    m, d = x.shape
    tr = _pick(m, (256, 128, 8))
    return pl.pallas_call(
        _rmsnorm_kernel,
        grid=(m // tr,),
        in_specs=[pl.BlockSpec((tr, d), lambda i: (i, 0)), pl.BlockSpec((1, d), lambda i: (0, 0))],
        out_specs=pl.BlockSpec((tr, d), lambda i: (i, 0)),
        out_shape=jax.ShapeDtypeStruct((m, d), BF16),
        compiler_params=_params("parallel"),
        name="rmsnorm",
    )(x, g.reshape(1, d))


def _gateup_kernel(h_ref, wg_ref, wu_ref, o_ref):
    h = h_ref[...]
    a = jnp.dot(h, wg_ref[...], preferred_element_type=F32)
    b = jnp.dot(h, wu_ref[...], preferred_element_type=F32)
    o_ref[...] = (a * jax.nn.sigmoid(a) * b).astype(o_ref.dtype)


def _gateup(h, wg, wu):
    m, k = h.shape
    n = wg.shape[1]
    bm = _pick(m, (1024, 512, 256, 128))
    bn = _pick(n, (256, 128))
    return pl.pallas_call(
        _gateup_kernel,
        grid=(m // bm, n // bn),
        in_specs=[
            pl.BlockSpec((bm, k), lambda i, j: (i, 0)),
            pl.BlockSpec((k, bn), lambda i, j: (0, j)),
            pl.BlockSpec((k, bn), lambda i, j: (0, j)),
        ],
        out_specs=pl.BlockSpec((bm, bn), lambda i, j: (i, j)),
        out_shape=jax.ShapeDtypeStruct((m, n), BF16),
        compiler_params=_params("parallel", "arbitrary"),
        name="ffn_gateup",
    )(h, wg, wu)


def _mm_res_kernel(a_ref, w_ref, r_ref, o_ref, *, scale):
    acc = jnp.dot(a_ref[...], w_ref[...], preferred_element_type=F32)
    o_ref[...] = r_ref[...] + scale * acc


def _mm_res(a, w, res, scale, *, bm_prefs, bn_prefs, name):
    m, k = a.shape
    n = w.shape[1]
    bm = _pick(m, bm_prefs)
    bn = _pick(n, bn_prefs)
    return pl.pallas_call(
        functools.partial(_mm_res_kernel, scale=scale),
        grid=(m // bm, n // bn),
        in_specs=[
            pl.BlockSpec((bm, k), lambda i, j: (i, 0)),
            pl.BlockSpec((k, bn), lambda i, j: (0, j)),
            pl.BlockSpec((bm, bn), lambda i, j: (i, j)),
        ],
        out_specs=pl.BlockSpec((bm, bn), lambda i, j: (i, j)),
        out_shape=jax.ShapeDtypeStruct((m, n), F32),
        compiler_params=_params("parallel", "arbitrary"),
        name=name,
    )(a, w, res)


def _mm_kernel(a_ref, w_ref, o_ref, *, sigmoid):
    acc = jnp.dot(a_ref[...], w_ref[...], preferred_element_type=F32)
    if sigmoid:
        acc = jax.nn.sigmoid(acc)
    o_ref[...] = acc.astype(o_ref.dtype)


def _mm(a, w, *, out_dtype, sigmoid=False, name):
    m, k = a.shape
    n = w.shape[1]
    bm = _pick(m, (1024, 512, 256, 128))
    bn = _pick(n, (512, 256, 128))
    return pl.pallas_call(
        functools.partial(_mm_kernel, sigmoid=sigmoid),
        grid=(m // bm, n // bn),
        in_specs=[pl.BlockSpec((bm, k), lambda i, j: (i, 0)), pl.BlockSpec((k, bn), lambda i, j: (0, j))],
        out_specs=pl.BlockSpec((bm, bn), lambda i, j: (i, j)),
        out_shape=jax.ShapeDtypeStruct((m, n), out_dtype),
        compiler_params=_params("parallel", "arbitrary"),
        name=name,
    )(a, w)


def _ffn(x, norm_g, wg, wu, wd):
    h = _rmsnorm(x, norm_g)
    act = _gateup(h, wg, wu)
    return _mm_res(act, wd, x, FFN_RES_WEIGHT, bm_prefs=(512, 256, 128), bn_prefs=(256, 128), name="ffn_down")


def _head_norm_rope(x, gain, cos, sin_signed):
    y = x * lax.rsqrt(jnp.mean(x * x, axis=-1, keepdims=True) + RMS_EPS) * gain
    return y * cos + pltpu.roll(y, LANES // 2, 1) * sin_signed


def _nsa_prep_kernel(p_ref, cos_ref, sin_ref, qn_ref, kn_ref, q_ref, kc_ref, vc_ref, ks_ref, vs_ref, kw_ref, vw_ref,
                     *, heads, groups):
    cos = cos_ref[...]
    sin = sin_ref[...]
    scale = LANES ** -0.5
    qn = qn_ref[...]
    for h in range(heads):
        x = p_ref[:, h * LANES:(h + 1) * LANES]
        q_ref[:, h * LANES:(h + 1) * LANES] = (_head_norm_rope(x, qn, cos, sin) * scale).astype(q_ref.dtype)
    base = heads * LANES
    kv_w = groups * LANES
    for br, (k_out, v_out) in enumerate(((kc_ref, vc_ref), (ks_ref, vs_ref), (kw_ref, vw_ref))):
        kn = kn_ref[br:br + 1, :]
        for g in range(groups):
            ck = base + (2 * br) * kv_w + g * LANES
            cv = base + (2 * br + 1) * kv_w + g * LANES
            k_out[g] = _head_norm_rope(p_ref[:, ck:ck + LANES], kn, cos, sin).astype(k_out.dtype)
            v_out[g] = p_ref[:, cv:cv + LANES].astype(v_out.dtype)


def _nsa_prep(proj, cos, sin_signed, q_norm, k_norm, *, heads, groups):
    b, s, n = proj.shape
    ts = _pick(s, (256, 128))
    kv_spec = pl.BlockSpec((None, groups, ts, LANES), lambda bi, i: (bi, 0, i, 0))
    tab_spec = pl.BlockSpec((ts, LANES), lambda bi, i: (i, 0))

    def kv_shape(dt):
        return jax.ShapeDtypeStruct((b, groups, s, LANES), dt)

    return pl.pallas_call(
        functools.partial(_nsa_prep_kernel, heads=heads, groups=groups),
        grid=(b, s // ts),
        in_specs=[
            pl.BlockSpec((None, ts, n), lambda bi, i: (bi, i, 0)),
            tab_spec,
            tab_spec,
            pl.BlockSpec((1, LANES), lambda bi, i: (0, 0)),
            pl.BlockSpec((3, LANES), lambda bi, i: (0, 0)),
        ],
        out_specs=[pl.BlockSpec((None, ts, heads * LANES), lambda bi, i: (bi, i, 0))] + [kv_spec] * 6,
        out_shape=[jax.ShapeDtypeStruct((b, s, heads * LANES), BF16),
                   kv_shape(F32), kv_shape(F32), kv_shape(BF16), kv_shape(BF16), kv_shape(BF16), kv_shape(BF16)],
        compiler_params=_params("parallel", "parallel"),
        name="nsa_prep",
    )(proj, cos, sin_signed, q_norm.reshape(1, LANES), k_norm)


def _gelu_tanh(x):
    return x * (0.5 * (1.0 + jnp.tanh(math.sqrt(2.0 / math.pi) * (x + 0.044715 * (x * x * x)))))


def _compress_kernel(k_ref, v_ref, pe_ref, w1_ref, w2_ref, ko_ref, vo_ref):
    for which, (x_ref, o_ref) in enumerate(((k_ref, ko_ref), (v_ref, vo_ref))):
        x = x_ref[...]
        nseg = x.shape[0]
        top = jnp.dot((x + pe_ref[which, 0]).astype(BF16), w1_ref[which, 0], preferred_element_type=F32)
        bot = jnp.dot((x + pe_ref[which, 1]).astype(BF16), w1_ref[which, 1], preferred_element_type=F32)
        pre = top + pltpu.roll(bot, nseg - 1, 0)
        hid = _gelu_tanh(pre).astype(BF16)
        o_ref[...] = jnp.dot(hid, w2_ref[which], preferred_element_type=F32).astype(o_ref.dtype)


def _nsa_compress(kcf, vcf, pe, w1, w2):
    b, g, nseg, width = kcf.shape
    x_spec = pl.BlockSpec((None, None, nseg, width), lambda bi, gi: (bi, gi, 0, 0))
    o_spec = pl.BlockSpec((None, None, nseg, LANES), lambda bi, gi: (bi, gi, 0, 0))
    o_shape = jax.ShapeDtypeStruct((b, g, nseg, LANES), BF16)
    return pl.pallas_call(
        _compress_kernel,
        grid=(b, g),
        in_specs=[
            x_spec, x_spec,
            pl.BlockSpec((2, 2, 1, width), lambda bi, gi: (0, 0, 0, 0)),
            pl.BlockSpec((2, 2, width, LANES), lambda bi, gi: (0, 0, 0, 0)),
            pl.BlockSpec((2, LANES, LANES), lambda bi, gi: (0, 0, 0)),
        ],
        out_specs=[o_spec, o_spec],
        out_shape=[o_shape, o_shape],
        compiler_params=_params("parallel", "parallel"),
        name="nsa_compress",
    )(kcf, vcf, pe, w1, w2)


def _stack_heads(q, hpg):
    return jnp.concatenate([q[:, h * LANES:(h + 1) * LANES] for h in range(hpg)], axis=0)


def _cmp_attn_kernel(q_ref, kc_ref, vc_ref, g_ref, ovl_ref, o_ref, sel_ref, *, tq, hpg, n_cmp, n_slc, top_n):
    i = pl.program_id(2)
    q8 = _stack_heads(q_ref[...], hpg)
    rows = hpg * tq
    ncp = kc_ref.shape[0]
    s = lax.dot_general(q8, kc_ref[...], NT_DIMS, preferred_element_type=F32)
    tpos = i * tq + lax.broadcasted_iota(jnp.int32, (tq, 1), 0)
    tpos8 = jnp.concatenate([tpos] * hpg, axis=0)
    ncol = lax.broadcasted_iota(jnp.int32, (1, ncp), 1)
    valid = jnp.where((ncol * CMP_STRIDE + (CMP_BLOCK - 1) <= tpos8) & (ncol < n_cmp), 1.0, 0.0)
    sm = jnp.where(valid > 0.5, s, NEG_INF)
    m = jnp.max(sm, axis=-1, keepdims=True)
    e = jnp.exp(sm - m) * valid
    l = jnp.sum(e, axis=-1, keepdims=True)
    p = e / jnp.where(l > 0.0, l, 1.0)
    o = jnp.dot(p.astype(BF16), vc_ref[...], preferred_element_type=F32)
    gates = g_ref[...]
    for h in range(hpg):
        o_ref[:, h * LANES:(h + 1) * LANES] = o[h * tq:(h + 1) * tq] * gates[:, h:h + 1]

    psum = p[0:tq]
    for h in range(1, hpg):
        psum = psum + p[h * tq:(h + 1) * tq]
    hi = psum.astype(BF16)
    lo = (psum - hi.astype(F32)).astype(BF16)
    ovl = ovl_ref[...]
    imp = (lax.dot_general(ovl, hi, NT_DIMS, preferred_element_type=F32)
           + lax.dot_general(ovl, lo, NT_DIMS, preferred_element_type=F32))
    jidx = lax.broadcasted_iota(jnp.int32, (n_slc, tq), 0)
    tq_pos = i * tq + lax.broadcasted_iota(jnp.int32, (n_slc, tq), 1)
    cur = lax.shift_right_logical(tq_pos, _SLC_SHIFT)
    forced = (jidx == 0) | (jidx == cur) | (jidx == cur - 1)
    causal = jidx * SLC_BLOCK <= tq_pos
    imp = jnp.where(forced, FORCED_SCORE, jnp.where(causal, imp, NEG_INF))
    sel_rows = []
    for j in range(n_slc):
        row = imp[j:j + 1, :]
        lower = jnp.where(jidx < j, 1.0, 0.0)
        beats = jnp.where(imp > row, 1.0, jnp.where(imp == row, lower, 0.0))
        rank = jnp.sum(beats, axis=0, keepdims=True)
        sel_rows.append(jnp.where(rank < top_n, 1.0, 0.0))
    sel_rows.append(jnp.zeros((LANES - n_slc, tq), F32))
    sel_t = jnp.concatenate(sel_rows, axis=0)
    sel_ref[...] = sel_t.T.astype(sel_ref.dtype)


def _nsa_cmp_attn(q, kcmp, vcmp, gates, ovl, *, groups, hpg, n_cmp, n_slc, top_n):
    b, s, d = q.shape
    ncp = kcmp.shape[2]
    tq = 128
    qo_spec = pl.BlockSpec((None, tq, hpg * LANES), lambda bi, gi, i: (bi, i, gi))
    c_spec = pl.BlockSpec((None, None, ncp, LANES), lambda bi, gi, i: (bi, gi, 0, 0))
    return pl.pallas_call(
        functools.partial(_cmp_attn_kernel, tq=tq, hpg=hpg, n_cmp=n_cmp, n_slc=n_slc, top_n=top_n),
        grid=(b, groups, s // tq),
        in_specs=[
            qo_spec, c_spec, c_spec,
            pl.BlockSpec((None, tq, LANES), lambda bi, gi, i: (bi, i, gi)),
            pl.BlockSpec((n_slc, ncp), lambda bi, gi, i: (0, 0)),
        ],
        out_specs=[qo_spec, pl.BlockSpec((None, None, tq, LANES), lambda bi, gi, i: (bi, gi, i, 0))],
        out_shape=[jax.ShapeDtypeStruct((b, s, d), F32), jax.ShapeDtypeStruct((b, groups, s, LANES), BF16)],
        compiler_params=_params("parallel", "parallel", "parallel"),
        name="nsa_cmp_attn",
    )(q, kcmp, vcmp, gates, ovl)


def _sel_attn_kernel(q_ref, k_ref, v_ref, sel_ref, g_ref, oin_ref, o_ref, *, tq, tk, hpg):
    i = pl.program_id(2)
    q8 = _stack_heads(q_ref[...], hpg)
    rows = hpg * tq
    sel = sel_ref[...]
    tpos = i * tq + lax.broadcasted_iota(jnp.int32, (tq, 1), 0)
    blk_row = lax.broadcasted_iota(jnp.int32, (LANES, 1), 0)
    n_kv = ((i + 1) * tq + tk - 1) // tk

    def body(kv, carry):
        m, l, acc = carry
        start = pl.multiple_of(kv * tk, tk)
        k = k_ref[pl.ds(start, tk), :]
        v = v_ref[pl.ds(start, tk), :]
        s = lax.dot_general(q8, k, NT_DIMS, preferred_element_type=F32)
        kpos = start + lax.broadcasted_iota(jnp.int32, (1, tk), 1)
        expand = jnp.where(blk_row == lax.shift_right_logical(kpos, _SLC_SHIFT), 1.0, 0.0).astype(BF16)
        picked = jnp.dot(sel, expand, preferred_element_type=F32)
        bias = jnp.where((picked > 0.5) & (kpos <= tpos), 0.0, NEG_INF)
        s = s + jnp.concatenate([bias] * hpg, axis=0)
        m_new = jnp.maximum(m, jnp.max(s, axis=-1, keepdims=True))
        alpha = jnp.exp(m - m_new)
        p = jnp.exp(s - m_new)
        l = alpha * l + jnp.sum(p, axis=-1, keepdims=True)
        acc = alpha * acc + jnp.dot(p.astype(BF16), v, preferred_element_type=F32)
        return m_new, l, acc

    init = (jnp.full((rows, 1), NEG_INF, F32), jnp.zeros((rows, 1), F32), jnp.zeros((rows, LANES), F32))
    _, l, acc = lax.fori_loop(0, n_kv, body, init)
    o = acc / l
    gates = g_ref[...]
    for h in range(hpg):
        cols = slice(h * LANES, (h + 1) * LANES)
        o_ref[:, cols] = oin_ref[:, cols] + o[h * tq:(h + 1) * tq] * gates[:, hpg + h:hpg + h + 1]


def _nsa_sel_attn(q, ks, vs, selm, gates, o_in, *, groups, hpg):
    b, s, d = q.shape
    tq = 128
    tk = _pick(s, (256, 128))
    qo_spec = pl.BlockSpec((None, tq, hpg * LANES), lambda bi, gi, i: (bi, i, gi))
    kv_spec = pl.BlockSpec((None, None, s, LANES), lambda bi, gi, i: (bi, gi, 0, 0))
    return pl.pallas_call(
        functools.partial(_sel_attn_kernel, tq=tq, tk=tk, hpg=hpg),
        grid=(b, groups, s // tq),
        in_specs=[
            qo_spec, kv_spec, kv_spec,
            pl.BlockSpec((None, None, tq, LANES), lambda bi, gi, i: (bi, gi, i, 0)),
            pl.BlockSpec((None, tq, LANES), lambda bi, gi, i: (bi, i, gi)),
            qo_spec,
        ],
        out_specs=qo_spec,
        out_shape=jax.ShapeDtypeStruct((b, s, d), F32),
        compiler_params=_params("parallel", "parallel", "parallel"),
        name="nsa_sel_attn",
    )(q, ks, vs, selm, gates, o_in)


def _win_attn_kernel(q_ref, k_ref, v_ref, g_ref, oin_ref, o_ref, *, tq, span, hpg):
    i = pl.program_id(2)
    q8 = _stack_heads(q_ref[...], hpg)
    start = pl.multiple_of(jnp.maximum(i * tq - WINDOW, 0), tq)
    k = k_ref[pl.ds(start, span), :]
    v = v_ref[pl.ds(start, span), :]
    s = lax.dot_general(q8, k, NT_DIMS, preferred_element_type=F32)
    tpos = i * tq + lax.broadcasted_iota(jnp.int32, (tq, 1), 0)
    kpos = start + lax.broadcasted_iota(jnp.int32, (1, span), 1)
    diff = tpos - kpos
    bias = jnp.where((diff >= 0) & (diff < WINDOW), 0.0, NEG_INF)
    s = s + jnp.concatenate([bias] * hpg, axis=0)
    m = jnp.max(s, axis=-1, keepdims=True)
    p = jnp.exp(s - m)
    l = jnp.sum(p, axis=-1, keepdims=True)
    o = jnp.dot(p.astype(BF16), v, preferred_element_type=F32) / l
    gates = g_ref[...]
    for h in range(hpg):
        cols = slice(h * LANES, (h + 1) * LANES)
        gate = gates[:, 2 * hpg + h:2 * hpg + h + 1]
        o_ref[:, cols] = (oin_ref[:, cols] + o[h * tq:(h + 1) * tq] * gate).astype(o_ref.dtype)


def _nsa_win_attn(q, kw, vw, gates, o_in, *, groups, hpg):
    b, s, d = q.shape
    tq = 128
    span = WINDOW + tq
    assert s >= span and WINDOW % tq == 0
    qo_spec = pl.BlockSpec((None, tq, hpg * LANES), lambda bi, gi, i: (bi, i, gi))
    kv_spec = pl.BlockSpec((None, None, s, LANES), lambda bi, gi, i: (bi, gi, 0, 0))
    return pl.pallas_call(
        functools.partial(_win_attn_kernel, tq=tq, span=span, hpg=hpg),
        grid=(b, groups, s // tq),
        in_specs=[qo_spec, kv_spec, kv_spec, pl.BlockSpec((None, tq, LANES), lambda bi, gi, i: (bi, i, gi)), qo_spec],
        out_specs=qo_spec,
        out_shape=jax.ShapeDtypeStruct((b, s, d), BF16),
        compiler_params=_params("parallel", "parallel", "parallel"),
        name="nsa_win_attn",
    )(q, kw, vw, gates, o_in)


def _rope_tables(seq, dim):
    inv = ROPE_THETA ** (-jnp.arange(0, dim, 2, dtype=F32) / dim)
    ang = jnp.arange(seq, dtype=F32)[:, None] * inv[None, :]
    ang = jnp.concatenate([ang, ang], axis=-1)
    sign = jnp.where(jnp.arange(dim) < dim // 2, -1.0, 1.0).astype(F32)
    return jnp.cos(ang), jnp.sin(ang) * sign


def _nsa_mixer(x2, h, b, s, w_in, q_norm, k_norm, cmp_pos, cmp_w1, cmp_w2, w_o):
    heads, groups = NSA_HEADS, NSA_GROUPS
    hpg = heads // groups
    d = x2.shape[1]
    assert d == heads * LANES and CMP_BLOCK == 2 * CMP_STRIDE and 3 * hpg <= LANES
    qd, kvd = heads * LANES, groups * LANES
    n_main = qd + 6 * kvd
    nseg = s // CMP_STRIDE
    n_cmp = (s - CMP_BLOCK) // CMP_STRIDE + 1
    n_slc = s // SLC_BLOCK
    top_n = min(SLC_TOPN, n_slc)
    assert n_cmp == nseg - 1 and n_slc % 8 == 0 and n_slc <= LANES and top_n >= 3

    proj = _mm(h, w_in[:, :n_main].astype(BF16), out_dtype=F32, name="nsa_in_proj")
    wg = w_in[:, n_main:].reshape(d, 3, groups, hpg).transpose(0, 2, 1, 3).reshape(d, groups, 3 * hpg)
    wg = jnp.pad(wg, ((0, 0), (0, 0), (0, LANES - 3 * hpg))).reshape(d, groups * LANES).astype(BF16)
    gates = _mm(h, wg, out_dtype=F32, sigmoid=True, name="nsa_gate_proj").reshape(b, s, groups * LANES)

    cos, sin_signed = _rope_tables(s, LANES)
    q, kc, vc, ks, vs, kw, vw = _nsa_prep(proj.reshape(b, s, n_main), cos, sin_signed, q_norm, k_norm,
                                          heads=heads, groups=groups)

    seg_w = CMP_STRIDE * LANES
    pe = cmp_pos.reshape(2, 2, 1, seg_w)
    w1 = cmp_w1.reshape(2, 2, seg_w, LANES).astype(BF16)
    kcmp, vcmp = _nsa_compress(kc.reshape(b, groups, nseg, seg_w), vc.reshape(b, groups, nseg, seg_w),
                               pe, w1, cmp_w2.astype(BF16))

    cmp_start = np.arange(nseg) * CMP_STRIDE
    slc_start = np.arange(n_slc) * SLC_BLOCK
    ovl = ((cmp_start[None, :] < slc_start[:, None] + SLC_BLOCK) & (cmp_start[None, :] + CMP_BLOCK > slc_start[:, None])
           & (np.arange(nseg)[None, :] < n_cmp))
    ovl = jnp.asarray(ovl.astype(np.float32), dtype=BF16)

    o1, selm = _nsa_cmp_attn(q, kcmp, vcmp, gates, ovl, groups=groups, hpg=hpg, n_cmp=n_cmp, n_slc=n_slc, top_n=top_n)
    o2 = _nsa_sel_attn(q, ks, vs, selm, gates, o1, groups=groups, hpg=hpg)
    o3 = _nsa_win_attn(q, kw, vw, gates, o2, groups=groups, hpg=hpg)
    return _mm_res(o3.reshape(b * s, d), w_o.astype(BF16), x2, 1.0,
                   bm_prefs=(1024, 512, 256, 128), bn_prefs=(512, 256, 128), name="nsa_out_proj")


def _hgrn_kernel(q_ref, f_ref, i_ref, g_ref, lbl_ref, on_ref, o_ref, *, seq, chunk, layer):
    lbl = lbl_ref[...]
    e = jnp.exp(lbl - jnp.max(lbl, axis=0, keepdims=True))
    p = e / jnp.sum(e, axis=0, keepdims=True)
    csum = p[0:1]
    for d in range(1, layer + 1):
        csum = csum + p[d:d + 1]
    lb = csum - p[0:1]
    o_gain = on_ref[...]
    r_i = lax.broadcasted_iota(jnp.int32, (chunk, chunk), 0)
    c_i = lax.broadcasted_iota(jnp.int32, (chunk, chunk), 1)
    causal = r_i >= c_i
    tri = jnp.where(causal, 1.0, 0.0).astype(BF16)

    def body(n, state_t):
        rows = pl.ds(pl.multiple_of(n * chunk, chunk), chunk)
        qc = q_ref[rows, :]
        vc = i_ref[rows, :].astype(BF16)
        gz = g_ref[rows, :]
        f = lb + (1.0 - lb) * jax.nn.sigmoid(f_ref[rows, :])
        kc = 1.0 - f
        logf = jnp.log(f)
        h1 = logf.astype(BF16)
        r1 = logf - h1.astype(F32)
        h2 = r1.astype(BF16)
        h3 = (r1 - h2.astype(F32)).astype(BF16)
        g_cum = (jnp.dot(tri, h1, preferred_element_type=F32) + jnp.dot(tri, h2, preferred_element_type=F32)
                 + jnp.dot(tri, h3, preferred_element_type=F32))
        g_last = g_cum[chunk - 1:chunk, :]
        q_dec = (qc * jnp.exp(g_cum)).astype(BF16)
        k_inv = (kc * jnp.exp(-g_cum)).astype(BF16)
        k_tail = (kc * jnp.exp(g_last - g_cum)).astype(BF16)
        a = lax.dot_general(q_dec, k_inv, NT_DIMS, preferred_element_type=F32)
        a = jnp.where(causal, a, 0.0).astype(BF16)
        o = jnp.dot(a, vc, preferred_element_type=F32)
        o = o + lax.dot_general(q_dec, state_t.astype(BF16), NT_DIMS, preferred_element_type=F32)
        ds_t = lax.dot_general(vc, k_tail, TN_DIMS, preferred_element_type=F32)
        y = o * lax.rsqrt(jnp.mean(o * o, axis=-1, keepdims=True) + RMS_EPS) * o_gain
        o_ref[rows, :] = (y * (gz * jax.nn.sigmoid(gz))).astype(o_ref.dtype)
        return state_t * jnp.exp(g_last) + ds_t

    lax.fori_loop(0, seq // chunk, body, jnp.zeros((LANES, LANES), F32))


def _hgrn_core(proj, lb_logits, o_norm, *, heads, layer):
    b, s, _ = proj.shape
    depth = lb_logits.shape[0]

    def col_spec(part):
        return pl.BlockSpec((None, s, LANES), lambda bi, hi: (bi, 0, part * heads + hi))

    return pl.pallas_call(
        functools.partial(_hgrn_kernel, seq=s, chunk=HGRN_CHUNK, layer=layer),
        grid=(b, heads),
        in_specs=[col_spec(0), col_spec(1), col_spec(2), col_spec(3),
                  pl.BlockSpec((depth, LANES), lambda bi, hi: (0, hi)),
                  pl.BlockSpec((1, LANES), lambda bi, hi: (0, 0))],
        out_specs=pl.BlockSpec((None, s, LANES), lambda bi, hi: (bi, 0, hi)),
        out_shape=jax.ShapeDtypeStruct((b, s, heads * LANES), BF16),
        compiler_params=_params("parallel", "parallel"),
        name="hgrn_core",
    )(proj, proj, proj, proj, lb_logits, o_norm.reshape(1, LANES))


def _hgrn_mixer(x2, h, b, s, w_in, lb_logits, o_norm, w_o, layer):
    heads = HGRN_HEADS
    d = x2.shape[1]
    assert d == heads * LANES and w_in.shape[1] == 4 * d and s % HGRN_CHUNK == 0
    proj = _mm(h, w_in.astype(BF16), out_dtype=F32, name="hgrn_in_proj")
    o = _hgrn_core(proj.reshape(b, s, 4 * d), lb_logits, o_norm, heads=heads, layer=layer)
    return _mm_res(o.reshape(b * s, d), w_o.astype(BF16), x2, 1.0,
                   bm_prefs=(1024, 512, 256, 128), bn_prefs=(512, 256, 128), name="hgrn_out_proj")


def kernel(x, ffn_norm, ffn_w_gate, ffn_w_up, ffn_w_down, mix_norm, nsa_w_in, nsa_q_norm, nsa_k_norm, nsa_cmp_pos,
           nsa_cmp_w1, nsa_cmp_w2, nsa_w_o, hgrn_w_in, hgrn_lb_logits, hgrn_o_norm, hgrn_w_o):
    b, s, d = x.shape
    depth = ffn_norm.shape[0]
    x2 = x.reshape(b * s, d)
    for layer in range(depth):
        slot = layer // N_MIXERS
        x2 = _ffn(x2, ffn_norm[layer, 0], ffn_w_gate[layer, 0].astype(BF16), ffn_w_up[layer, 0].astype(BF16),
                  ffn_w_down[layer, 0].astype(BF16))
        h = _rmsnorm(x2, mix_norm[layer])
        if layer % N_MIXERS == 0:
            x2 = _nsa_mixer(x2, h, b, s, nsa_w_in[slot], nsa_q_norm[slot], nsa_k_norm[slot], nsa_cmp_pos[slot],
                            nsa_cmp_w1[slot], nsa_cmp_w2[slot], nsa_w_o[slot])
        else:
            x2 = _hgrn_mixer(x2, h, b, s, hgrn_w_in[slot], hgrn_lb_logits, hgrn_o_norm[slot], hgrn_w_o[slot], layer)
        x2 = _ffn(x2, ffn_norm[layer, 1], ffn_w_gate[layer, 1].astype(BF16), ffn_w_up[layer, 1].astype(BF16),
                  ffn_w_down[layer, 1].astype(BF16))
    return x2.reshape(b, s, d)
```

```python
import functools
import math

import jax
import jax.numpy as jnp
import numpy as np
from jax import lax
from jax.experimental import pallas as pl
from jax.experimental.pallas import tpu as pltpu

F32 = jnp.float32
BF16 = jnp.bfloat16

RMS_EPS = 1e-6
NEG_INF = -1e30
FORCED_SCORE = 1e30
FFN_RES_WEIGHT = 0.5
N_MIXERS = 2

NSA_HEADS = 32
NSA_GROUPS = 4
CMP_BLOCK = 32
CMP_STRIDE = 16
SLC_BLOCK = 64
SLC_TOPN = 16
_SLC_SHIFT = SLC_BLOCK.bit_length() - 1
assert 1 << _SLC_SHIFT == SLC_BLOCK
WINDOW = 512
ROPE_THETA = 10000.0

HGRN_HEADS = 32
HGRN_CHUNK = 64

LANES = 128
VMEM_LIMIT_BYTES = 56 * 1024 * 1024

NT_DIMS = (((1,), (1,)), ((), ()))
TN_DIMS = (((0,), (0,)), ((), ()))


def _params(*sem):
    return pltpu.CompilerParams(dimension_semantics=sem, vmem_limit_bytes=VMEM_LIMIT_BYTES)


def _pick(n, prefs):
    for p in prefs:
        if n % p == 0:
            return p
    raise ValueError(f"no tile in {prefs} divides {n}")


def _rmsnorm_kernel(x_ref, g_ref, o_ref):
    x = x_ref[...]
    y = x * lax.rsqrt(jnp.mean(x * x, axis=-1, keepdims=True) + RMS_EPS)
    o_ref[...] = (y * g_ref[...]).astype(o_ref.dtype)


def _rmsnorm(x, g):
    m, d = x.shape
    tr = _pick(m, (256, 128, 8))
    return pl.pallas_call(
        _rmsnorm_kernel,
        grid=(m // tr,),
        in_specs=[pl.BlockSpec((tr, d), lambda i: (i, 0)), pl.BlockSpec((1, d), lambda i: (0, 0))],
        out_specs=pl.BlockSpec((tr, d), lambda i: (i, 0)),
        out_shape=jax.ShapeDtypeStruct((m, d), BF16),
        compiler_params=_params("parallel"),
        name="rmsnorm",
    )(x, g.reshape(1, d))


def _weight_spec(w, lead, bn):
    k = w.shape[-2]
    return pl.BlockSpec((None,) * len(lead) + (k, bn), lambda i, j: lead + (0, j))


def _lhs_spec(bm, k, single_buffer):
    if single_buffer:
        return pl.BlockSpec((bm, k), lambda i, j: (i, 0), pipeline_mode=pl.Buffered(1))
    return pl.BlockSpec((bm, k), lambda i, j: (i, 0))


def _gateup_kernel(h_ref, wg_ref, wu_ref, o_ref):
    h = h_ref[...]
    a = jnp.dot(h, wg_ref[...].astype(BF16), preferred_element_type=F32)
    b = jnp.dot(h, wu_ref[...].astype(BF16), preferred_element_type=F32)
    o_ref[...] = (a * jax.nn.sigmoid(a) * b).astype(o_ref.dtype)


def _gateup(h, wg, wu, lead):
    m, k = h.shape
    n = wg.shape[-1]
    bm = _pick(m, (2048, 1024, 512, 256, 128))
    bn = _pick(n, (256, 128))
    return pl.pallas_call(
        _gateup_kernel,
        grid=(m // bm, n // bn),
        in_specs=[_lhs_spec(bm, k, True), _weight_spec(wg, lead, bn), _weight_spec(wu, lead, bn)],
        out_specs=pl.BlockSpec((bm, bn), lambda i, j: (i, j)),
        out_shape=jax.ShapeDtypeStruct((m, n), BF16),
        compiler_params=_params("parallel", "arbitrary"),
        name="ffn_gateup",
    )(h, wg, wu)


def _mm_res_kernel(a_ref, w_ref, r_ref, o_ref, *, scale):
    acc = jnp.dot(a_ref[...], w_ref[...].astype(BF16), preferred_element_type=F32)
    o_ref[...] = r_ref[...] + scale * acc


def _mm_res(a, w, res, scale, *, lead=(), bm_prefs, bn_prefs, single_buffer_lhs=False, name):
    m, k = a.shape
    n = w.shape[-1]
    bm = _pick(m, bm_prefs)
    bn = _pick(n, bn_prefs)
    return pl.pallas_call(
        functools.partial(_mm_res_kernel, scale=scale),
        grid=(m // bm, n // bn),
        in_specs=[_lhs_spec(bm, k, single_buffer_lhs), _weight_spec(w, lead, bn),
                  pl.BlockSpec((bm, bn), lambda i, j: (i, j))],
        out_specs=pl.BlockSpec((bm, bn), lambda i, j: (i, j)),
        out_shape=jax.ShapeDtypeStruct((m, n), F32),
        compiler_params=_params("parallel", "arbitrary"),
        name=name,
    )(a, w, res)


def _mm_kernel(a_ref, w_ref, o_ref, *, sigmoid):
    acc = jnp.dot(a_ref[...], w_ref[...].astype(BF16), preferred_element_type=F32)
    if sigmoid:
        acc = jax.nn.sigmoid(acc)
    o_ref[...] = acc.astype(o_ref.dtype)


def _mm(a, w, *, lead=(), n=None, out_dtype, sigmoid=False, name):
    m, k = a.shape
    n = w.shape[-1] if n is None else n
    bm = _pick(m, (1024, 512, 256, 128))
    bn = _pick(n, (512, 256, 128))
    return pl.pallas_call(
        functools.partial(_mm_kernel, sigmoid=sigmoid),
        grid=(m // bm, n // bn),
        in_specs=[_lhs_spec(bm, k, False), _weight_spec(w, lead, bn)],
        out_specs=pl.BlockSpec((bm, bn), lambda i, j: (i, j)),
        out_shape=jax.ShapeDtypeStruct((m, n), out_dtype),
        compiler_params=_params("parallel", "arbitrary"),
        name=name,
    )(a, w)


def _ffn(x, norm_g, w_gate, w_up, w_down_bf16, lead):
    h = _rmsnorm(x, norm_g)
    act = _gateup(h, w_gate, w_up, lead)
    return _mm_res(act, w_down_bf16, x, FFN_RES_WEIGHT, lead=lead, bm_prefs=(1024, 512, 256, 128),
                   bn_prefs=(512, 256, 128), single_buffer_lhs=True, name="ffn_down")


def _head_norm_rope(x, gain, cos, sin_signed):
    y = x * lax.rsqrt(jnp.mean(x * x, axis=-1, keepdims=True) + RMS_EPS) * gain
    return y * cos + pltpu.roll(y, LANES // 2, 1) * sin_signed


def _nsa_prep_kernel(p_ref, cos_ref, sin_ref, qn_ref, kn_ref, q_ref, kc_ref, vc_ref, ks_ref, vs_ref, kw_ref, vw_ref,
                     *, heads, groups):
    cos = cos_ref[...]
    sin = sin_ref[...]
    scale = LANES ** -0.5
    qn = qn_ref[...]
    for h in range(heads):
        x = p_ref[:, h * LANES:(h + 1) * LANES]
        q_ref[:, h * LANES:(h + 1) * LANES] = (_head_norm_rope(x, qn, cos, sin) * scale).astype(q_ref.dtype)
    base = heads * LANES
    kv_w = groups * LANES
    ts = p_ref.shape[0]
    pos = pl.program_id(1) * ts + lax.broadcasted_iota(jnp.int32, (ts, LANES), 0)
    blk_onehot = jnp.where(lax.broadcasted_iota(jnp.int32, (ts, LANES), 1) == jnp.right_shift(pos, _SLC_SHIFT),
                           1.0, 0.0).astype(ks_ref.dtype)
    for br, (k_out, v_out) in enumerate(((kc_ref, vc_ref), (ks_ref, vs_ref), (kw_ref, vw_ref))):
        kn = kn_ref[br:br + 1, :]
        for g in range(groups):
            ck = base + (2 * br) * kv_w + g * LANES
            cv = base + (2 * br + 1) * kv_w + g * LANES
            k = _head_norm_rope(p_ref[:, ck:ck + LANES], kn, cos, sin).astype(k_out.dtype)
            v = p_ref[:, cv:cv + LANES].astype(v_out.dtype)
            if br == 1:
                k_out[g, :, :LANES] = k
                k_out[g, :, LANES:] = blk_onehot
                v_out[g, :, :LANES] = v
                v_out[g, :, LANES:] = jnp.ones_like(v)
            else:
                k_out[g] = k
                v_out[g] = v


def _nsa_prep(proj, cos, sin_signed, q_norm, k_norm, *, heads, groups):
    b, s, n = proj.shape
    ts = _pick(s, (256, 128))
    kv_spec = pl.BlockSpec((None, groups, ts, LANES), lambda bi, i: (bi, 0, i, 0))
    aug_spec = pl.BlockSpec((None, groups, ts, 2 * LANES), lambda bi, i: (bi, 0, i, 0))
    tab_spec = pl.BlockSpec((ts, LANES), lambda bi, i: (i, 0))

    def kv_shape(dt, width=LANES):
        return jax.ShapeDtypeStruct((b, groups, s, width), dt)

    return pl.pallas_call(
        functools.partial(_nsa_prep_kernel, heads=heads, groups=groups),
        grid=(b, s // ts),
        in_specs=[
            pl.BlockSpec((None, ts, n), lambda bi, i: (bi, i, 0)),
            tab_spec,
            tab_spec,
            pl.BlockSpec((1, LANES), lambda bi, i: (0, 0)),
            pl.BlockSpec((3, LANES), lambda bi, i: (0, 0)),
        ],
        out_specs=[pl.BlockSpec((None, ts, heads * LANES), lambda bi, i: (bi, i, 0)),
                   kv_spec, kv_spec, aug_spec, aug_spec, kv_spec, kv_spec],
        out_shape=[jax.ShapeDtypeStruct((b, s, heads * LANES), BF16),
                   kv_shape(F32), kv_shape(F32), kv_shape(BF16, 2 * LANES), kv_shape(BF16, 2 * LANES),
                   kv_shape(BF16), kv_shape(BF16)],
        compiler_params=_params("parallel", "parallel"),
        name="nsa_prep",
    )(proj, cos, sin_signed, q_norm.reshape(1, LANES), k_norm)


def _gelu_tanh(x):
    return x * (0.5 * (1.0 + jnp.tanh(math.sqrt(2.0 / math.pi) * (x + 0.044715 * (x * x * x)))))


def _compress_kernel(k_ref, v_ref, pe_ref, w1_ref, w2_ref, ko_ref, vo_ref):
    for which, (x_ref, o_ref) in enumerate(((k_ref, ko_ref), (v_ref, vo_ref))):
        x = x_ref[...]
        nseg = x.shape[0]
        top = jnp.dot((x + pe_ref[which, 0]).astype(BF16), w1_ref[which, 0], preferred_element_type=F32)
        bot = jnp.dot((x + pe_ref[which, 1]).astype(BF16), w1_ref[which, 1], preferred_element_type=F32)
        pre = top + pltpu.roll(bot, nseg - 1, 0)
        hid = _gelu_tanh(pre).astype(BF16)
        o_ref[...] = jnp.dot(hid, w2_ref[which], preferred_element_type=F32).astype(o_ref.dtype)


def _nsa_compress(kcf, vcf, pe, w1, w2):
    b, g, nseg, width = kcf.shape
    x_spec = pl.BlockSpec((None, None, nseg, width), lambda bi, gi: (bi, gi, 0, 0))
    o_spec = pl.BlockSpec((None, None, nseg, LANES), lambda bi, gi: (bi, gi, 0, 0))
    o_shape = jax.ShapeDtypeStruct((b, g, nseg, LANES), BF16)
    return pl.pallas_call(
        _compress_kernel,
        grid=(b, g),
        in_specs=[
            x_spec, x_spec,
            pl.BlockSpec((2, 2, 1, width), lambda bi, gi: (0, 0, 0, 0)),
            pl.BlockSpec((2, 2, width, LANES), lambda bi, gi: (0, 0, 0, 0)),
            pl.BlockSpec((2, LANES, LANES), lambda bi, gi: (0, 0, 0)),
        ],
        out_specs=[o_spec, o_spec],
        out_shape=[o_shape, o_shape],
        compiler_params=_params("parallel", "parallel"),
        name="nsa_compress",
    )(kcf, vcf, pe, w1, w2)


def _stack_heads(q, hpg):
    return jnp.concatenate([q[:, h * LANES:(h + 1) * LANES] for h in range(hpg)], axis=0)


def _cmp_attn_kernel(q_ref, kc_ref, vc_ref, g_ref, ovl_ref, o_ref, sel_ref, *, tq, hpg, n_cmp, n_slc, top_n):
    i = pl.program_id(2)
    q8 = _stack_heads(q_ref[...], hpg)
    rows = hpg * tq
    ncp = kc_ref.shape[0]
    s = lax.dot_general(q8, kc_ref[...], NT_DIMS, preferred_element_type=F32)
    tpos = i * tq + lax.broadcasted_iota(jnp.int32, (tq, 1), 0)
    tpos8 = jnp.concatenate([tpos] * hpg, axis=0)
    ncol = lax.broadcasted_iota(jnp.int32, (1, ncp), 1)
    valid = jnp.where((ncol * CMP_STRIDE + (CMP_BLOCK - 1) <= tpos8) & (ncol < n_cmp), 1.0, 0.0)
    sm = jnp.where(valid > 0.5, s, NEG_INF)
    m = jnp.max(sm, axis=-1, keepdims=True)
    e = jnp.exp(sm - m) * valid
    l = jnp.sum(e, axis=-1, keepdims=True)
    p = e / jnp.where(l > 0.0, l, 1.0)
    o = jnp.dot(p.astype(BF16), vc_ref[...], preferred_element_type=F32)
    gates = g_ref[...]
    for h in range(hpg):
        o_ref[:, h * LANES:(h + 1) * LANES] = o[h * tq:(h + 1) * tq] * gates[:, h:h + 1]

    psum = p[0:tq]
    for h in range(1, hpg):
        psum = psum + p[h * tq:(h + 1) * tq]
    hi = psum.astype(BF16)
    lo = (psum - hi.astype(F32)).astype(BF16)
    ovl = ovl_ref[...]
    imp = (lax.dot_general(ovl, hi, NT_DIMS, preferred_element_type=F32)
           + lax.dot_general(ovl, lo, NT_DIMS, preferred_element_type=F32))
    jidx = lax.broadcasted_iota(jnp.int32, (n_slc, tq), 0)
    tq_pos = i * tq + lax.broadcasted_iota(jnp.int32, (n_slc, tq), 1)
    cur = jnp.right_shift(tq_pos, _SLC_SHIFT)
    forced = (jidx == 0) | (jidx == cur) | (jidx == cur - 1)
    causal = jidx * SLC_BLOCK <= tq_pos
    imp = jnp.where(forced, FORCED_SCORE, jnp.where(causal, imp, NEG_INF))
    sel_rows = []
    for j in range(n_slc):
        row = imp[j:j + 1, :]
        lower = jnp.where(jidx < j, 1.0, 0.0)
        beats = jnp.where(imp > row, 1.0, jnp.where(imp == row, lower, 0.0))
        rank = jnp.sum(beats, axis=0, keepdims=True)
        sel_rows.append(jnp.where(rank < top_n, 0.0, NEG_INF))
    sel_rows.append(jnp.zeros((LANES - n_slc, tq), F32))
    bias_t = jnp.concatenate(sel_rows, axis=0)
    sel_ref[...] = bias_t.T.astype(sel_ref.dtype)


def _nsa_cmp_attn(q, kcmp, vcmp, gates, ovl, *, groups, hpg, n_cmp, n_slc, top_n):
    b, s, d = q.shape
    ncp = kcmp.shape[2]
    tq = 128
    qo_spec = pl.BlockSpec((None, tq, hpg * LANES), lambda bi, gi, i: (bi, i, gi))
    c_spec = pl.BlockSpec((None, None, ncp, LANES), lambda bi, gi, i: (bi, gi, 0, 0))
    return pl.pallas_call(
        functools.partial(_cmp_attn_kernel, tq=tq, hpg=hpg, n_cmp=n_cmp, n_slc=n_slc, top_n=top_n),
        grid=(b, groups, s // tq),
        in_specs=[
            qo_spec, c_spec, c_spec,
            pl.BlockSpec((None, tq, LANES), lambda bi, gi, i: (bi, i, gi)),
            pl.BlockSpec((n_slc, ncp), lambda bi, gi, i: (0, 0)),
        ],
        out_specs=[qo_spec, pl.BlockSpec((None, None, tq, LANES), lambda bi, gi, i: (bi, gi, i, 0))],
        out_shape=[jax.ShapeDtypeStruct((b, s, d), F32), jax.ShapeDtypeStruct((b, groups, s, LANES), BF16)],
        compiler_params=_params("parallel", "parallel", "parallel"),
        name="nsa_cmp_attn",
    )(q, kcmp, vcmp, gates, ovl)


def _sel_attn_kernel(q_ref, k_ref, v_ref, sel_ref, g_ref, oin_ref, o_ref, *, tq, tk, hpg):
    i = pl.program_id(2)
    q = q_ref[...]
    bias = sel_ref[...]
    q8 = jnp.concatenate([jnp.concatenate([q[:, h * LANES:(h + 1) * LANES], bias], axis=1) for h in range(hpg)],
                         axis=0)
    rows = hpg * tq
    tpos = i * tq + lax.broadcasted_iota(jnp.int32, (tq, 1), 0)
    n_kv = ((i + 1) * tq + tk - 1) // tk

    def tile(kv, carry, diagonal):
        m, acc = carry
        start = pl.multiple_of(kv * tk, tk)
        s = lax.dot_general(q8, k_ref[pl.ds(start, tk), :], NT_DIMS, preferred_element_type=F32)
        if diagonal:
            kpos = start + lax.broadcasted_iota(jnp.int32, (1, tk), 1)
            s = s + jnp.concatenate([jnp.where(kpos <= tpos, 0.0, NEG_INF)] * hpg, axis=0)
        m_new = jnp.maximum(m, jnp.max(s, axis=-1, keepdims=True))
        p = jnp.exp(s - m_new).astype(BF16)
        acc = jnp.exp(m - m_new) * acc + jnp.dot(p, v_ref[pl.ds(start, tk), :], preferred_element_type=F32)
        return m_new, acc

    init = (jnp.full((rows, 1), NEG_INF, F32), jnp.zeros((rows, 2 * LANES), F32))
    carry = lax.fori_loop(0, n_kv - 1, lambda kv, c: tile(kv, c, False), init)
    _, acc = tile(n_kv - 1, carry, True)
    o = acc[:, :LANES] / acc[:, LANES:]
    gates = g_ref[...]
    for h in range(hpg):
        cols = slice(h * LANES, (h + 1) * LANES)
        o_ref[:, cols] = oin_ref[:, cols] + o[h * tq:(h + 1) * tq] * gates[:, hpg + h:hpg + h + 1]


def _nsa_sel_attn(q, ks, vs, selm, gates, o_in, *, groups, hpg):
    b, s, d = q.shape
    tq = 128
    tk = _pick(s, (512, 256, 128))
    qo_spec = pl.BlockSpec((None, tq, hpg * LANES), lambda bi, gi, i: (bi, i, gi))
    kv_spec = pl.BlockSpec((None, None, s, 2 * LANES), lambda bi, gi, i: (bi, gi, 0, 0))
    return pl.pallas_call(
        functools.partial(_sel_attn_kernel, tq=tq, tk=tk, hpg=hpg),
        grid=(b, groups, s // tq),
        in_specs=[
            qo_spec, kv_spec, kv_spec,
            pl.BlockSpec((None, None, tq, LANES), lambda bi, gi, i: (bi, gi, i, 0)),
            pl.BlockSpec((None, tq, LANES), lambda bi, gi, i: (bi, i, gi)),
            qo_spec,
        ],
        out_specs=qo_spec,
        out_shape=jax.ShapeDtypeStruct((b, s, d), F32),
        compiler_params=_params("parallel", "parallel", "parallel"),
        name="nsa_sel_attn",
    )(q, ks, vs, selm, gates, o_in)


def _win_attn_kernel(q_ref, k_ref, v_ref, g_ref, oin_ref, o_ref, *, tq, span, hpg):
    i = pl.program_id(2)
    q8 = _stack_heads(q_ref[...], hpg)
    start = pl.multiple_of(jnp.maximum(i * tq - WINDOW, 0), tq)
    k = k_ref[pl.ds(start, span), :]
    v = v_ref[pl.ds(start, span), :]
    s = lax.dot_general(q8, k, NT_DIMS, preferred_element_type=F32)
    tpos = i * tq + lax.broadcasted_iota(jnp.int32, (tq, 1), 0)
    kpos = start + lax.broadcasted_iota(jnp.int32, (1, span), 1)
    diff = tpos - kpos
    bias = jnp.where((diff >= 0) & (diff < WINDOW), 0.0, NEG_INF)
    s = s + jnp.concatenate([bias] * hpg, axis=0)
    m = jnp.max(s, axis=-1, keepdims=True)
    p = jnp.exp(s - m)
    l = jnp.sum(p, axis=-1, keepdims=True)
    o = jnp.dot(p.astype(BF16), v, preferred_element_type=F32) / l
    gates = g_ref[...]
    for h in range(hpg):
        cols = slice(h * LANES, (h + 1) * LANES)
        gate = gates[:, 2 * hpg + h:2 * hpg + h + 1]
        o_ref[:, cols] = (oin_ref[:, cols] + o[h * tq:(h + 1) * tq] * gate).astype(o_ref.dtype)


def _nsa_win_attn(q, kw, vw, gates, o_in, *, groups, hpg):
    b, s, d = q.shape
    tq = 128
    span = WINDOW + tq
    assert s >= span and WINDOW % tq == 0
    qo_spec = pl.BlockSpec((None, tq, hpg * LANES), lambda bi, gi, i: (bi, i, gi))
    kv_spec = pl.BlockSpec((None, None, s, LANES), lambda bi, gi, i: (bi, gi, 0, 0))
    return pl.pallas_call(
        functools.partial(_win_attn_kernel, tq=tq, span=span, hpg=hpg),
        grid=(b, groups, s // tq),
        in_specs=[qo_spec, kv_spec, kv_spec, pl.BlockSpec((None, tq, LANES), lambda bi, gi, i: (bi, i, gi)), qo_spec],
        out_specs=qo_spec,
        out_shape=jax.ShapeDtypeStruct((b, s, d), BF16),
        compiler_params=_params("parallel", "parallel", "parallel"),
        name="nsa_win_attn",
    )(q, kw, vw, gates, o_in)


def _rope_tables(seq, dim):
    inv = ROPE_THETA ** (-jnp.arange(0, dim, 2, dtype=F32) / dim)
    ang = jnp.arange(seq, dtype=F32)[:, None] * inv[None, :]
    ang = jnp.concatenate([ang, ang], axis=-1)
    sign = jnp.where(jnp.arange(dim) < dim // 2, -1.0, 1.0).astype(F32)
    return jnp.cos(ang), jnp.sin(ang) * sign


def _nsa_mixer(x2, h, b, s, w_in, q_norm, k_norm, cmp_pos, cmp_w1, cmp_w2, w_o):
    heads, groups = NSA_HEADS, NSA_GROUPS
    hpg = heads // groups
    d = x2.shape[1]
    assert d == heads * LANES and CMP_BLOCK == 2 * CMP_STRIDE and 3 * hpg <= LANES
    qd, kvd = heads * LANES, groups * LANES
    n_main = qd + 6 * kvd
    nseg = s // CMP_STRIDE
    n_cmp = (s - CMP_BLOCK) // CMP_STRIDE + 1
    n_slc = s // SLC_BLOCK
    top_n = min(SLC_TOPN, n_slc)
    assert n_cmp == nseg - 1 and n_slc % 8 == 0 and n_slc <= LANES and top_n >= 3

    proj = _mm(h, w_in, n=n_main, out_dtype=F32, name="nsa_in_proj")
    wg = w_in[:, n_main:].reshape(d, 3, groups, hpg).transpose(0, 2, 1, 3).reshape(d, groups, 3 * hpg)
    wg = jnp.pad(wg, ((0, 0), (0, 0), (0, LANES - 3 * hpg))).reshape(d, groups * LANES).astype(BF16)
    gates = _mm(h, wg, out_dtype=F32, sigmoid=True, name="nsa_gate_proj").reshape(b, s, groups * LANES)

    cos, sin_signed = _rope_tables(s, LANES)
    q, kc, vc, ks, vs, kw, vw = _nsa_prep(proj.reshape(b, s, n_main), cos, sin_signed, q_norm, k_norm,
                                          heads=heads, groups=groups)

    seg_w = CMP_STRIDE * LANES
    pe = cmp_pos.reshape(2, 2, 1, seg_w)
    w1 = cmp_w1.reshape(2, 2, seg_w, LANES).astype(BF16)
    kcmp, vcmp = _nsa_compress(kc.reshape(b, groups, nseg, seg_w), vc.reshape(b, groups, nseg, seg_w),
                               pe, w1, cmp_w2.astype(BF16))

    cmp_start = np.arange(nseg) * CMP_STRIDE
    slc_start = np.arange(n_slc) * SLC_BLOCK
    ovl = ((cmp_start[None, :] < slc_start[:, None] + SLC_BLOCK) & (cmp_start[None, :] + CMP_BLOCK > slc_start[:, None])
           & (np.arange(nseg)[None, :] < n_cmp))
    ovl = jnp.asarray(ovl.astype(np.float32), dtype=BF16)

    o1, selm = _nsa_cmp_attn(q, kcmp, vcmp, gates, ovl, groups=groups, hpg=hpg, n_cmp=n_cmp, n_slc=n_slc, top_n=top_n)
    o2 = _nsa_sel_attn(q, ks, vs, selm, gates, o1, groups=groups, hpg=hpg)
    o3 = _nsa_win_attn(q, kw, vw, gates, o2, groups=groups, hpg=hpg)
    return _mm_res(o3.reshape(b * s, d), w_o, x2, 1.0,
                   bm_prefs=(1024, 512, 256, 128), bn_prefs=(512, 256, 128), name="nsa_out_proj")


def _chunk_cumsum(x, row):
    n = x.shape[0]
    step = 1
    while step < n:
        x = x + jnp.where(row >= step, pltpu.roll(x, step, 0), 0.0)
        step *= 2
    return x


def _hgrn_kernel(q_ref, f_ref, i_ref, g_ref, lbl_ref, on_ref, o_ref, *, seq, chunk, layer, hps, unroll):
    lbl = lbl_ref[...]
    e = jnp.exp(lbl - jnp.max(lbl, axis=0, keepdims=True))
    p = e / jnp.sum(e, axis=0, keepdims=True)
    csum = p[0:1]
    for d in range(1, layer + 1):
        csum = csum + p[d:d + 1]
    lb = csum - p[0:1]
    o_gain = on_ref[...]
    causal = (lax.broadcasted_iota(jnp.int32, (chunk, chunk), 0) >= lax.broadcasted_iota(jnp.int32, (chunk, chunk), 1))
    row = lax.broadcasted_iota(jnp.int32, (chunk, hps * LANES), 0)
    heads = [slice(k * LANES, (k + 1) * LANES) for k in range(hps)]

    def body(it, states):
        states = list(states)
        chunks = []
        for u in range(unroll):
            rows = pl.ds(pl.multiple_of((it * unroll + u) * chunk, chunk), chunk)
            f = lb + (1.0 - lb) * jax.nn.sigmoid(f_ref[rows, :])
            kc = 1.0 - f
            g_cum = _chunk_cumsum(jnp.log(f), row)
            g_last = g_cum[chunk - 1:chunk, :]
            q_dec = (q_ref[rows, :] * jnp.exp(g_cum)).astype(BF16)
            k_inv = (kc * jnp.exp(-g_cum)).astype(BF16)
            k_tail = (kc * jnp.exp(g_last - g_cum)).astype(BF16)
            chunks.append((rows, q_dec, k_inv, k_tail, i_ref[rows, :].astype(BF16), jnp.exp(g_last)))
        a_all = [[lax.dot_general(q_dec[:, h], k_inv[:, h], NT_DIMS, preferred_element_type=F32) for h in heads]
                 for (_, q_dec, k_inv, _, _, _) in chunks]
        ds_all = [[lax.dot_general(v[:, h], k_tail[:, h], TN_DIMS, preferred_element_type=F32) for h in heads]
                  for (_, _, _, k_tail, v, _) in chunks]
        o_all = [[jnp.dot(jnp.where(causal, a, 0.0).astype(BF16), v[:, h], preferred_element_type=F32)
                  for a, h in zip(a_row, heads)]
                 for a_row, (_, _, _, _, v, _) in zip(a_all, chunks)]
        for u, (rows, q_dec, _, _, _, decay) in enumerate(chunks):
            for k, h in enumerate(heads):
                o = o_all[u][k] + lax.dot_general(q_dec[:, h], states[k].astype(BF16), NT_DIMS,
                                                  preferred_element_type=F32)
                states[k] = states[k] * decay[:, h] + ds_all[u][k]
                y = o * lax.rsqrt(jnp.mean(o * o, axis=-1, keepdims=True) + RMS_EPS) * o_gain
                gz = g_ref[rows, h]
                o_ref[rows, h] = (y * (gz * jax.nn.sigmoid(gz))).astype(o_ref.dtype)
        return tuple(states)

    init = tuple(jnp.zeros((LANES, LANES), F32) for _ in range(hps))
    lax.fori_loop(0, seq // (chunk * unroll), body, init)


def _hgrn_core(proj, lb_logits, o_norm, *, heads, layer, hps=4, unroll=2):
    b, s, _ = proj.shape
    depth = lb_logits.shape[0]
    hps = math.gcd(hps, heads)
    width = hps * LANES
    groups = heads // hps
    assert s % (HGRN_CHUNK * unroll) == 0

    def col_spec(part):
        return pl.BlockSpec((None, s, width), lambda bi, hi: (bi, 0, part * groups + hi))

    return pl.pallas_call(
        functools.partial(_hgrn_kernel, seq=s, chunk=HGRN_CHUNK, layer=layer, hps=hps, unroll=unroll),
        grid=(b, groups),
        in_specs=[col_spec(0), col_spec(1), col_spec(2), col_spec(3),
                  pl.BlockSpec((depth, width), lambda bi, hi: (0, hi)),
                  pl.BlockSpec((1, LANES), lambda bi, hi: (0, 0))],
        out_specs=pl.BlockSpec((None, s, width), lambda bi, hi: (bi, 0, hi)),
        out_shape=jax.ShapeDtypeStruct((b, s, heads * LANES), BF16),
        compiler_params=_params("parallel", "parallel"),
        name="hgrn_core",
    )(proj, proj, proj, proj, lb_logits, o_norm.reshape(1, LANES))


def _hgrn_mixer(x2, h, b, s, w_in, lb_logits, o_norm, w_o, layer):
    heads = HGRN_HEADS
    d = x2.shape[1]
    assert d == heads * LANES and w_in.shape[1] == 4 * d and s % HGRN_CHUNK == 0
    proj = _mm(h, w_in, out_dtype=F32, name="hgrn_in_proj")
    o = _hgrn_core(proj.reshape(b, s, 4 * d), lb_logits, o_norm, heads=heads, layer=layer)
    return _mm_res(o.reshape(b * s, d), w_o, x2, 1.0,
                   bm_prefs=(1024, 512, 256, 128), bn_prefs=(512, 256, 128), name="hgrn_out_proj")


def kernel(x, ffn_norm, ffn_w_gate, ffn_w_up, ffn_w_down, mix_norm, nsa_w_in, nsa_q_norm, nsa_k_norm, nsa_cmp_pos,
           nsa_cmp_w1, nsa_cmp_w2, nsa_w_o, hgrn_w_in, hgrn_lb_logits, hgrn_o_norm, hgrn_w_o):
    b, s, d = x.shape
    depth = ffn_norm.shape[0]
    x2 = x.reshape(b * s, d)
    w_down = ffn_w_down.astype(BF16)
    for layer in range(depth):
        slot = layer // N_MIXERS
        x2 = _ffn(x2, ffn_norm[layer, 0], ffn_w_gate, ffn_w_up, w_down, (layer, 0))
        h = _rmsnorm(x2, mix_norm[layer])
        if layer % N_MIXERS == 0:
            x2 = _nsa_mixer(x2, h, b, s, nsa_w_in[slot], nsa_q_norm[slot], nsa_k_norm[slot], nsa_cmp_pos[slot],
                            nsa_cmp_w1[slot], nsa_cmp_w2[slot], nsa_w_o[slot])
        else:
            x2 = _hgrn_mixer(x2, h, b, s, hgrn_w_in[slot], hgrn_lb_logits, hgrn_o_norm[slot], hgrn_w_o[slot], layer)
        x2 = _ffn(x2, ffn_norm[layer, 1], ffn_w_gate, ffn_w_up, w_down, (layer, 1))
    return x2.reshape(b, s, d)
```

```python
import functools
import math

import jax
import jax.numpy as jnp
import numpy as np
from jax import lax
from jax.experimental import pallas as pl
from jax.experimental.pallas import tpu as pltpu

F32 = jnp.float32
BF16 = jnp.bfloat16

RMS_EPS = 1e-6
NEG_INF = -1e30
FORCED_SCORE = 1e30
FFN_RES_WEIGHT = 0.5
N_MIXERS = 2

NSA_HEADS = 32
NSA_GROUPS = 4
CMP_BLOCK = 32
CMP_STRIDE = 16
SLC_BLOCK = 64
SLC_TOPN = 16
_SLC_SHIFT = SLC_BLOCK.bit_length() - 1
assert 1 << _SLC_SHIFT == SLC_BLOCK
WINDOW = 512
ROPE_THETA = 10000.0

HGRN_HEADS = 32
HGRN_CHUNK = 64

LANES = 128
SOFTMAX_FLOOR = 1e-30
BOUND_SLACK = 1.02
VMEM_LIMIT_BYTES = 56 * 1024 * 1024

NT_DIMS = (((1,), (1,)), ((), ()))
TN_DIMS = (((0,), (0,)), ((), ()))


def _params(*sem):
    return pltpu.CompilerParams(dimension_semantics=sem, vmem_limit_bytes=VMEM_LIMIT_BYTES)


def _pick(n, prefs):
    for p in prefs:
        if n % p == 0:
            return p
    raise ValueError(f"no tile in {prefs} divides {n}")


def _rmsnorm_kernel(x_ref, g_ref, o_ref):
    x = x_ref[...]
    y = x * lax.rsqrt(jnp.mean(x * x, axis=-1, keepdims=True) + RMS_EPS)
    o_ref[...] = (y * g_ref[...]).astype(o_ref.dtype)


def _rmsnorm(x, g):
    m, d = x.shape
    tr = _pick(m, (256, 128, 8))
    return pl.pallas_call(
        _rmsnorm_kernel,
        grid=(m // tr,),
        in_specs=[pl.BlockSpec((tr, d), lambda i: (i, 0)), pl.BlockSpec((1, d), lambda i: (0, 0))],
        out_specs=pl.BlockSpec((tr, d), lambda i: (i, 0)),
        out_shape=jax.ShapeDtypeStruct((m, d), BF16),
        compiler_params=_params("parallel"),
        name="rmsnorm",
    )(x, g.reshape(1, d))


def _weight_spec(w, lead, bn):
    k = w.shape[-2]
    return pl.BlockSpec((None,) * len(lead) + (k, bn), lambda i, j: lead + (0, j))


def _lhs_spec(bm, k, single_buffer):
    if single_buffer:
        return pl.BlockSpec((bm, k), lambda i, j: (i, 0), pipeline_mode=pl.Buffered(1))
    return pl.BlockSpec((bm, k), lambda i, j: (i, 0))


def _gateup_kernel(h_ref, wg_ref, wu_ref, o_ref):
    h = h_ref[...]
    a = jnp.dot(h, wg_ref[...].astype(BF16), preferred_element_type=F32)
    b = jnp.dot(h, wu_ref[...].astype(BF16), preferred_element_type=F32)
    o_ref[...] = (a * jax.nn.sigmoid(a) * b).astype(o_ref.dtype)


def _gateup(h, wg, wu, lead):
    m, k = h.shape
    n = wg.shape[-1]
    bm = _pick(m, (2048, 1024, 512, 256, 128))
    bn = _pick(n, (256, 128))
    return pl.pallas_call(
        _gateup_kernel,
        grid=(m // bm, n // bn),
        in_specs=[_lhs_spec(bm, k, True), _weight_spec(wg, lead, bn), _weight_spec(wu, lead, bn)],
        out_specs=pl.BlockSpec((bm, bn), lambda i, j: (i, j)),
        out_shape=jax.ShapeDtypeStruct((m, n), BF16),
        compiler_params=_params("parallel", "arbitrary"),
        name="ffn_gateup",
    )(h, wg, wu)


def _mm_res_kernel(a_ref, w_ref, r_ref, o_ref, *, scale):
    acc = jnp.dot(a_ref[...], w_ref[...].astype(BF16), preferred_element_type=F32)
    o_ref[...] = r_ref[...] + scale * acc


def _mm_res(a, w, res, scale, *, lead=(), bm_prefs, bn_prefs, single_buffer_lhs=False, name):
    m, k = a.shape
    n = w.shape[-1]
    bm = _pick(m, bm_prefs)
    bn = _pick(n, bn_prefs)
    return pl.pallas_call(
        functools.partial(_mm_res_kernel, scale=scale),
        grid=(m // bm, n // bn),
        in_specs=[_lhs_spec(bm, k, single_buffer_lhs), _weight_spec(w, lead, bn),
                  pl.BlockSpec((bm, bn), lambda i, j: (i, j))],
        out_specs=pl.BlockSpec((bm, bn), lambda i, j: (i, j)),
        out_shape=jax.ShapeDtypeStruct((m, n), F32),
        compiler_params=_params("parallel", "arbitrary"),
        name=name,
    )(a, w, res)


def _mm_kernel(a_ref, w_ref, o_ref, *, sigmoid):
    acc = jnp.dot(a_ref[...], w_ref[...].astype(BF16), preferred_element_type=F32)
    if sigmoid:
        acc = jax.nn.sigmoid(acc)
    o_ref[...] = acc.astype(o_ref.dtype)


def _mm(a, w, *, lead=(), n=None, out_dtype, sigmoid=False, name):
    m, k = a.shape
    n = w.shape[-1] if n is None else n
    bm = _pick(m, (1024, 512, 256, 128))
    bn = _pick(n, (512, 256, 128))
    return pl.pallas_call(
        functools.partial(_mm_kernel, sigmoid=sigmoid),
        grid=(m // bm, n // bn),
        in_specs=[_lhs_spec(bm, k, False), _weight_spec(w, lead, bn)],
        out_specs=pl.BlockSpec((bm, bn), lambda i, j: (i, j)),
        out_shape=jax.ShapeDtypeStruct((m, n), out_dtype),
        compiler_params=_params("parallel", "arbitrary"),
        name=name,
    )(a, w)


def _ffn(x, norm_g, w_gate, w_up, w_down_bf16, lead):
    h = _rmsnorm(x, norm_g)
    act = _gateup(h, w_gate, w_up, lead)
    return _mm_res(act, w_down_bf16, x, FFN_RES_WEIGHT, lead=lead, bm_prefs=(512, 256, 128),
                   bn_prefs=(512, 256, 128), name="ffn_down")


def _head_norm_rope(x, gain, cos, sin_signed):
    y = x * lax.rsqrt(jnp.mean(x * x, axis=-1, keepdims=True) + RMS_EPS) * gain
    return y * cos + pltpu.roll(y, LANES // 2, 1) * sin_signed


def _nsa_prep_kernel(p_ref, cos_ref, sin_ref, qn_ref, kn_ref, q_ref, kc_ref, vc_ref, ks_ref, vs_ref, kw_ref, vw_ref,
                     *, heads, groups):
    cos = cos_ref[...]
    sin = sin_ref[...]
    scale = LANES ** -0.5
    qn = qn_ref[...]
    for h in range(heads):
        x = p_ref[:, h * LANES:(h + 1) * LANES]
        q_ref[:, h * LANES:(h + 1) * LANES] = (_head_norm_rope(x, qn, cos, sin) * scale).astype(q_ref.dtype)
    base = heads * LANES
    kv_w = groups * LANES
    ts = p_ref.shape[0]
    pos = pl.program_id(1) * ts + lax.broadcasted_iota(jnp.int32, (ts, LANES), 0)
    lane = lax.broadcasted_iota(jnp.int32, (ts, LANES), 1)
    blk_onehot = jnp.where(lane == jnp.right_shift(pos, _SLC_SHIFT), 1.0, 0.0).astype(ks_ref.dtype)
    lane0_onehot = jnp.where(lane == 0, 1.0, 0.0).astype(kw_ref.dtype)
    for br, (k_out, v_out) in enumerate(((kc_ref, vc_ref), (ks_ref, vs_ref), (kw_ref, vw_ref))):
        kn = kn_ref[br:br + 1, :]
        for g in range(groups):
            ck = base + (2 * br) * kv_w + g * LANES
            cv = base + (2 * br + 1) * kv_w + g * LANES
            k = _head_norm_rope(p_ref[:, ck:ck + LANES], kn, cos, sin).astype(k_out.dtype)
            v = p_ref[:, cv:cv + LANES].astype(v_out.dtype)
            if br == 0:
                k_out[g] = k
                v_out[g] = v
            else:
                k_out[g, :, :LANES] = k
                k_out[g, :, LANES:] = blk_onehot if br == 1 else lane0_onehot
                v_out[g, :, :LANES] = v
                v_out[g, :, LANES:] = jnp.ones_like(v)


def _nsa_prep(proj, cos, sin_signed, q_norm, k_norm, *, heads, groups):
    b, s, n = proj.shape
    ts = _pick(s, (256, 128))
    kv_spec = pl.BlockSpec((None, groups, ts, LANES), lambda bi, i: (bi, 0, i, 0))
    aug_spec = pl.BlockSpec((None, groups, ts, 2 * LANES), lambda bi, i: (bi, 0, i, 0))
    tab_spec = pl.BlockSpec((ts, LANES), lambda bi, i: (i, 0))

    def kv_shape(dt, width=LANES):
        return jax.ShapeDtypeStruct((b, groups, s, width), dt)

    return pl.pallas_call(
        functools.partial(_nsa_prep_kernel, heads=heads, groups=groups),
        grid=(b, s // ts),
        in_specs=[
            pl.BlockSpec((None, ts, n), lambda bi, i: (bi, i, 0)),
            tab_spec,
            tab_spec,
            pl.BlockSpec((1, LANES), lambda bi, i: (0, 0)),
            pl.BlockSpec((3, LANES), lambda bi, i: (0, 0)),
        ],
        out_specs=[pl.BlockSpec((None, ts, heads * LANES), lambda bi, i: (bi, i, 0)),
                   kv_spec, kv_spec, aug_spec, aug_spec, aug_spec, aug_spec],
        out_shape=[jax.ShapeDtypeStruct((b, s, heads * LANES), BF16),
                   kv_shape(F32), kv_shape(F32), kv_shape(BF16, 2 * LANES), kv_shape(BF16, 2 * LANES),
                   kv_shape(BF16, 2 * LANES), kv_shape(BF16, 2 * LANES)],
        compiler_params=_params("parallel", "parallel"),
        name="nsa_prep",
    )(proj, cos, sin_signed, q_norm.reshape(1, LANES), k_norm)


def _gelu_tanh(x):
    return x * (0.5 * (1.0 + jnp.tanh(math.sqrt(2.0 / math.pi) * (x + 0.044715 * (x * x * x)))))


def _compress_kernel(k_ref, v_ref, pe_ref, w1_ref, w2_ref, ko_ref, vo_ref):
    for which, (x_ref, o_ref) in enumerate(((k_ref, ko_ref), (v_ref, vo_ref))):
        x = x_ref[...]
        nseg = x.shape[0]
        top = jnp.dot((x + pe_ref[which, 0]).astype(BF16), w1_ref[which, 0], preferred_element_type=F32)
        bot = jnp.dot((x + pe_ref[which, 1]).astype(BF16), w1_ref[which, 1], preferred_element_type=F32)
        pre = top + pltpu.roll(bot, nseg - 1, 0)
        hid = _gelu_tanh(pre).astype(BF16)
        o_ref[...] = jnp.dot(hid, w2_ref[which], preferred_element_type=F32).astype(o_ref.dtype)


def _nsa_compress(kcf, vcf, pe, w1, w2):
    b, g, nseg, width = kcf.shape
    x_spec = pl.BlockSpec((None, None, nseg, width), lambda bi, gi: (bi, gi, 0, 0))
    o_spec = pl.BlockSpec((None, None, nseg, LANES), lambda bi, gi: (bi, gi, 0, 0))
    o_shape = jax.ShapeDtypeStruct((b, g, nseg, LANES), BF16)
    return pl.pallas_call(
        _compress_kernel,
        grid=(b, g),
        in_specs=[
            x_spec, x_spec,
            pl.BlockSpec((2, 2, 1, width), lambda bi, gi: (0, 0, 0, 0)),
            pl.BlockSpec((2, 2, width, LANES), lambda bi, gi: (0, 0, 0, 0)),
            pl.BlockSpec((2, LANES, LANES), lambda bi, gi: (0, 0, 0)),
        ],
        out_specs=[o_spec, o_spec],
        out_shape=[o_shape, o_shape],
        compiler_params=_params("parallel", "parallel"),
        name="nsa_compress",
    )(kcf, vcf, pe, w1, w2)


def _stack_heads(q, hpg):
    return jnp.concatenate([q[:, h * LANES:(h + 1) * LANES] for h in range(hpg)], axis=0)


def _cmp_attn_kernel(q_ref, kc_ref, vc_ref, g_ref, ovl_ref, o_ref, sel_ref, *, tq, hpg, n_cmp, n_slc, top_n):
    i = pl.program_id(2)
    q8 = _stack_heads(q_ref[...], hpg)
    rows = hpg * tq
    ncp = kc_ref.shape[0]
    s = lax.dot_general(q8, kc_ref[...], NT_DIMS, preferred_element_type=F32)
    tpos = i * tq + lax.broadcasted_iota(jnp.int32, (tq, 1), 0)
    tpos8 = jnp.concatenate([tpos] * hpg, axis=0)
    ncol = lax.broadcasted_iota(jnp.int32, (1, ncp), 1)
    valid = jnp.where((ncol * CMP_STRIDE + (CMP_BLOCK - 1) <= tpos8) & (ncol < n_cmp), 1.0, 0.0)
    sm = jnp.where(valid > 0.5, s, NEG_INF)
    m = jnp.max(sm, axis=-1, keepdims=True)
    e = jnp.exp(sm - m) * valid
    l = jnp.sum(e, axis=-1, keepdims=True)
    p = e / jnp.where(l > 0.0, l, 1.0)
    o = jnp.dot(p.astype(BF16), vc_ref[...], preferred_element_type=F32)
    gates = g_ref[...]
    for h in range(hpg):
        o_ref[:, h * LANES:(h + 1) * LANES] = o[h * tq:(h + 1) * tq] * gates[:, h:h + 1]

    psum = p[0:tq]
    for h in range(1, hpg):
        psum = psum + p[h * tq:(h + 1) * tq]
    hi = psum.astype(BF16)
    lo = (psum - hi.astype(F32)).astype(BF16)
    ovl = ovl_ref[...]
    imp = (lax.dot_general(ovl, hi, NT_DIMS, preferred_element_type=F32)
           + lax.dot_general(ovl, lo, NT_DIMS, preferred_element_type=F32))
    jidx = lax.broadcasted_iota(jnp.int32, (n_slc, tq), 0)
    tq_pos = i * tq + lax.broadcasted_iota(jnp.int32, (n_slc, tq), 1)
    cur = jnp.right_shift(tq_pos, _SLC_SHIFT)
    forced = (jidx == 0) | (jidx == cur) | (jidx == cur - 1)
    causal = jidx * SLC_BLOCK <= tq_pos
    imp = jnp.where(forced, FORCED_SCORE, jnp.where(causal, imp, NEG_INF))
    sel_rows = []
    for j in range(n_slc):
        row = imp[j:j + 1, :]
        lower = jnp.where(jidx < j, 1.0, 0.0)
        beats = jnp.where(imp > row, 1.0, jnp.where(imp == row, lower, 0.0))
        rank = jnp.sum(beats, axis=0, keepdims=True)
        sel_rows.append(jnp.where(rank < top_n, 0.0, NEG_INF))
    sel_rows.append(jnp.zeros((LANES - n_slc, tq), F32))
    bias_t = jnp.concatenate(sel_rows, axis=0)
    sel_ref[...] = bias_t.T.astype(sel_ref.dtype)


def _nsa_cmp_attn(q, kcmp, vcmp, gates, ovl, *, groups, hpg, n_cmp, n_slc, top_n):
    b, s, d = q.shape
    ncp = kcmp.shape[2]
    tq = 128
    qo_spec = pl.BlockSpec((None, tq, hpg * LANES), lambda bi, gi, i: (bi, i, gi))
    c_spec = pl.BlockSpec((None, None, ncp, LANES), lambda bi, gi, i: (bi, gi, 0, 0))
    return pl.pallas_call(
        functools.partial(_cmp_attn_kernel, tq=tq, hpg=hpg, n_cmp=n_cmp, n_slc=n_slc, top_n=top_n),
        grid=(b, groups, s // tq),
        in_specs=[
            qo_spec, c_spec, c_spec,
            pl.BlockSpec((None, tq, LANES), lambda bi, gi, i: (bi, i, gi)),
            pl.BlockSpec((n_slc, ncp), lambda bi, gi, i: (0, 0)),
        ],
        out_specs=[qo_spec, pl.BlockSpec((None, None, tq, LANES), lambda bi, gi, i: (bi, gi, i, 0))],
        out_shape=[jax.ShapeDtypeStruct((b, s, d), F32), jax.ShapeDtypeStruct((b, groups, s, LANES), BF16)],
        compiler_params=_params("parallel", "parallel", "parallel"),
        name="nsa_cmp_attn",
    )(q, kcmp, vcmp, gates, ovl)


def _stack_aug(q, upper, hpg):
    return jnp.concatenate(
        [jnp.concatenate([q[:, h * LANES:(h + 1) * LANES], upper[h]], axis=1) for h in range(hpg)], axis=0)


def _score_bounds(q, k_gain, hpg):
    k_bound = BOUND_SLACK * math.sqrt(LANES) * jnp.max(jnp.abs(k_gain), axis=-1, keepdims=True)
    out = []
    for h in range(hpg):
        qh = q[:, h * LANES:(h + 1) * LANES].astype(F32)
        out.append(jnp.sqrt(jnp.sum(qh * qh, axis=-1, keepdims=True)) * k_bound)
    return out


def _gated_store(o_ref, oin_ref, gates, o, first_gate, tq, hpg):
    for h in range(hpg):
        cols = slice(h * LANES, (h + 1) * LANES)
        gate = gates[:, first_gate + h:first_gate + h + 1]
        o_ref[:, cols] = (oin_ref[:, cols] + o[h * tq:(h + 1) * tq] * gate).astype(o_ref.dtype)


def _sel_attn_kernel(q_ref, k_ref, v_ref, sel_ref, kn_ref, g_ref, oin_ref, o_ref, *, tq, tk, hpg):
    i = pl.program_id(2)
    q = q_ref[...]
    bias = sel_ref[...].astype(F32)
    rows = hpg * tq
    tpos = i * tq + lax.broadcasted_iota(jnp.int32, (tq, 1), 0)
    n_kv = ((i + 1) * tq + tk - 1) // tk
    gates = g_ref[...]

    def causal_bias(start):
        kpos = start + lax.broadcasted_iota(jnp.int32, (1, tk), 1)
        return jnp.concatenate([jnp.where(kpos <= tpos, 0.0, NEG_INF)] * hpg, axis=0)

    q8 = _stack_aug(q, [(bias - m).astype(BF16) for m in _score_bounds(q, kn_ref[...], hpg)], hpg)

    def fast_tile(kv, acc, diagonal):
        start = pl.multiple_of(kv * tk, tk)
        s = lax.dot_general(q8, k_ref[pl.ds(start, tk), :], NT_DIMS, preferred_element_type=F32)
        if diagonal:
            s = s + causal_bias(start)
        return acc + jnp.dot(jnp.exp(s).astype(BF16), v_ref[pl.ds(start, tk), :], preferred_element_type=F32)

    acc = lax.fori_loop(0, n_kv - 1, lambda kv, a: fast_tile(kv, a, False), jnp.zeros((rows, 2 * LANES), F32))
    acc = fast_tile(n_kv - 1, acc, True)
    denom = acc[:, LANES:]
    _gated_store(o_ref, oin_ref, gates, acc[:, :LANES] / denom, hpg, tq, hpg)

    @pl.when(jnp.logical_not(jnp.min(denom) >= SOFTMAX_FLOOR))
    def _():
        q8x = _stack_aug(q, [sel_ref[...]] * hpg, hpg)

        def exact_tile(kv, carry, diagonal):
            m, acc = carry
            start = pl.multiple_of(kv * tk, tk)
            s = lax.dot_general(q8x, k_ref[pl.ds(start, tk), :], NT_DIMS, preferred_element_type=F32)
            if diagonal:
                s = s + causal_bias(start)
            m_new = jnp.maximum(m, jnp.max(s, axis=-1, keepdims=True))
            p = jnp.exp(s - m_new).astype(BF16)
            acc = jnp.exp(m - m_new) * acc + jnp.dot(p, v_ref[pl.ds(start, tk), :], preferred_element_type=F32)
            return m_new, acc

        init = (jnp.full((rows, 1), NEG_INF, F32), jnp.zeros((rows, 2 * LANES), F32))
        carry = lax.fori_loop(0, n_kv - 1, lambda kv, c: exact_tile(kv, c, False), init)
        _, acc_x = exact_tile(n_kv - 1, carry, True)
        _gated_store(o_ref, oin_ref, gates, acc_x[:, :LANES] / acc_x[:, LANES:], hpg, tq, hpg)


def _nsa_sel_attn(q, ks, vs, selb, k_gain, gates, o_in, *, groups, hpg):
    b, s, d = q.shape
    tq = 128
    tk = _pick(s, (512, 256, 128))
    qo_spec = pl.BlockSpec((None, tq, hpg * LANES), lambda bi, gi, i: (bi, i, gi))
    kv_spec = pl.BlockSpec((None, None, s, 2 * LANES), lambda bi, gi, i: (bi, gi, 0, 0))
    return pl.pallas_call(
        functools.partial(_sel_attn_kernel, tq=tq, tk=tk, hpg=hpg),
        grid=(b, groups, s // tq),
        in_specs=[
            qo_spec, kv_spec, kv_spec,
            pl.BlockSpec((None, None, tq, LANES), lambda bi, gi, i: (bi, gi, i, 0)),
            pl.BlockSpec((1, LANES), lambda bi, gi, i: (0, 0)),
            pl.BlockSpec((None, tq, LANES), lambda bi, gi, i: (bi, i, gi)),
            qo_spec,
        ],
        out_specs=qo_spec,
        out_shape=jax.ShapeDtypeStruct((b, s, d), F32),
        compiler_params=_params("parallel", "parallel", "parallel"),
        name="nsa_sel_attn",
    )(q, ks, vs, selb, k_gain.reshape(1, LANES), gates, o_in)


def _win_attn_kernel(q_ref, k_ref, v_ref, kn_ref, g_ref, oin_ref, o_ref, *, tq, span, hpg):
    i = pl.program_id(2)
    q = q_ref[...]
    start = pl.multiple_of(jnp.maximum(i * tq - WINDOW, 0), tq)
    k = k_ref[pl.ds(start, span), :]
    v = v_ref[pl.ds(start, span), :]
    tpos = i * tq + lax.broadcasted_iota(jnp.int32, (tq, 1), 0)
    kpos = start + lax.broadcasted_iota(jnp.int32, (1, span), 1)
    diff = tpos - kpos
    bias = jnp.concatenate([jnp.where((diff >= 0) & (diff < WINDOW), 0.0, NEG_INF)] * hpg, axis=0)
    gates = g_ref[...]

    q8 = _stack_aug(q, [jnp.broadcast_to(-m, (tq, LANES)).astype(BF16) for m in _score_bounds(q, kn_ref[...], hpg)],
                    hpg)
    s = lax.dot_general(q8, k, NT_DIMS, preferred_element_type=F32) + bias
    acc = jnp.dot(jnp.exp(s).astype(BF16), v, preferred_element_type=F32)
    denom = acc[:, LANES:]
    _gated_store(o_ref, oin_ref, gates, acc[:, :LANES] / denom, 2 * hpg, tq, hpg)

    @pl.when(jnp.logical_not(jnp.min(denom) >= SOFTMAX_FLOOR))
    def _():
        q8x = _stack_aug(q, [jnp.zeros((tq, LANES), BF16)] * hpg, hpg)
        sx = lax.dot_general(q8x, k, NT_DIMS, preferred_element_type=F32) + bias
        p = jnp.exp(sx - jnp.max(sx, axis=-1, keepdims=True))
        acc_x = jnp.dot(p.astype(BF16), v, preferred_element_type=F32)
        _gated_store(o_ref, oin_ref, gates, acc_x[:, :LANES] / acc_x[:, LANES:], 2 * hpg, tq, hpg)


def _nsa_win_attn(q, kw, vw, k_gain, gates, o_in, *, groups, hpg):
    b, s, d = q.shape
    tq = 128
    span = WINDOW + tq
    assert s >= span and WINDOW % tq == 0
    qo_spec = pl.BlockSpec((None, tq, hpg * LANES), lambda bi, gi, i: (bi, i, gi))
    kv_spec = pl.BlockSpec((None, None, s, 2 * LANES), lambda bi, gi, i: (bi, gi, 0, 0))
    return pl.pallas_call(
        functools.partial(_win_attn_kernel, tq=tq, span=span, hpg=hpg),
        grid=(b, groups, s // tq),
        in_specs=[qo_spec, kv_spec, kv_spec, pl.BlockSpec((1, LANES), lambda bi, gi, i: (0, 0)),
                  pl.BlockSpec((None, tq, LANES), lambda bi, gi, i: (bi, i, gi)), qo_spec],
        out_specs=qo_spec,
        out_shape=jax.ShapeDtypeStruct((b, s, d), BF16),
        compiler_params=_params("parallel", "parallel", "parallel"),
        name="nsa_win_attn",
    )(q, kw, vw, k_gain.reshape(1, LANES), gates, o_in)


def _rope_tables(seq, dim):
    inv = ROPE_THETA ** (-jnp.arange(0, dim, 2, dtype=F32) / dim)
    ang = jnp.arange(seq, dtype=F32)[:, None] * inv[None, :]
    ang = jnp.concatenate([ang, ang], axis=-1)
    sign = jnp.where(jnp.arange(dim) < dim // 2, -1.0, 1.0).astype(F32)
    return jnp.cos(ang), jnp.sin(ang) * sign


def _nsa_mixer(x2, h, b, s, w_in, q_norm, k_norm, cmp_pos, cmp_w1, cmp_w2, w_o):
    heads, groups = NSA_HEADS, NSA_GROUPS
    hpg = heads // groups
    d = x2.shape[1]
    assert d == heads * LANES and CMP_BLOCK == 2 * CMP_STRIDE and 3 * hpg <= LANES
    qd, kvd = heads * LANES, groups * LANES
    n_main = qd + 6 * kvd
    nseg = s // CMP_STRIDE
    n_cmp = (s - CMP_BLOCK) // CMP_STRIDE + 1
    n_slc = s // SLC_BLOCK
    top_n = min(SLC_TOPN, n_slc)
    assert n_cmp == nseg - 1 and n_slc % 8 == 0 and n_slc <= LANES and top_n >= 3

    proj = _mm(h, w_in, n=n_main, out_dtype=F32, name="nsa_in_proj")
    wg = w_in[:, n_main:].reshape(d, 3, groups, hpg).transpose(0, 2, 1, 3).reshape(d, groups, 3 * hpg)
    wg = jnp.pad(wg, ((0, 0), (0, 0), (0, LANES - 3 * hpg))).reshape(d, groups * LANES).astype(BF16)
    gates = _mm(h, wg, out_dtype=F32, sigmoid=True, name="nsa_gate_proj").reshape(b, s, groups * LANES)

    cos, sin_signed = _rope_tables(s, LANES)
    q, kc, vc, ks, vs, kw, vw = _nsa_prep(proj.reshape(b, s, n_main), cos, sin_signed, q_norm, k_norm,
                                          heads=heads, groups=groups)

    seg_w = CMP_STRIDE * LANES
    pe = cmp_pos.reshape(2, 2, 1, seg_w)
    w1 = cmp_w1.reshape(2, 2, seg_w, LANES).astype(BF16)
    kcmp, vcmp = _nsa_compress(kc.reshape(b, groups, nseg, seg_w), vc.reshape(b, groups, nseg, seg_w),
                               pe, w1, cmp_w2.astype(BF16))

    cmp_start = np.arange(nseg) * CMP_STRIDE
    slc_start = np.arange(n_slc) * SLC_BLOCK
    ovl = ((cmp_start[None, :] < slc_start[:, None] + SLC_BLOCK) & (cmp_start[None, :] + CMP_BLOCK > slc_start[:, None])
           & (np.arange(nseg)[None, :] < n_cmp))
    ovl = jnp.asarray(ovl.astype(np.float32), dtype=BF16)

    o1, selm = _nsa_cmp_attn(q, kcmp, vcmp, gates, ovl, groups=groups, hpg=hpg, n_cmp=n_cmp, n_slc=n_slc, top_n=top_n)
    o2 = _nsa_sel_attn(q, ks, vs, selm, k_norm[1], gates, o1, groups=groups, hpg=hpg)
    o3 = _nsa_win_attn(q, kw, vw, k_norm[2], gates, o2, groups=groups, hpg=hpg)
    return _mm_res(o3.reshape(b * s, d), w_o, x2, 1.0,
                   bm_prefs=(1024, 512, 256, 128), bn_prefs=(512, 256, 128), name="nsa_out_proj")


def _chunk_cumsum(x, row):
    n = x.shape[0]
    step = 1
    while step < n:
        x = x + jnp.where(row >= step, pltpu.roll(x, step, 0), 0.0)
        step *= 2
    return x


def _hgrn_kernel(q_ref, f_ref, i_ref, g_ref, lbl_ref, on_ref, o_ref, *, seq, chunk, layer, hps, unroll):
    lbl = lbl_ref[...]
    e = jnp.exp(lbl - jnp.max(lbl, axis=0, keepdims=True))
    p = e / jnp.sum(e, axis=0, keepdims=True)
    csum = p[0:1]
    for d in range(1, layer + 1):
        csum = csum + p[d:d + 1]
    lb = csum - p[0:1]
    o_gain = on_ref[...]
    causal = (lax.broadcasted_iota(jnp.int32, (chunk, chunk), 0) >= lax.broadcasted_iota(jnp.int32, (chunk, chunk), 1))
    row = lax.broadcasted_iota(jnp.int32, (chunk, hps * LANES), 0)
    heads = [slice(k * LANES, (k + 1) * LANES) for k in range(hps)]

    def body(it, states):
        states = list(states)
        chunks = []
        for u in range(unroll):
            rows = pl.ds(pl.multiple_of((it * unroll + u) * chunk, chunk), chunk)
            f = lb + (1.0 - lb) * jax.nn.sigmoid(f_ref[rows, :])
            kc = 1.0 - f
            g_cum = _chunk_cumsum(jnp.log(f), row)
            g_last = g_cum[chunk - 1:chunk, :]
            q_dec = (q_ref[rows, :] * jnp.exp(g_cum)).astype(BF16)
            k_inv = (kc * jnp.exp(-g_cum)).astype(BF16)
            k_tail = (kc * jnp.exp(g_last - g_cum)).astype(BF16)
            chunks.append((rows, q_dec, k_inv, k_tail, i_ref[rows, :].astype(BF16), jnp.exp(g_last)))
        a_all = [[lax.dot_general(q_dec[:, h], k_inv[:, h], NT_DIMS, preferred_element_type=F32) for h in heads]
                 for (_, q_dec, k_inv, _, _, _) in chunks]
        ds_all = [[lax.dot_general(v[:, h], k_tail[:, h], TN_DIMS, preferred_element_type=F32) for h in heads]
                  for (_, _, _, k_tail, v, _) in chunks]
        o_all = [[jnp.dot(jnp.where(causal, a, 0.0).astype(BF16), v[:, h], preferred_element_type=F32)
                  for a, h in zip(a_row, heads)]
                 for a_row, (_, _, _, _, v, _) in zip(a_all, chunks)]
        for u, (rows, q_dec, _, _, _, decay) in enumerate(chunks):
            for k, h in enumerate(heads):
                o = o_all[u][k] + lax.dot_general(q_dec[:, h], states[k].astype(BF16), NT_DIMS,
                                                  preferred_element_type=F32)
                states[k] = states[k] * decay[:, h] + ds_all[u][k]
                y = o * lax.rsqrt(jnp.mean(o * o, axis=-1, keepdims=True) + RMS_EPS) * o_gain
                gz = g_ref[rows, h]
                o_ref[rows, h] = (y * (gz * jax.nn.sigmoid(gz))).astype(o_ref.dtype)
        return tuple(states)

    init = tuple(jnp.zeros((LANES, LANES), F32) for _ in range(hps))
    lax.fori_loop(0, seq // (chunk * unroll), body, init)


def _hgrn_core(proj, lb_logits, o_norm, *, heads, layer, hps=4, unroll=2):
    b, s, _ = proj.shape
    depth = lb_logits.shape[0]
    hps = math.gcd(hps, heads)
    width = hps * LANES
    groups = heads // hps
    assert s % (HGRN_CHUNK * unroll) == 0

    def col_spec(part):
        return pl.BlockSpec((None, s, width), lambda bi, hi: (bi, 0, part * groups + hi))

    return pl.pallas_call(
        functools.partial(_hgrn_kernel, seq=s, chunk=HGRN_CHUNK, layer=layer, hps=hps, unroll=unroll),
        grid=(b, groups),
        in_specs=[col_spec(0), col_spec(1), col_spec(2), col_spec(3),
                  pl.BlockSpec((depth, width), lambda bi, hi: (0, hi)),
                  pl.BlockSpec((1, LANES), lambda bi, hi: (0, 0))],
        out_specs=pl.BlockSpec((None, s, width), lambda bi, hi: (bi, 0, hi)),
        out_shape=jax.ShapeDtypeStruct((b, s, heads * LANES), BF16),
        compiler_params=_params("parallel", "parallel"),
        name="hgrn_core",
    )(proj, proj, proj, proj, lb_logits, o_norm.reshape(1, LANES))


def _hgrn_mixer(x2, h, b, s, w_in, lb_logits, o_norm, w_o, layer):
    heads = HGRN_HEADS
    d = x2.shape[1]
    assert d == heads * LANES and w_in.shape[1] == 4 * d and s % HGRN_CHUNK == 0
    proj = _mm(h, w_in, out_dtype=F32, name="hgrn_in_proj")
    o = _hgrn_core(proj.reshape(b, s, 4 * d), lb_logits, o_norm, heads=heads, layer=layer)
    return _mm_res(o.reshape(b * s, d), w_o, x2, 1.0,
                   bm_prefs=(1024, 512, 256, 128), bn_prefs=(512, 256, 128), name="hgrn_out_proj")


def kernel(x, ffn_norm, ffn_w_gate, ffn_w_up, ffn_w_down, mix_norm, nsa_w_in, nsa_q_norm, nsa_k_norm, nsa_cmp_pos,
           nsa_cmp_w1, nsa_cmp_w2, nsa_w_o, hgrn_w_in, hgrn_lb_logits, hgrn_o_norm, hgrn_w_o):
    b, s, d = x.shape
    depth = ffn_norm.shape[0]
    x2 = x.reshape(b * s, d)
    w_down = ffn_w_down.astype(BF16)
    for layer in range(depth):
        slot = layer // N_MIXERS
        x2 = _ffn(x2, ffn_norm[layer, 0], ffn_w_gate, ffn_w_up, w_down, (layer, 0))
        h = _rmsnorm(x2, mix_norm[layer])
        if layer % N_MIXERS == 0:
            x2 = _nsa_mixer(x2, h, b, s, nsa_w_in[slot], nsa_q_norm[slot], nsa_k_norm[slot], nsa_cmp_pos[slot],
                            nsa_cmp_w1[slot], nsa_cmp_w2[slot], nsa_w_o[slot])
        else:
            x2 = _hgrn_mixer(x2, h, b, s, hgrn_w_in[slot], hgrn_lb_logits, hgrn_o_norm[slot], hgrn_w_o[slot], layer)
        x2 = _ffn(x2, ffn_norm[layer, 1], ffn_w_gate, ffn_w_up, w_down, (layer, 1))
    return x2.reshape(b, s, d)
```

```python
import functools
import math

import jax
import jax.numpy as jnp
import numpy as np
from jax import lax
from jax.experimental import pallas as pl
from jax.experimental.pallas import tpu as pltpu

F32 = jnp.float32
BF16 = jnp.bfloat16

RMS_EPS = 1e-6
NEG_INF = -1e30
FORCED_SCORE = 1e30
FFN_RES_WEIGHT = 0.5
N_MIXERS = 2

NSA_HEADS = 32
NSA_GROUPS = 4
CMP_BLOCK = 32
CMP_STRIDE = 16
SLC_BLOCK = 64
SLC_TOPN = 16
_SLC_SHIFT = SLC_BLOCK.bit_length() - 1
assert 1 << _SLC_SHIFT == SLC_BLOCK
WINDOW = 512
ROPE_THETA = 10000.0

HGRN_HEADS = 32
HGRN_CHUNK = 64

LANES = 128
SOFTMAX_FLOOR = 1e-30
BOUND_SLACK = 1.02
VMEM_LIMIT_BYTES = 56 * 1024 * 1024

NT_DIMS = (((1,), (1,)), ((), ()))
TN_DIMS = (((0,), (0,)), ((), ()))


def _params(*sem):
    return pltpu.CompilerParams(dimension_semantics=sem, vmem_limit_bytes=VMEM_LIMIT_BYTES)


def _pick(n, prefs):
    for p in prefs:
        if n % p == 0:
            return p
    raise ValueError(f"no tile in {prefs} divides {n}")


def _fold_lanes(x):
    out = x[:, :LANES]
    for t in range(1, x.shape[1] // LANES):
        out = out + x[:, t * LANES:(t + 1) * LANES]
    return out


def _inv_rms(ss, width):
    return lax.rsqrt(jnp.sum(ss, axis=-1, keepdims=True) * (1.0 / width) + RMS_EPS)


def _prenorm_kernel(x_ref, g_ref, xg_ref, ss_ref):
    x = x_ref[...]
    xg_ref[...] = (x * g_ref[...]).astype(xg_ref.dtype)
    ss_ref[...] = _fold_lanes(x * x)


def _prenorm(x, g):
    m, d = x.shape
    tr = _pick(m, (256, 128, 8))
    return pl.pallas_call(
        _prenorm_kernel,
        grid=(m // tr,),
        in_specs=[pl.BlockSpec((tr, d), lambda i: (i, 0)), pl.BlockSpec((1, d), lambda i: (0, 0))],
        out_specs=[pl.BlockSpec((tr, d), lambda i: (i, 0)), pl.BlockSpec((tr, LANES), lambda i: (i, 0))],
        out_shape=[jax.ShapeDtypeStruct((m, d), BF16), jax.ShapeDtypeStruct((m, LANES), F32)],
        compiler_params=_params("parallel"),
        name="prenorm",
    )(x, g.reshape(1, d))


def _weight_spec(w, lead, bn):
    k = w.shape[-2]
    return pl.BlockSpec((None,) * len(lead) + (k, bn), lambda i, j: lead + (0, j))


def _lhs_spec(bm, k, single_buffer):
    if single_buffer:
        return pl.BlockSpec((bm, k), lambda i, j: (i, 0), pipeline_mode=pl.Buffered(1))
    return pl.BlockSpec((bm, k), lambda i, j: (i, 0))


def _stat_spec(bm):
    return pl.BlockSpec((bm, LANES), lambda i, j: (i, 0))


def _gateup_kernel(xg_ref, ss_ref, wg_ref, wu_ref, o_ref, r_ref, *, width):
    @pl.when(pl.program_id(1) == 0)
    def _():
        r_ref[...] = jnp.broadcast_to(_inv_rms(ss_ref[...], width), r_ref.shape)

    xg = xg_ref[...]
    r = jnp.concatenate([r_ref[...]] * (o_ref.shape[1] // LANES), axis=1)
    a = jnp.dot(xg, wg_ref[...].astype(BF16), preferred_element_type=F32) * r
    b = jnp.dot(xg, wu_ref[...].astype(BF16), preferred_element_type=F32) * r
    o_ref[...] = (a * jax.nn.sigmoid(a) * b).astype(o_ref.dtype)


def _gateup(xg, ss, wg, wu, lead):
    m, k = xg.shape
    n = wg.shape[-1]
    bm = _pick(m, (2048, 1024, 512, 256, 128))
    bn = _pick(n, (256, 128))
    return pl.pallas_call(
        functools.partial(_gateup_kernel, width=k),
        grid=(m // bm, n // bn),
        in_specs=[_lhs_spec(bm, k, True), _stat_spec(bm), _weight_spec(wg, lead, bn), _weight_spec(wu, lead, bn)],
        out_specs=pl.BlockSpec((bm, bn), lambda i, j: (i, j)),
        out_shape=jax.ShapeDtypeStruct((m, n), BF16),
        scratch_shapes=[pltpu.VMEM((bm, LANES), F32)],
        compiler_params=_params("parallel", "arbitrary"),
        name="ffn_gateup",
    )(xg, ss, wg, wu)


def _mm_res_kernel(a_ref, w_ref, r_ref, *rest, scale, emit_norm):
    acc = jnp.dot(a_ref[...], w_ref[...].astype(BF16), preferred_element_type=F32)
    x = r_ref[...] + scale * acc
    if not emit_norm:
        (o_ref,) = rest
        o_ref[...] = x
        return
    gn_ref, o_ref, xg_ref, ss_ref = rest
    o_ref[...] = x
    xg_ref[...] = (x * gn_ref[...]).astype(xg_ref.dtype)

    @pl.when(pl.program_id(1) == 0)
    def _():
        ss_ref[...] = jnp.zeros_like(ss_ref)

    ss_ref[...] += _fold_lanes(x * x)


def _mm_res(a, w, res, scale, *, lead=(), next_gain=None, bm_prefs, bn_prefs, single_buffer_lhs=False, name):
    m, k = a.shape
    n = w.shape[-1]
    bm = _pick(m, bm_prefs)
    bn = _pick(n, bn_prefs)
    tile = pl.BlockSpec((bm, bn), lambda i, j: (i, j))
    emit_norm = next_gain is not None
    in_specs = [_lhs_spec(bm, k, single_buffer_lhs), _weight_spec(w, lead, bn), tile]
    out_specs, out_shape, args = tile, jax.ShapeDtypeStruct((m, n), F32), (a, w, res)
    if emit_norm:
        in_specs.append(pl.BlockSpec((1, bn), lambda i, j: (0, j)))
        out_specs = [tile, tile, _stat_spec(bm)]
        out_shape = [out_shape, jax.ShapeDtypeStruct((m, n), BF16), jax.ShapeDtypeStruct((m, LANES), F32)]
        args = args + (next_gain.reshape(1, n),)
    return pl.pallas_call(
        functools.partial(_mm_res_kernel, scale=scale, emit_norm=emit_norm),
        grid=(m // bm, n // bn),
        in_specs=in_specs,
        out_specs=out_specs,
        out_shape=out_shape,
        compiler_params=_params("parallel", "arbitrary"),
        name=name,
    )(*args)


def _mm_kernel(xg_ref, ss_ref, w_ref, o_ref, *, width, sigmoid):
    acc = jnp.dot(xg_ref[...], w_ref[...].astype(BF16), preferred_element_type=F32) * _inv_rms(ss_ref[...], width)
    if sigmoid:
        acc = jax.nn.sigmoid(acc)
    o_ref[...] = acc.astype(o_ref.dtype)


def _mm(xg, ss, w, *, lead=(), n=None, out_dtype, sigmoid=False, name):
    m, k = xg.shape
    n = w.shape[-1] if n is None else n
    bm = _pick(m, (2048, 1024, 512, 256, 128))
    bn = _pick(n, (512, 256, 128))
    return pl.pallas_call(
        functools.partial(_mm_kernel, width=k, sigmoid=sigmoid),
        grid=(m // bm, n // bn),
        in_specs=[_lhs_spec(bm, k, True), _stat_spec(bm), _weight_spec(w, lead, bn)],
        out_specs=pl.BlockSpec((bm, bn), lambda i, j: (i, j)),
        out_shape=jax.ShapeDtypeStruct((m, n), out_dtype),
        compiler_params=_params("parallel", "arbitrary"),
        name=name,
    )(xg, ss, w)


def _ffn(x, xg, ss, w_gate, w_up, w_down_bf16, lead, next_gain):
    act = _gateup(xg, ss, w_gate, w_up, lead)
    return _mm_res(act, w_down_bf16, x, FFN_RES_WEIGHT, lead=lead, next_gain=next_gain,
                   bm_prefs=(512, 256, 128), bn_prefs=(512, 256, 128), name="ffn_down")


def _head_norm_rope(x, gain, cos, sin_signed):
    y = x * lax.rsqrt(jnp.mean(x * x, axis=-1, keepdims=True) + RMS_EPS) * gain
    return y * cos + pltpu.roll(y, LANES // 2, 1) * sin_signed


def _nsa_prep_kernel(p_ref, cos_ref, sin_ref, qn_ref, kn_ref, q_ref, kc_ref, vc_ref, ks_ref, vs_ref, kw_ref, vw_ref,
                     *, heads, groups):
    cos = cos_ref[...]
    sin = sin_ref[...]
    scale = LANES ** -0.5
    qn = qn_ref[...]
    for h in range(heads):
        x = p_ref[:, h * LANES:(h + 1) * LANES]
        q_ref[:, h * LANES:(h + 1) * LANES] = (_head_norm_rope(x, qn, cos, sin) * scale).astype(q_ref.dtype)
    base = heads * LANES
    kv_w = groups * LANES
    ts = p_ref.shape[0]
    pos = pl.program_id(1) * ts + lax.broadcasted_iota(jnp.int32, (ts, LANES), 0)
    lane = lax.broadcasted_iota(jnp.int32, (ts, LANES), 1)
    blk_onehot = jnp.where(lane == jnp.right_shift(pos, _SLC_SHIFT), 1.0, 0.0).astype(ks_ref.dtype)
    lane0_onehot = jnp.where(lane == 0, 1.0, 0.0).astype(kw_ref.dtype)
    for br, (k_out, v_out) in enumerate(((kc_ref, vc_ref), (ks_ref, vs_ref), (kw_ref, vw_ref))):
        kn = kn_ref[br:br + 1, :]
        for g in range(groups):
            ck = base + (2 * br) * kv_w + g * LANES
            cv = base + (2 * br + 1) * kv_w + g * LANES
            k = _head_norm_rope(p_ref[:, ck:ck + LANES], kn, cos, sin).astype(k_out.dtype)
            v = p_ref[:, cv:cv + LANES].astype(v_out.dtype)
            if br == 0:
                k_out[g] = k
                v_out[g] = v
            else:
                k_out[g, :, :LANES] = k
                k_out[g, :, LANES:] = blk_onehot if br == 1 else lane0_onehot
                v_out[g, :, :LANES] = v
                v_out[g, :, LANES:] = jnp.ones_like(v)


def _nsa_prep(proj, cos, sin_signed, q_norm, k_norm, *, heads, groups):
    b, s, n = proj.shape
    ts = _pick(s, (256, 128))
    kv_spec = pl.BlockSpec((None, groups, ts, LANES), lambda bi, i: (bi, 0, i, 0))
    aug_spec = pl.BlockSpec((None, groups, ts, 2 * LANES), lambda bi, i: (bi, 0, i, 0))
    tab_spec = pl.BlockSpec((ts, LANES), lambda bi, i: (i, 0))

    def kv_shape(dt, width=LANES):
        return jax.ShapeDtypeStruct((b, groups, s, width), dt)

    return pl.pallas_call(
        functools.partial(_nsa_prep_kernel, heads=heads, groups=groups),
        grid=(b, s // ts),
        in_specs=[
            pl.BlockSpec((None, ts, n), lambda bi, i: (bi, i, 0)),
            tab_spec,
            tab_spec,
            pl.BlockSpec((1, LANES), lambda bi, i: (0, 0)),
            pl.BlockSpec((3, LANES), lambda bi, i: (0, 0)),
        ],
        out_specs=[pl.BlockSpec((None, ts, heads * LANES), lambda bi, i: (bi, i, 0)),
                   kv_spec, kv_spec, aug_spec, aug_spec, aug_spec, aug_spec],
        out_shape=[jax.ShapeDtypeStruct((b, s, heads * LANES), BF16),
                   kv_shape(F32), kv_shape(F32), kv_shape(BF16, 2 * LANES), kv_shape(BF16, 2 * LANES),
                   kv_shape(BF16, 2 * LANES), kv_shape(BF16, 2 * LANES)],
        compiler_params=_params("parallel", "parallel"),
        name="nsa_prep",
    )(proj, cos, sin_signed, q_norm.reshape(1, LANES), k_norm)


def _gelu_tanh(x):
    return x * (0.5 * (1.0 + jnp.tanh(math.sqrt(2.0 / math.pi) * (x + 0.044715 * (x * x * x)))))


def _compress_kernel(k_ref, v_ref, pe_ref, w1_ref, w2_ref, ko_ref, vo_ref):
    for which, (x_ref, o_ref) in enumerate(((k_ref, ko_ref), (v_ref, vo_ref))):
        x = x_ref[...]
        nseg = x.shape[0]
        top = jnp.dot((x + pe_ref[which, 0]).astype(BF16), w1_ref[which, 0], preferred_element_type=F32)
        bot = jnp.dot((x + pe_ref[which, 1]).astype(BF16), w1_ref[which, 1], preferred_element_type=F32)
        pre = top + pltpu.roll(bot, nseg - 1, 0)
        hid = _gelu_tanh(pre).astype(BF16)
        o_ref[...] = jnp.dot(hid, w2_ref[which], preferred_element_type=F32).astype(o_ref.dtype)


def _nsa_compress(kcf, vcf, pe, w1, w2):
    b, g, nseg, width = kcf.shape
    x_spec = pl.BlockSpec((None, None, nseg, width), lambda bi, gi: (bi, gi, 0, 0))
    o_spec = pl.BlockSpec((None, None, nseg, LANES), lambda bi, gi: (bi, gi, 0, 0))
    o_shape = jax.ShapeDtypeStruct((b, g, nseg, LANES), BF16)
    return pl.pallas_call(
        _compress_kernel,
        grid=(b, g),
        in_specs=[
            x_spec, x_spec,
            pl.BlockSpec((2, 2, 1, width), lambda bi, gi: (0, 0, 0, 0)),
            pl.BlockSpec((2, 2, width, LANES), lambda bi, gi: (0, 0, 0, 0)),
            pl.BlockSpec((2, LANES, LANES), lambda bi, gi: (0, 0, 0)),
        ],
        out_specs=[o_spec, o_spec],
        out_shape=[o_shape, o_shape],
        compiler_params=_params("parallel", "parallel"),
        name="nsa_compress",
    )(kcf, vcf, pe, w1, w2)


def _stack_heads(q, hpg):
    return jnp.concatenate([q[:, h * LANES:(h + 1) * LANES] for h in range(hpg)], axis=0)


def _cmp_attn_kernel(q_ref, kc_ref, vc_ref, g_ref, ovl_ref, o_ref, sel_ref, *, tq, hpg, n_cmp, n_slc, top_n):
    i = pl.program_id(2)
    q8 = _stack_heads(q_ref[...], hpg)
    rows = hpg * tq
    ncp = kc_ref.shape[0]
    s = lax.dot_general(q8, kc_ref[...], NT_DIMS, preferred_element_type=F32)
    tpos = i * tq + lax.broadcasted_iota(jnp.int32, (tq, 1), 0)
    tpos8 = jnp.concatenate([tpos] * hpg, axis=0)
    ncol = lax.broadcasted_iota(jnp.int32, (1, ncp), 1)
    valid = jnp.where((ncol * CMP_STRIDE + (CMP_BLOCK - 1) <= tpos8) & (ncol < n_cmp), 1.0, 0.0)
    sm = jnp.where(valid > 0.5, s, NEG_INF)
    m = jnp.max(sm, axis=-1, keepdims=True)
    e = jnp.exp(sm - m) * valid
    l = jnp.sum(e, axis=-1, keepdims=True)
    p = e / jnp.where(l > 0.0, l, 1.0)
    o = jnp.dot(p.astype(BF16), vc_ref[...], preferred_element_type=F32)
    gates = g_ref[...]
    for h in range(hpg):
        o_ref[:, h * LANES:(h + 1) * LANES] = o[h * tq:(h + 1) * tq] * gates[:, h:h + 1]

    psum = p[0:tq]
    for h in range(1, hpg):
        psum = psum + p[h * tq:(h + 1) * tq]
    hi = psum.astype(BF16)
    lo = (psum - hi.astype(F32)).astype(BF16)
    ovl = ovl_ref[...]
    imp = (lax.dot_general(ovl, hi, NT_DIMS, preferred_element_type=F32)
           + lax.dot_general(ovl, lo, NT_DIMS, preferred_element_type=F32))
    jidx = lax.broadcasted_iota(jnp.int32, (n_slc, tq), 0)
    tq_pos = i * tq + lax.broadcasted_iota(jnp.int32, (n_slc, tq), 1)
    cur = jnp.right_shift(tq_pos, _SLC_SHIFT)
    forced = (jidx == 0) | (jidx == cur) | (jidx == cur - 1)
    causal = jidx * SLC_BLOCK <= tq_pos
    imp = jnp.where(forced, FORCED_SCORE, jnp.where(causal, imp, NEG_INF))
    sel_rows = []
    for j in range(n_slc):
        row = imp[j:j + 1, :]
        lower = jnp.where(jidx < j, 1.0, 0.0)
        beats = jnp.where(imp > row, 1.0, jnp.where(imp == row, lower, 0.0))
        rank = jnp.sum(beats, axis=0, keepdims=True)
        sel_rows.append(jnp.where(rank < top_n, 0.0, NEG_INF))
    sel_rows.append(jnp.zeros((LANES - n_slc, tq), F32))
    bias_t = jnp.concatenate(sel_rows, axis=0)
    sel_ref[...] = bias_t.T.astype(sel_ref.dtype)


def _nsa_cmp_attn(q, kcmp, vcmp, gates, ovl, *, groups, hpg, n_cmp, n_slc, top_n):
    b, s, d = q.shape
    ncp = kcmp.shape[2]
    tq = 128
    qo_spec = pl.BlockSpec((None, tq, hpg * LANES), lambda bi, gi, i: (bi, i, gi))
    c_spec = pl.BlockSpec((None, None, ncp, LANES), lambda bi, gi, i: (bi, gi, 0, 0))
    return pl.pallas_call(
        functools.partial(_cmp_attn_kernel, tq=tq, hpg=hpg, n_cmp=n_cmp, n_slc=n_slc, top_n=top_n),
        grid=(b, groups, s // tq),
        in_specs=[
            qo_spec, c_spec, c_spec,
            pl.BlockSpec((None, tq, LANES), lambda bi, gi, i: (bi, i, gi)),
            pl.BlockSpec((n_slc, ncp), lambda bi, gi, i: (0, 0)),
        ],
        out_specs=[qo_spec, pl.BlockSpec((None, None, tq, LANES), lambda bi, gi, i: (bi, gi, i, 0))],
        out_shape=[jax.ShapeDtypeStruct((b, s, d), F32), jax.ShapeDtypeStruct((b, groups, s, LANES), BF16)],
        compiler_params=_params("parallel", "parallel", "parallel"),
        name="nsa_cmp_attn",
    )(q, kcmp, vcmp, gates, ovl)


def _stack_aug(q, upper, hpg):
    return jnp.concatenate(
        [jnp.concatenate([q[:, h * LANES:(h + 1) * LANES], upper[h]], axis=1) for h in range(hpg)], axis=0)


def _score_bounds(q, k_gain, hpg):
    k_bound = BOUND_SLACK * math.sqrt(LANES) * jnp.max(jnp.abs(k_gain), axis=-1, keepdims=True)
    out = []
    for h in range(hpg):
        qh = q[:, h * LANES:(h + 1) * LANES].astype(F32)
        out.append(jnp.sqrt(jnp.sum(qh * qh, axis=-1, keepdims=True)) * k_bound)
    return out


def _gated_store(o_ref, oin_ref, gates, o, first_gate, tq, hpg):
    for h in range(hpg):
        cols = slice(h * LANES, (h + 1) * LANES)
        gate = gates[:, first_gate + h:first_gate + h + 1]
        o_ref[:, cols] = (oin_ref[:, cols] + o[h * tq:(h + 1) * tq] * gate).astype(o_ref.dtype)


def _sel_attn_kernel(q_ref, k_ref, v_ref, sel_ref, kn_ref, g_ref, oin_ref, o_ref, *, tq, tk, hpg):
    i = pl.program_id(2)
    q = q_ref[...]
    bias = sel_ref[...].astype(F32)
    rows = hpg * tq
    tpos = i * tq + lax.broadcasted_iota(jnp.int32, (tq, 1), 0)
    n_kv = ((i + 1) * tq + tk - 1) // tk
    gates = g_ref[...]

    def causal_bias(start):
        kpos = start + lax.broadcasted_iota(jnp.int32, (1, tk), 1)
        return jnp.concatenate([jnp.where(kpos <= tpos, 0.0, NEG_INF)] * hpg, axis=0)

    q8 = _stack_aug(q, [(bias - m).astype(BF16) for m in _score_bounds(q, kn_ref[...], hpg)], hpg)

    def fast_tile(kv, acc, diagonal):
        start = pl.multiple_of(kv * tk, tk)
        s = lax.dot_general(q8, k_ref[pl.ds(start, tk), :], NT_DIMS, preferred_element_type=F32)
        if diagonal:
            s = s + causal_bias(start)
        return acc + jnp.dot(jnp.exp(s).astype(BF16), v_ref[pl.ds(start, tk), :], preferred_element_type=F32)

    acc = lax.fori_loop(0, n_kv - 1, lambda kv, a: fast_tile(kv, a, False), jnp.zeros((rows, 2 * LANES), F32))
    acc = fast_tile(n_kv - 1, acc, True)
    denom = acc[:, LANES:]
    _gated_store(o_ref, oin_ref, gates, acc[:, :LANES] / denom, hpg, tq, hpg)

    @pl.when(jnp.logical_not(jnp.min(denom) >= SOFTMAX_FLOOR))
    def _():
        q8x = _stack_aug(q, [sel_ref[...]] * hpg, hpg)

        def exact_tile(kv, carry, diagonal):
            m, acc = carry
            start = pl.multiple_of(kv * tk, tk)
            s = lax.dot_general(q8x, k_ref[pl.ds(start, tk), :], NT_DIMS, preferred_element_type=F32)
            if diagonal:
                s = s + causal_bias(start)
            m_new = jnp.maximum(m, jnp.max(s, axis=-1, keepdims=True))
            p = jnp.exp(s - m_new).astype(BF16)
            acc = jnp.exp(m - m_new) * acc + jnp.dot(p, v_ref[pl.ds(start, tk), :], preferred_element_type=F32)
            return m_new, acc

        init = (jnp.full((rows, 1), NEG_INF, F32), jnp.zeros((rows, 2 * LANES), F32))
        carry = lax.fori_loop(0, n_kv - 1, lambda kv, c: exact_tile(kv, c, False), init)
        _, acc_x = exact_tile(n_kv - 1, carry, True)
        _gated_store(o_ref, oin_ref, gates, acc_x[:, :LANES] / acc_x[:, LANES:], hpg, tq, hpg)


def _nsa_sel_attn(q, ks, vs, selb, k_gain, gates, o_in, *, groups, hpg):
    b, s, d = q.shape
    tq = 128
    tk = _pick(s, (512, 256, 128))
    qo_spec = pl.BlockSpec((None, tq, hpg * LANES), lambda bi, gi, i: (bi, i, gi))
    kv_spec = pl.BlockSpec((None, None, s, 2 * LANES), lambda bi, gi, i: (bi, gi, 0, 0))
    return pl.pallas_call(
        functools.partial(_sel_attn_kernel, tq=tq, tk=tk, hpg=hpg),
        grid=(b, groups, s // tq),
        in_specs=[
            qo_spec, kv_spec, kv_spec,
            pl.BlockSpec((None, None, tq, LANES), lambda bi, gi, i: (bi, gi, i, 0)),
            pl.BlockSpec((1, LANES), lambda bi, gi, i: (0, 0)),
            pl.BlockSpec((None, tq, LANES), lambda bi, gi, i: (bi, i, gi)),
            qo_spec,
        ],
        out_specs=qo_spec,
        out_shape=jax.ShapeDtypeStruct((b, s, d), F32),
        compiler_params=_params("parallel", "parallel", "parallel"),
        name="nsa_sel_attn",
    )(q, ks, vs, selb, k_gain.reshape(1, LANES), gates, o_in)


def _win_attn_kernel(q_ref, k_ref, v_ref, kn_ref, g_ref, oin_ref, o_ref, *, tq, span, hpg):
    i = pl.program_id(2)
    q = q_ref[...]
    start = pl.multiple_of(jnp.maximum(i * tq - WINDOW, 0), tq)
    k = k_ref[pl.ds(start, span), :]
    v = v_ref[pl.ds(start, span), :]
    tpos = i * tq + lax.broadcasted_iota(jnp.int32, (tq, 1), 0)
    kpos = start + lax.broadcasted_iota(jnp.int32, (1, span), 1)
    diff = tpos - kpos
    bias = jnp.concatenate([jnp.where((diff >= 0) & (diff < WINDOW), 0.0, NEG_INF)] * hpg, axis=0)
    gates = g_ref[...]

    q8 = _stack_aug(q, [jnp.broadcast_to(-m, (tq, LANES)).astype(BF16) for m in _score_bounds(q, kn_ref[...], hpg)],
                    hpg)
    s = lax.dot_general(q8, k, NT_DIMS, preferred_element_type=F32) + bias
    acc = jnp.dot(jnp.exp(s).astype(BF16), v, preferred_element_type=F32)
    denom = acc[:, LANES:]
    _gated_store(o_ref, oin_ref, gates, acc[:, :LANES] / denom, 2 * hpg, tq, hpg)

    @pl.when(jnp.logical_not(jnp.min(denom) >= SOFTMAX_FLOOR))
    def _():
        q8x = _stack_aug(q, [jnp.zeros((tq, LANES), BF16)] * hpg, hpg)
        sx = lax.dot_general(q8x, k, NT_DIMS, preferred_element_type=F32) + bias
        p = jnp.exp(sx - jnp.max(sx, axis=-1, keepdims=True))
        acc_x = jnp.dot(p.astype(BF16), v, preferred_element_type=F32)
        _gated_store(o_ref, oin_ref, gates, acc_x[:, :LANES] / acc_x[:, LANES:], 2 * hpg, tq, hpg)


def _nsa_win_attn(q, kw, vw, k_gain, gates, o_in, *, groups, hpg):
    b, s, d = q.shape
    tq = 128
    span = WINDOW + tq
    assert s >= span and WINDOW % tq == 0
    qo_spec = pl.BlockSpec((None, tq, hpg * LANES), lambda bi, gi, i: (bi, i, gi))
    kv_spec = pl.BlockSpec((None, None, s, 2 * LANES), lambda bi, gi, i: (bi, gi, 0, 0))
    return pl.pallas_call(
        functools.partial(_win_attn_kernel, tq=tq, span=span, hpg=hpg),
        grid=(b, groups, s // tq),
        in_specs=[qo_spec, kv_spec, kv_spec, pl.BlockSpec((1, LANES), lambda bi, gi, i: (0, 0)),
                  pl.BlockSpec((None, tq, LANES), lambda bi, gi, i: (bi, i, gi)), qo_spec],
        out_specs=qo_spec,
        out_shape=jax.ShapeDtypeStruct((b, s, d), BF16),
        compiler_params=_params("parallel", "parallel", "parallel"),
        name="nsa_win_attn",
    )(q, kw, vw, k_gain.reshape(1, LANES), gates, o_in)


def _rope_tables(seq, dim):
    inv = ROPE_THETA ** (-jnp.arange(0, dim, 2, dtype=F32) / dim)
    ang = jnp.arange(seq, dtype=F32)[:, None] * inv[None, :]
    ang = jnp.concatenate([ang, ang], axis=-1)
    sign = jnp.where(jnp.arange(dim) < dim // 2, -1.0, 1.0).astype(F32)
    return jnp.cos(ang), jnp.sin(ang) * sign


def _nsa_mixer(x2, xg, ss, b, s, w_in, q_norm, k_norm, cmp_pos, cmp_w1, cmp_w2, w_o, next_gain):
    heads, groups = NSA_HEADS, NSA_GROUPS
    hpg = heads // groups
    d = x2.shape[1]
    assert d == heads * LANES and CMP_BLOCK == 2 * CMP_STRIDE and 3 * hpg <= LANES
    qd, kvd = heads * LANES, groups * LANES
    n_main = qd + 6 * kvd
    nseg = s // CMP_STRIDE
    n_cmp = (s - CMP_BLOCK) // CMP_STRIDE + 1
    n_slc = s // SLC_BLOCK
    top_n = min(SLC_TOPN, n_slc)
    assert n_cmp == nseg - 1 and n_slc % 8 == 0 and n_slc <= LANES and top_n >= 3

    proj = _mm(xg, ss, w_in, n=n_main, out_dtype=F32, name="nsa_in_proj")
    wg = w_in[:, n_main:].reshape(d, 3, groups, hpg).transpose(0, 2, 1, 3).reshape(d, groups, 3 * hpg)
    wg = jnp.pad(wg, ((0, 0), (0, 0), (0, LANES - 3 * hpg))).reshape(d, groups * LANES).astype(BF16)
    gates = _mm(xg, ss, wg, out_dtype=F32, sigmoid=True, name="nsa_gate_proj").reshape(b, s, groups * LANES)

    cos, sin_signed = _rope_tables(s, LANES)
    q, kc, vc, ks, vs, kw, vw = _nsa_prep(proj.reshape(b, s, n_main), cos, sin_signed, q_norm, k_norm,
                                          heads=heads, groups=groups)

    seg_w = CMP_STRIDE * LANES
    pe = cmp_pos.reshape(2, 2, 1, seg_w)
    w1 = cmp_w1.reshape(2, 2, seg_w, LANES).astype(BF16)
    kcmp, vcmp = _nsa_compress(kc.reshape(b, groups, nseg, seg_w), vc.reshape(b, groups, nseg, seg_w),
                               pe, w1, cmp_w2.astype(BF16))

    cmp_start = np.arange(nseg) * CMP_STRIDE
    slc_start = np.arange(n_slc) * SLC_BLOCK
    ovl = ((cmp_start[None, :] < slc_start[:, None] + SLC_BLOCK) & (cmp_start[None, :] + CMP_BLOCK > slc_start[:, None])
           & (np.arange(nseg)[None, :] < n_cmp))
    ovl = jnp.asarray(ovl.astype(np.float32), dtype=BF16)

    o1, selm = _nsa_cmp_attn(q, kcmp, vcmp, gates, ovl, groups=groups, hpg=hpg, n_cmp=n_cmp, n_slc=n_slc, top_n=top_n)
    o2 = _nsa_sel_attn(q, ks, vs, selm, k_norm[1], gates, o1, groups=groups, hpg=hpg)
    o3 = _nsa_win_attn(q, kw, vw, k_norm[2], gates, o2, groups=groups, hpg=hpg)
    return _mm_res(o3.reshape(b * s, d), w_o, x2, 1.0, next_gain=next_gain,
                   bm_prefs=(2048, 1024, 512, 256, 128), bn_prefs=(256, 128), single_buffer_lhs=True,
                   name="nsa_out_proj")


def _chunk_cumsum(x, row):
    n = x.shape[0]
    step = 1
    while step < n:
        x = x + jnp.where(row >= step, pltpu.roll(x, step, 0), 0.0)
        step *= 2
    return x


def _hgrn_kernel(q_ref, f_ref, i_ref, g_ref, lbl_ref, on_ref, o_ref, *, seq, chunk, layer, hps, unroll):
    lbl = lbl_ref[...]
    e = jnp.exp(lbl - jnp.max(lbl, axis=0, keepdims=True))
    p = e / jnp.sum(e, axis=0, keepdims=True)
    csum = p[0:1]
    for d in range(1, layer + 1):
        csum = csum + p[d:d + 1]
    lb = csum - p[0:1]
    o_gain = on_ref[...]
    causal = (lax.broadcasted_iota(jnp.int32, (chunk, chunk), 0) >= lax.broadcasted_iota(jnp.int32, (chunk, chunk), 1))
    row = lax.broadcasted_iota(jnp.int32, (chunk, hps * LANES), 0)
    heads = [slice(k * LANES, (k + 1) * LANES) for k in range(hps)]

    def body(it, states):
        states = list(states)
        chunks = []
        for u in range(unroll):
            rows = pl.ds(pl.multiple_of((it * unroll + u) * chunk, chunk), chunk)
            f = lb + (1.0 - lb) * jax.nn.sigmoid(f_ref[rows, :])
            kc = 1.0 - f
            g_cum = _chunk_cumsum(jnp.log(f), row)
            g_last = g_cum[chunk - 1:chunk, :]
            q_dec = (q_ref[rows, :] * jnp.exp(g_cum)).astype(BF16)
            k_inv = (kc * jnp.exp(-g_cum)).astype(BF16)
            k_tail = (kc * jnp.exp(g_last - g_cum)).astype(BF16)
            chunks.append((rows, q_dec, k_inv, k_tail, i_ref[rows, :].astype(BF16), jnp.exp(g_last)))
        a_all = [[lax.dot_general(q_dec[:, h], k_inv[:, h], NT_DIMS, preferred_element_type=F32) for h in heads]
                 for (_, q_dec, k_inv, _, _, _) in chunks]
        ds_all = [[lax.dot_general(v[:, h], k_tail[:, h], TN_DIMS, preferred_element_type=F32) for h in heads]
                  for (_, _, _, k_tail, v, _) in chunks]
        o_all = [[jnp.dot(jnp.where(causal, a, 0.0).astype(BF16), v[:, h], preferred_element_type=F32)
                  for a, h in zip(a_row, heads)]
                 for a_row, (_, _, _, _, v, _) in zip(a_all, chunks)]
        for u, (rows, q_dec, _, _, _, decay) in enumerate(chunks):
            for k, h in enumerate(heads):
                o = o_all[u][k] + lax.dot_general(q_dec[:, h], states[k].astype(BF16), NT_DIMS,
                                                  preferred_element_type=F32)
                states[k] = states[k] * decay[:, h] + ds_all[u][k]
                y = o * lax.rsqrt(jnp.mean(o * o, axis=-1, keepdims=True) + RMS_EPS) * o_gain
                gz = g_ref[rows, h]
                o_ref[rows, h] = (y * (gz * jax.nn.sigmoid(gz))).astype(o_ref.dtype)
        return tuple(states)

    init = tuple(jnp.zeros((LANES, LANES), F32) for _ in range(hps))
    lax.fori_loop(0, seq // (chunk * unroll), body, init)


def _hgrn_core(proj, lb_logits, o_norm, *, heads, layer, hps=4, unroll=2):
    b, s, _ = proj.shape
    depth = lb_logits.shape[0]
    hps = math.gcd(hps, heads)
    width = hps * LANES
    groups = heads // hps
    assert s % (HGRN_CHUNK * unroll) == 0

    def col_spec(part):
        return pl.BlockSpec((None, s, width), lambda bi, hi: (bi, 0, part * groups + hi))

    return pl.pallas_call(
        functools.partial(_hgrn_kernel, seq=s, chunk=HGRN_CHUNK, layer=layer, hps=hps, unroll=unroll),
        grid=(b, groups),
        in_specs=[col_spec(0), col_spec(1), col_spec(2), col_spec(3),
                  pl.BlockSpec((depth, width), lambda bi, hi: (0, hi)),
                  pl.BlockSpec((1, LANES), lambda bi, hi: (0, 0))],
        out_specs=pl.BlockSpec((None, s, width), lambda bi, hi: (bi, 0, hi)),
        out_shape=jax.ShapeDtypeStruct((b, s, heads * LANES), BF16),
        compiler_params=_params("parallel", "parallel"),
        name="hgrn_core",
    )(proj, proj, proj, proj, lb_logits, o_norm.reshape(1, LANES))


def _hgrn_mixer(x2, xg, ss, b, s, w_in, lb_logits, o_norm, w_o, layer, next_gain):
    heads = HGRN_HEADS
    d = x2.shape[1]
    assert d == heads * LANES and w_in.shape[1] == 4 * d and s % HGRN_CHUNK == 0
    proj = _mm(xg, ss, w_in, out_dtype=F32, name="hgrn_in_proj")
    o = _hgrn_core(proj.reshape(b, s, 4 * d), lb_logits, o_norm, heads=heads, layer=layer)
    return _mm_res(o.reshape(b * s, d), w_o, x2, 1.0, next_gain=next_gain,
                   bm_prefs=(2048, 1024, 512, 256, 128), bn_prefs=(256, 128), single_buffer_lhs=True,
                   name="hgrn_out_proj")


def kernel(x, ffn_norm, ffn_w_gate, ffn_w_up, ffn_w_down, mix_norm, nsa_w_in, nsa_q_norm, nsa_k_norm, nsa_cmp_pos,
           nsa_cmp_w1, nsa_cmp_w2, nsa_w_o, hgrn_w_in, hgrn_lb_logits, hgrn_o_norm, hgrn_w_o):
    b, s, d = x.shape
    depth = ffn_norm.shape[0]
    x2 = x.reshape(b * s, d)
    w_down = ffn_w_down.astype(BF16)
    xg, ss = _prenorm(x2, ffn_norm[0, 0])
    for layer in range(depth):
        slot = layer // N_MIXERS
        x2, xg, ss = _ffn(x2, xg, ss, ffn_w_gate, ffn_w_up, w_down, (layer, 0), mix_norm[layer])
        if layer % N_MIXERS == 0:
            x2, xg, ss = _nsa_mixer(x2, xg, ss, b, s, nsa_w_in[slot], nsa_q_norm[slot], nsa_k_norm[slot],
                                    nsa_cmp_pos[slot], nsa_cmp_w1[slot], nsa_cmp_w2[slot], nsa_w_o[slot],
                                    ffn_norm[layer, 1])
        else:
            x2, xg, ss = _hgrn_mixer(x2, xg, ss, b, s, hgrn_w_in[slot], hgrn_lb_logits, hgrn_o_norm[slot],
                                     hgrn_w_o[slot], layer, ffn_norm[layer, 1])
        if layer + 1 < depth:
            x2, xg, ss = _ffn(x2, xg, ss, ffn_w_gate, ffn_w_up, w_down, (layer, 1), ffn_norm[layer + 1, 0])
        else:
            x2 = _ffn(x2, xg, ss, ffn_w_gate, ffn_w_up, w_down, (layer, 1), None)
    return x2.reshape(b, s, d)
```

```python
import functools
import math

import jax
import jax.numpy as jnp
import numpy as np
from jax import lax
from jax.experimental import pallas as pl
from jax.experimental.pallas import tpu as pltpu

F32 = jnp.float32
BF16 = jnp.bfloat16

RMS_EPS = 1e-6
NEG_INF = -1e30
FORCED_SCORE = 1e30
FFN_RES_WEIGHT = 0.5
N_MIXERS = 2

NSA_HEADS = 32
NSA_GROUPS = 4
CMP_BLOCK = 32
CMP_STRIDE = 16
SLC_BLOCK = 64
SLC_TOPN = 16
_SLC_SHIFT = SLC_BLOCK.bit_length() - 1
assert 1 << _SLC_SHIFT == SLC_BLOCK
WINDOW = 512
ROPE_THETA = 10000.0

HGRN_HEADS = 32
HGRN_CHUNK = 64

LANES = 128
SOFTMAX_FLOOR = 1e-30
BOUND_SLACK = 1.02
VMEM_LIMIT_BYTES = 56 * 1024 * 1024

NT_DIMS = (((1,), (1,)), ((), ()))
TN_DIMS = (((0,), (0,)), ((), ()))


def _params(*sem):
    return pltpu.CompilerParams(dimension_semantics=sem, vmem_limit_bytes=VMEM_LIMIT_BYTES)


def _pick(n, prefs):
    for p in prefs:
        if n % p == 0:
            return p
    raise ValueError(f"no tile in {prefs} divides {n}")


def _fold_lanes(x):
    out = x[:, :LANES]
    for t in range(1, x.shape[1] // LANES):
        out = out + x[:, t * LANES:(t + 1) * LANES]
    return out


def _inv_rms(ss, width):
    return lax.rsqrt(jnp.sum(ss, axis=-1, keepdims=True) * (1.0 / width) + RMS_EPS)


def _prenorm_kernel(x_ref, g_ref, xg_ref, ss_ref):
    x = x_ref[...]
    xg_ref[...] = (x * g_ref[...]).astype(xg_ref.dtype)
    ss_ref[...] = _fold_lanes(x * x)


def _prenorm(x, g):
    m, d = x.shape
    tr = _pick(m, (256, 128, 8))
    return pl.pallas_call(
        _prenorm_kernel,
        grid=(m // tr,),
        in_specs=[pl.BlockSpec((tr, d), lambda i: (i, 0)), pl.BlockSpec((1, d), lambda i: (0, 0))],
        out_specs=[pl.BlockSpec((tr, d), lambda i: (i, 0)), pl.BlockSpec((tr, LANES), lambda i: (i, 0))],
        out_shape=[jax.ShapeDtypeStruct((m, d), BF16), jax.ShapeDtypeStruct((m, LANES), F32)],
        compiler_params=_params("parallel"),
        name="prenorm",
    )(x, g.reshape(1, d))


def _weight_spec(w, lead, bn):
    k = w.shape[-2]
    return pl.BlockSpec((None,) * len(lead) + (k, bn), lambda i, j: lead + (0, j))


def _lhs_spec(bm, k, single_buffer):
    if single_buffer:
        return pl.BlockSpec((bm, k), lambda i, j: (i, 0), pipeline_mode=pl.Buffered(1))
    return pl.BlockSpec((bm, k), lambda i, j: (i, 0))


def _stat_spec(bm):
    return pl.BlockSpec((bm, LANES), lambda i, j: (i, 0))


def _gateup_kernel(xg_ref, ss_ref, wg_ref, wu_ref, o_ref, r_ref, *, width):
    @pl.when(pl.program_id(1) == 0)
    def _():
        r_ref[...] = jnp.broadcast_to(_inv_rms(ss_ref[...], width), r_ref.shape)

    xg = xg_ref[...]
    r = jnp.concatenate([r_ref[...]] * (o_ref.shape[1] // LANES), axis=1)
    a = jnp.dot(xg, wg_ref[...].astype(BF16), preferred_element_type=F32) * r
    b = jnp.dot(xg, wu_ref[...].astype(BF16), preferred_element_type=F32) * r
    o_ref[...] = (a * jax.nn.sigmoid(a) * b).astype(o_ref.dtype)


def _gateup(xg, ss, wg, wu, lead):
    m, k = xg.shape
    n = wg.shape[-1]
    bm = _pick(m, (2048, 1024, 512, 256, 128))
    bn = _pick(n, (256, 128))
    return pl.pallas_call(
        functools.partial(_gateup_kernel, width=k),
        grid=(m // bm, n // bn),
        in_specs=[_lhs_spec(bm, k, True), _stat_spec(bm), _weight_spec(wg, lead, bn), _weight_spec(wu, lead, bn)],
        out_specs=pl.BlockSpec((bm, bn), lambda i, j: (i, j)),
        out_shape=jax.ShapeDtypeStruct((m, n), BF16),
        scratch_shapes=[pltpu.VMEM((bm, LANES), F32)],
        compiler_params=_params("parallel", "arbitrary"),
        name="ffn_gateup",
    )(xg, ss, wg, wu)


def _mm_res_kernel(a_ref, w_ref, r_ref, *rest, scale, emit_norm):
    acc = jnp.dot(a_ref[...], w_ref[...].astype(BF16), preferred_element_type=F32)
    x = r_ref[...] + scale * acc
    if not emit_norm:
        (o_ref,) = rest
        o_ref[...] = x
        return
    gn_ref, o_ref, xg_ref, ss_ref = rest
    o_ref[...] = x
    xg_ref[...] = (x * gn_ref[...]).astype(xg_ref.dtype)

    @pl.when(pl.program_id(1) == 0)
    def _():
        ss_ref[...] = jnp.zeros_like(ss_ref)

    ss_ref[...] += _fold_lanes(x * x)


def _mm_res(a, w, res, scale, *, lead=(), next_gain=None, bm_prefs, bn_prefs, name):
    m, k = a.shape
    n = w.shape[-1]
    bm = _pick(m, bm_prefs)
    bn = _pick(n, bn_prefs)
    tile = pl.BlockSpec((bm, bn), lambda i, j: (i, j))
    emit_norm = next_gain is not None
    in_specs = [_lhs_spec(bm, k, False), _weight_spec(w, lead, bn), tile]
    out_specs, out_shape, args = tile, jax.ShapeDtypeStruct((m, n), F32), (a, w, res)
    if emit_norm:
        in_specs.append(pl.BlockSpec((1, bn), lambda i, j: (0, j)))
        out_specs = [tile, tile, _stat_spec(bm)]
        out_shape = [out_shape, jax.ShapeDtypeStruct((m, n), BF16), jax.ShapeDtypeStruct((m, LANES), F32)]
        args = args + (next_gain.reshape(1, n),)
    return pl.pallas_call(
        functools.partial(_mm_res_kernel, scale=scale, emit_norm=emit_norm),
        grid=(m // bm, n // bn),
        in_specs=in_specs,
        out_specs=out_specs,
        out_shape=out_shape,
        compiler_params=_params("parallel", "arbitrary"),
        name=name,
    )(*args)


def _mm_kernel(xg_ref, ss_ref, w_ref, o_ref, *, width, sigmoid):
    acc = jnp.dot(xg_ref[...], w_ref[...].astype(BF16), preferred_element_type=F32) * _inv_rms(ss_ref[...], width)
    if sigmoid:
        acc = jax.nn.sigmoid(acc)
    o_ref[...] = acc.astype(o_ref.dtype)


def _mm(xg, ss, w, *, lead=(), n=None, out_dtype, sigmoid=False, name):
    m, k = xg.shape
    n = w.shape[-1] if n is None else n
    bn = _pick(n, (512, 256, 128))
    tall = n // bn >= 8
    bm = _pick(m, (2048, 1024, 512, 256, 128) if tall else (1024, 512, 256, 128))
    return pl.pallas_call(
        functools.partial(_mm_kernel, width=k, sigmoid=sigmoid),
        grid=(m // bm, n // bn),
        in_specs=[_lhs_spec(bm, k, tall), _stat_spec(bm), _weight_spec(w, lead, bn)],
        out_specs=pl.BlockSpec((bm, bn), lambda i, j: (i, j)),
        out_shape=jax.ShapeDtypeStruct((m, n), out_dtype),
        compiler_params=_params("parallel", "arbitrary"),
        name=name,
    )(xg, ss, w)


def _ffn(x, xg, ss, w_gate, w_up, w_down_bf16, lead, next_gain):
    act = _gateup(xg, ss, w_gate, w_up, lead)
    return _mm_res(act, w_down_bf16, x, FFN_RES_WEIGHT, lead=lead, next_gain=next_gain,
                   bm_prefs=(512, 256, 128), bn_prefs=(512, 256, 128), name="ffn_down")


def _nsa_prep_kernel(p_ref, cos_ref, sin_ref, qn_ref, kn_ref, q_ref, kc_ref, vc_ref, ks_ref, vs_ref, kw_ref, vw_ref,
                     *, heads, groups):
    cos = cos_ref[...]
    sin = sin_ref[...]
    scale = LANES ** -0.5
    base = heads * LANES
    kv_w = groups * LANES
    ts = p_ref.shape[0]
    pos = pl.program_id(1) * ts + lax.broadcasted_iota(jnp.int32, (ts, LANES), 0)
    lane = lax.broadcasted_iota(jnp.int32, (ts, LANES), 1)
    blk_onehot = jnp.where(lane == jnp.right_shift(pos, _SLC_SHIFT), 1.0, 0.0).astype(ks_ref.dtype)
    lane0_onehot = jnp.where(lane == 0, 1.0, 0.0).astype(kw_ref.dtype)

    def store_q(h):
        def store(y):
            q_ref[:, h * LANES:(h + 1) * LANES] = y.astype(q_ref.dtype)
        return store

    def store_k(k_out, g, upper):
        def store(y):
            if upper is None:
                k_out[g] = y.astype(k_out.dtype)
            else:
                k_out[g, :, :LANES] = y.astype(k_out.dtype)
                k_out[g, :, LANES:] = upper
        return store

    jobs = [(h * LANES, qn_ref[...], scale, store_q(h)) for h in range(heads)]
    for br, (k_out, upper) in enumerate(((kc_ref, None), (ks_ref, blk_onehot), (kw_ref, lane0_onehot))):
        for g in range(groups):
            jobs.append((base + (2 * br) * kv_w + g * LANES, kn_ref[br:br + 1, :], None, store_k(k_out, g, upper)))
    inv = [lax.rsqrt(jnp.mean(jnp.square(p_ref[:, c:c + LANES]), axis=-1, keepdims=True) + RMS_EPS)
           for c, _, _, _ in jobs]
    for (c, gain, post, store), r in zip(jobs, inv):
        y = p_ref[:, c:c + LANES] * r * gain
        y = y * cos + pltpu.roll(y, LANES // 2, 1) * sin
        store(y if post is None else y * post)

    for br, v_out in enumerate((vc_ref, vs_ref, vw_ref)):
        for g in range(groups):
            cv = base + (2 * br + 1) * kv_w + g * LANES
            v = p_ref[:, cv:cv + LANES].astype(v_out.dtype)
            if br == 0:
                v_out[g] = v
            else:
                v_out[g, :, :LANES] = v
                v_out[g, :, LANES:] = jnp.ones_like(v)


def _nsa_prep(proj, cos, sin_signed, q_norm, k_norm, *, heads, groups):
    b, s, n = proj.shape
    ts = _pick(s, (256, 128))
    kv_spec = pl.BlockSpec((None, groups, ts, LANES), lambda bi, i: (bi, 0, i, 0))
    aug_spec = pl.BlockSpec((None, groups, ts, 2 * LANES), lambda bi, i: (bi, 0, i, 0))
    tab_spec = pl.BlockSpec((ts, LANES), lambda bi, i: (i, 0))

    def kv_shape(dt, width=LANES):
        return jax.ShapeDtypeStruct((b, groups, s, width), dt)

    return pl.pallas_call(
        functools.partial(_nsa_prep_kernel, heads=heads, groups=groups),
        grid=(b, s // ts),
        in_specs=[
            pl.BlockSpec((None, ts, n), lambda bi, i: (bi, i, 0)),
            tab_spec,
            tab_spec,
            pl.BlockSpec((1, LANES), lambda bi, i: (0, 0)),
            pl.BlockSpec((3, LANES), lambda bi, i: (0, 0)),
        ],
        out_specs=[pl.BlockSpec((None, ts, heads * LANES), lambda bi, i: (bi, i, 0)),
                   kv_spec, kv_spec, aug_spec, aug_spec, aug_spec, aug_spec],
        out_shape=[jax.ShapeDtypeStruct((b, s, heads * LANES), BF16),
                   kv_shape(F32), kv_shape(F32), kv_shape(BF16, 2 * LANES), kv_shape(BF16, 2 * LANES),
                   kv_shape(BF16, 2 * LANES), kv_shape(BF16, 2 * LANES)],
        compiler_params=_params("parallel", "parallel"),
        name="nsa_prep",
    )(proj, cos, sin_signed, q_norm.reshape(1, LANES), k_norm)


def _gelu_tanh(x):
    return x * (0.5 * (1.0 + jnp.tanh(math.sqrt(2.0 / math.pi) * (x + 0.044715 * (x * x * x)))))


def _compress_kernel(k_ref, v_ref, pe_ref, w1_ref, w2_ref, ko_ref, vo_ref):
    for which, (x_ref, o_ref) in enumerate(((k_ref, ko_ref), (v_ref, vo_ref))):
        x = x_ref[...]
        nseg = x.shape[0]
        top = jnp.dot((x + pe_ref[which, 0]).astype(BF16), w1_ref[which, 0], preferred_element_type=F32)
        bot = jnp.dot((x + pe_ref[which, 1]).astype(BF16), w1_ref[which, 1], preferred_element_type=F32)
        pre = top + pltpu.roll(bot, nseg - 1, 0)
        hid = _gelu_tanh(pre).astype(BF16)
        o_ref[...] = jnp.dot(hid, w2_ref[which], preferred_element_type=F32).astype(o_ref.dtype)


def _nsa_compress(kcf, vcf, pe, w1, w2):
    b, g, nseg, width = kcf.shape
    x_spec = pl.BlockSpec((None, None, nseg, width), lambda bi, gi: (bi, gi, 0, 0))
    o_spec = pl.BlockSpec((None, None, nseg, LANES), lambda bi, gi: (bi, gi, 0, 0))
    o_shape = jax.ShapeDtypeStruct((b, g, nseg, LANES), BF16)
    return pl.pallas_call(
        _compress_kernel,
        grid=(b, g),
        in_specs=[
            x_spec, x_spec,
            pl.BlockSpec((2, 2, 1, width), lambda bi, gi: (0, 0, 0, 0)),
            pl.BlockSpec((2, 2, width, LANES), lambda bi, gi: (0, 0, 0, 0)),
            pl.BlockSpec((2, LANES, LANES), lambda bi, gi: (0, 0, 0)),
        ],
        out_specs=[o_spec, o_spec],
        out_shape=[o_shape, o_shape],
        compiler_params=_params("parallel", "parallel"),
        name="nsa_compress",
    )(kcf, vcf, pe, w1, w2)


def _stack_heads(q, hpg):
    return jnp.concatenate([q[:, h * LANES:(h + 1) * LANES] for h in range(hpg)], axis=0)


def _cmp_attn_kernel(q_ref, kc_ref, vc_ref, g_ref, ovl_ref, o_ref, sel_ref, *, tq, hpg, n_cmp, n_slc, top_n):
    i = pl.program_id(2)
    q8 = _stack_heads(q_ref[...], hpg)
    rows = hpg * tq
    ncp = kc_ref.shape[0]
    s = lax.dot_general(q8, kc_ref[...], NT_DIMS, preferred_element_type=F32)
    tpos = i * tq + lax.broadcasted_iota(jnp.int32, (tq, 1), 0)
    tpos8 = jnp.concatenate([tpos] * hpg, axis=0)
    ncol = lax.broadcasted_iota(jnp.int32, (1, ncp), 1)
    valid = jnp.where((ncol * CMP_STRIDE + (CMP_BLOCK - 1) <= tpos8) & (ncol < n_cmp), 1.0, 0.0)
    sm = jnp.where(valid > 0.5, s, NEG_INF)
    m = jnp.max(sm, axis=-1, keepdims=True)
    e = jnp.exp(sm - m) * valid
    l = jnp.sum(e, axis=-1, keepdims=True)
    p = e / jnp.where(l > 0.0, l, 1.0)
    o = jnp.dot(p.astype(BF16), vc_ref[...], preferred_element_type=F32)
    gates = g_ref[...]
    for h in range(hpg):
        o_ref[:, h * LANES:(h + 1) * LANES] = o[h * tq:(h + 1) * tq] * gates[:, h:h + 1]

    psum = p[0:tq]
    for h in range(1, hpg):
        psum = psum + p[h * tq:(h + 1) * tq]
    hi = psum.astype(BF16)
    lo = (psum - hi.astype(F32)).astype(BF16)
    ovl = ovl_ref[...]
    imp = (lax.dot_general(ovl, hi, NT_DIMS, preferred_element_type=F32)
           + lax.dot_general(ovl, lo, NT_DIMS, preferred_element_type=F32))
    jidx = lax.broadcasted_iota(jnp.int32, (n_slc, tq), 0)
    tq_pos = i * tq + lax.broadcasted_iota(jnp.int32, (n_slc, tq), 1)
    cur = jnp.right_shift(tq_pos, _SLC_SHIFT)
    forced = (jidx == 0) | (jidx == cur) | (jidx == cur - 1)
    causal = jidx * SLC_BLOCK <= tq_pos
    imp = jnp.where(forced, FORCED_SCORE, jnp.where(causal, imp, NEG_INF))
    sel_rows = []
    for j in range(n_slc):
        row = imp[j:j + 1, :]
        lower = jnp.where(jidx < j, 1.0, 0.0)
        beats = jnp.where(imp > row, 1.0, jnp.where(imp == row, lower, 0.0))
        rank = jnp.sum(beats, axis=0, keepdims=True)
        sel_rows.append(jnp.where(rank < top_n, 0.0, NEG_INF))
    sel_rows.append(jnp.zeros((LANES - n_slc, tq), F32))
    bias_t = jnp.concatenate(sel_rows, axis=0)
    sel_ref[...] = bias_t.T.astype(sel_ref.dtype)


def _nsa_cmp_attn(q, kcmp, vcmp, gates, ovl, *, groups, hpg, n_cmp, n_slc, top_n):
    b, s, d = q.shape
    ncp = kcmp.shape[2]
    tq = _pick(s, (512, 256, 128))
    qo_spec = pl.BlockSpec((None, tq, hpg * LANES), lambda bi, gi, i: (bi, i, gi))
    c_spec = pl.BlockSpec((None, None, ncp, LANES), lambda bi, gi, i: (bi, gi, 0, 0))
    return pl.pallas_call(
        functools.partial(_cmp_attn_kernel, tq=tq, hpg=hpg, n_cmp=n_cmp, n_slc=n_slc, top_n=top_n),
        grid=(b, groups, s // tq),
        in_specs=[
            qo_spec, c_spec, c_spec,
            pl.BlockSpec((None, tq, LANES), lambda bi, gi, i: (bi, i, gi)),
            pl.BlockSpec((n_slc, ncp), lambda bi, gi, i: (0, 0)),
        ],
        out_specs=[qo_spec, pl.BlockSpec((None, None, tq, LANES), lambda bi, gi, i: (bi, gi, i, 0))],
        out_shape=[jax.ShapeDtypeStruct((b, s, d), F32), jax.ShapeDtypeStruct((b, groups, s, LANES), BF16)],
        compiler_params=_params("parallel", "parallel", "parallel"),
        name="nsa_cmp_attn",
    )(q, kcmp, vcmp, gates, ovl)


def _stack_aug(q, upper, hpg):
    return jnp.concatenate(
        [jnp.concatenate([q[:, h * LANES:(h + 1) * LANES], upper[h]], axis=1) for h in range(hpg)], axis=0)


def _score_bounds(q, k_gain, hpg):
    k_bound = BOUND_SLACK * math.sqrt(LANES) * jnp.max(jnp.abs(k_gain), axis=-1, keepdims=True)
    out = []
    for h in range(hpg):
        qh = q[:, h * LANES:(h + 1) * LANES].astype(F32)
        out.append(jnp.sqrt(jnp.sum(qh * qh, axis=-1, keepdims=True)) * k_bound)
    return out


def _gated_store(o_ref, oin_ref, gates, o, first_gate, tq, hpg):
    for h in range(hpg):
        cols = slice(h * LANES, (h + 1) * LANES)
        gate = gates[:, first_gate + h:first_gate + h + 1]
        o_ref[:, cols] = (oin_ref[:, cols] + o[h * tq:(h + 1) * tq] * gate).astype(o_ref.dtype)


def _sel_attn_kernel(q_ref, k_ref, v_ref, sel_ref, kn_ref, g_ref, oin_ref, o_ref, *, tq, tk, hpg):
    i = pl.program_id(2)
    q = q_ref[...]
    bias = sel_ref[...].astype(F32)
    rows = hpg * tq
    tpos = i * tq + lax.broadcasted_iota(jnp.int32, (tq, 1), 0)
    n_kv = ((i + 1) * tq + tk - 1) // tk
    gates = g_ref[...]

    def causal_bias(start):
        kpos = start + lax.broadcasted_iota(jnp.int32, (1, tk), 1)
        return jnp.concatenate([jnp.where(kpos <= tpos, 0.0, NEG_INF)] * hpg, axis=0)

    q8 = _stack_aug(q, [(bias - m).astype(BF16) for m in _score_bounds(q, kn_ref[...], hpg)], hpg)

    def fast_tile(kv, acc, diagonal):
        start = pl.multiple_of(kv * tk, tk)
        s = lax.dot_general(q8, k_ref[pl.ds(start, tk), :], NT_DIMS, preferred_element_type=F32)
        if diagonal:
            s = s + causal_bias(start)
        return acc + jnp.dot(jnp.exp(s).astype(BF16), v_ref[pl.ds(start, tk), :], preferred_element_type=F32)

    acc = lax.fori_loop(0, n_kv - 1, lambda kv, a: fast_tile(kv, a, False), jnp.zeros((rows, 2 * LANES), F32))
    acc = fast_tile(n_kv - 1, acc, True)
    denom = acc[:, LANES:]
    _gated_store(o_ref, oin_ref, gates, acc[:, :LANES] / denom, hpg, tq, hpg)

    @pl.when(jnp.logical_not(jnp.min(denom) >= SOFTMAX_FLOOR))
    def _():
        q8x = _stack_aug(q, [sel_ref[...]] * hpg, hpg)

        def exact_tile(kv, carry, diagonal):
            m, acc = carry
            start = pl.multiple_of(kv * tk, tk)
            s = lax.dot_general(q8x, k_ref[pl.ds(start, tk), :], NT_DIMS, preferred_element_type=F32)
            if diagonal:
                s = s + causal_bias(start)
            m_new = jnp.maximum(m, jnp.max(s, axis=-1, keepdims=True))
            p = jnp.exp(s - m_new).astype(BF16)
            acc = jnp.exp(m - m_new) * acc + jnp.dot(p, v_ref[pl.ds(start, tk), :], preferred_element_type=F32)
            return m_new, acc

        init = (jnp.full((rows, 1), NEG_INF, F32), jnp.zeros((rows, 2 * LANES), F32))
        carry = lax.fori_loop(0, n_kv - 1, lambda kv, c: exact_tile(kv, c, False), init)
        _, acc_x = exact_tile(n_kv - 1, carry, True)
        _gated_store(o_ref, oin_ref, gates, acc_x[:, :LANES] / acc_x[:, LANES:], hpg, tq, hpg)


def _nsa_sel_attn(q, ks, vs, selb, k_gain, gates, o_in, *, groups, hpg):
    b, s, d = q.shape
    tq = _pick(s, (512, 256, 128))
    tk = _pick(s, (512, 256, 128))
    qo_spec = pl.BlockSpec((None, tq, hpg * LANES), lambda bi, gi, i: (bi, i, gi))
    kv_spec = pl.BlockSpec((None, None, s, 2 * LANES), lambda bi, gi, i: (bi, gi, 0, 0))
    return pl.pallas_call(
        functools.partial(_sel_attn_kernel, tq=tq, tk=tk, hpg=hpg),
        grid=(b, groups, s // tq),
        in_specs=[
            qo_spec, kv_spec, kv_spec,
            pl.BlockSpec((None, None, tq, LANES), lambda bi, gi, i: (bi, gi, i, 0)),
            pl.BlockSpec((1, LANES), lambda bi, gi, i: (0, 0)),
            pl.BlockSpec((None, tq, LANES), lambda bi, gi, i: (bi, i, gi)),
            qo_spec,
        ],
        out_specs=qo_spec,
        out_shape=jax.ShapeDtypeStruct((b, s, d), F32),
        compiler_params=_params("parallel", "parallel", "parallel"),
        name="nsa_sel_attn",
    )(q, ks, vs, selb, k_gain.reshape(1, LANES), gates, o_in)


def _win_attn_kernel(q_ref, k_ref, v_ref, kn_ref, g_ref, oin_ref, o_ref, *, tq, span, hpg):
    i = pl.program_id(2)
    q = q_ref[...]
    start = pl.multiple_of(jnp.maximum(i * tq - WINDOW, 0), tq)
    k = k_ref[pl.ds(start, span), :]
    v = v_ref[pl.ds(start, span), :]
    tpos = i * tq + lax.broadcasted_iota(jnp.int32, (tq, 1), 0)
    kpos = start + lax.broadcasted_iota(jnp.int32, (1, span), 1)
    diff = tpos - kpos
    bias = jnp.concatenate([jnp.where((diff >= 0) & (diff < WINDOW), 0.0, NEG_INF)] * hpg, axis=0)
    gates = g_ref[...]

    q8 = _stack_aug(q, [jnp.broadcast_to(-m, (tq, LANES)).astype(BF16) for m in _score_bounds(q, kn_ref[...], hpg)],
                    hpg)
    s = lax.dot_general(q8, k, NT_DIMS, preferred_element_type=F32) + bias
    acc = jnp.dot(jnp.exp(s).astype(BF16), v, preferred_element_type=F32)
    denom = acc[:, LANES:]
    _gated_store(o_ref, oin_ref, gates, acc[:, :LANES] / denom, 2 * hpg, tq, hpg)

    @pl.when(jnp.logical_not(jnp.min(denom) >= SOFTMAX_FLOOR))
    def _():
        q8x = _stack_aug(q, [jnp.zeros((tq, LANES), BF16)] * hpg, hpg)
        sx = lax.dot_general(q8x, k, NT_DIMS, preferred_element_type=F32) + bias
        p = jnp.exp(sx - jnp.max(sx, axis=-1, keepdims=True))
        acc_x = jnp.dot(p.astype(BF16), v, preferred_element_type=F32)
        _gated_store(o_ref, oin_ref, gates, acc_x[:, :LANES] / acc_x[:, LANES:], 2 * hpg, tq, hpg)


def _nsa_win_attn(q, kw, vw, k_gain, gates, o_in, *, groups, hpg):
    b, s, d = q.shape
    tq = _pick(s, (256, 128))
    span = WINDOW + tq
    assert s >= span and WINDOW % tq == 0
    qo_spec = pl.BlockSpec((None, tq, hpg * LANES), lambda bi, gi, i: (bi, i, gi))
    kv_spec = pl.BlockSpec((None, None, s, 2 * LANES), lambda bi, gi, i: (bi, gi, 0, 0))
    return pl.pallas_call(
        functools.partial(_win_attn_kernel, tq=tq, span=span, hpg=hpg),
        grid=(b, groups, s // tq),
        in_specs=[qo_spec, kv_spec, kv_spec, pl.BlockSpec((1, LANES), lambda bi, gi, i: (0, 0)),
                  pl.BlockSpec((None, tq, LANES), lambda bi, gi, i: (bi, i, gi)), qo_spec],
        out_specs=qo_spec,
        out_shape=jax.ShapeDtypeStruct((b, s, d), BF16),
        compiler_params=_params("parallel", "parallel", "parallel"),
        name="nsa_win_attn",
    )(q, kw, vw, k_gain.reshape(1, LANES), gates, o_in)


def _rope_tables(seq, dim):
    inv = ROPE_THETA ** (-jnp.arange(0, dim, 2, dtype=F32) / dim)
    ang = jnp.arange(seq, dtype=F32)[:, None] * inv[None, :]
    ang = jnp.concatenate([ang, ang], axis=-1)
    sign = jnp.where(jnp.arange(dim) < dim // 2, -1.0, 1.0).astype(F32)
    return jnp.cos(ang), jnp.sin(ang) * sign


def _nsa_mixer(x2, xg, ss, b, s, w_in, q_norm, k_norm, cmp_pos, cmp_w1, cmp_w2, w_o, next_gain):
    heads, groups = NSA_HEADS, NSA_GROUPS
    hpg = heads // groups
    d = x2.shape[1]
    assert d == heads * LANES and CMP_BLOCK == 2 * CMP_STRIDE and 3 * hpg <= LANES
    qd, kvd = heads * LANES, groups * LANES
    n_main = qd + 6 * kvd
    nseg = s // CMP_STRIDE
    n_cmp = (s - CMP_BLOCK) // CMP_STRIDE + 1
    n_slc = s // SLC_BLOCK
    top_n = min(SLC_TOPN, n_slc)
    assert n_cmp == nseg - 1 and n_slc % 8 == 0 and n_slc <= LANES and top_n >= 3

    proj = _mm(xg, ss, w_in, n=n_main, out_dtype=F32, name="nsa_in_proj")
    wg = w_in[:, n_main:].reshape(d, 3, groups, hpg).transpose(0, 2, 1, 3).reshape(d, groups, 3 * hpg)
    wg = jnp.pad(wg, ((0, 0), (0, 0), (0, LANES - 3 * hpg))).reshape(d, groups * LANES).astype(BF16)
    gates = _mm(xg, ss, wg, out_dtype=F32, sigmoid=True, name="nsa_gate_proj").reshape(b, s, groups * LANES)

    cos, sin_signed = _rope_tables(s, LANES)
    q, kc, vc, ks, vs, kw, vw = _nsa_prep(proj.reshape(b, s, n_main), cos, sin_signed, q_norm, k_norm,
                                          heads=heads, groups=groups)

    seg_w = CMP_STRIDE * LANES
    pe = cmp_pos.reshape(2, 2, 1, seg_w)
    w1 = cmp_w1.reshape(2, 2, seg_w, LANES).astype(BF16)
    kcmp, vcmp = _nsa_compress(kc.reshape(b, groups, nseg, seg_w), vc.reshape(b, groups, nseg, seg_w),
                               pe, w1, cmp_w2.astype(BF16))

    cmp_start = np.arange(nseg) * CMP_STRIDE
    slc_start = np.arange(n_slc) * SLC_BLOCK
    ovl = ((cmp_start[None, :] < slc_start[:, None] + SLC_BLOCK) & (cmp_start[None, :] + CMP_BLOCK > slc_start[:, None])
           & (np.arange(nseg)[None, :] < n_cmp))
    ovl = jnp.asarray(ovl.astype(np.float32), dtype=BF16)

    o1, selm = _nsa_cmp_attn(q, kcmp, vcmp, gates, ovl, groups=groups, hpg=hpg, n_cmp=n_cmp, n_slc=n_slc, top_n=top_n)
    o2 = _nsa_sel_attn(q, ks, vs, selm, k_norm[1], gates, o1, groups=groups, hpg=hpg)
    o3 = _nsa_win_attn(q, kw, vw, k_norm[2], gates, o2, groups=groups, hpg=hpg)
    return _mm_res(o3.reshape(b * s, d), w_o.astype(BF16), x2, 1.0, next_gain=next_gain,
                   bm_prefs=(1024, 512, 256, 128), bn_prefs=(512, 256, 128), name="nsa_out_proj")


def _chunk_cumsum(x, row):
    n = x.shape[0]
    step = 1
    while step < n:
        x = x + jnp.where(row >= step, pltpu.roll(x, step, 0), 0.0)
        step *= 2
    return x


def _hgrn_kernel(q_ref, f_ref, i_ref, g_ref, lbl_ref, on_ref, o_ref, *, seq, chunk, layer, hps, unroll):
    lbl = lbl_ref[...]
    e = jnp.exp(lbl - jnp.max(lbl, axis=0, keepdims=True))
    p = e / jnp.sum(e, axis=0, keepdims=True)
    csum = p[0:1]
    for d in range(1, layer + 1):
        csum = csum + p[d:d + 1]
    lb = csum - p[0:1]
    o_gain = on_ref[...]
    causal = (lax.broadcasted_iota(jnp.int32, (chunk, chunk), 0) >= lax.broadcasted_iota(jnp.int32, (chunk, chunk), 1))
    row = lax.broadcasted_iota(jnp.int32, (chunk, hps * LANES), 0)
    heads = [slice(k * LANES, (k + 1) * LANES) for k in range(hps)]

    def body(it, states):
        states = list(states)
        chunks = []
        for u in range(unroll):
            rows = pl.ds(pl.multiple_of((it * unroll + u) * chunk, chunk), chunk)
            f = lb + (1.0 - lb) * jax.nn.sigmoid(f_ref[rows, :])
            kc = 1.0 - f
            g_cum = _chunk_cumsum(jnp.log(f), row)
            g_last = g_cum[chunk - 1:chunk, :]
            q_dec = (q_ref[rows, :] * jnp.exp(g_cum)).astype(BF16)
            k_inv = (kc * jnp.exp(-g_cum)).astype(BF16)
            k_tail = (kc * jnp.exp(g_last - g_cum)).astype(BF16)
            chunks.append((rows, q_dec, k_inv, k_tail, i_ref[rows, :].astype(BF16), jnp.exp(g_last)))
        a_all = [[lax.dot_general(q_dec[:, h], k_inv[:, h], NT_DIMS, preferred_element_type=F32) for h in heads]
                 for (_, q_dec, k_inv, _, _, _) in chunks]
        ds_all = [[lax.dot_general(v[:, h], k_tail[:, h], TN_DIMS, preferred_element_type=F32) for h in heads]
                  for (_, _, _, k_tail, v, _) in chunks]
        o_all = [[jnp.dot(jnp.where(causal, a, 0.0).astype(BF16), v[:, h], preferred_element_type=F32)
                  for a, h in zip(a_row, heads)]
                 for a_row, (_, _, _, _, v, _) in zip(a_all, chunks)]
        for u, (rows, q_dec, _, _, _, decay) in enumerate(chunks):
            for k, h in enumerate(heads):
                o = o_all[u][k] + lax.dot_general(q_dec[:, h], states[k].astype(BF16), NT_DIMS,
                                                  preferred_element_type=F32)
                states[k] = states[k] * decay[:, h] + ds_all[u][k]
                y = o * lax.rsqrt(jnp.mean(o * o, axis=-1, keepdims=True) + RMS_EPS) * o_gain
                gz = g_ref[rows, h]
                o_ref[rows, h] = (y * (gz * jax.nn.sigmoid(gz))).astype(o_ref.dtype)
        return tuple(states)

    init = tuple(jnp.zeros((LANES, LANES), F32) for _ in range(hps))
    lax.fori_loop(0, seq // (chunk * unroll), body, init)


def _hgrn_core(proj, lb_logits, o_norm, *, heads, layer, hps=4, unroll=4):
    b, s, _ = proj.shape
    depth = lb_logits.shape[0]
    hps = math.gcd(hps, heads)
    width = hps * LANES
    groups = heads // hps
    assert s % (HGRN_CHUNK * unroll) == 0

    def col_spec(part):
        return pl.BlockSpec((None, s, width), lambda bi, hi: (bi, 0, part * groups + hi))

    return pl.pallas_call(
        functools.partial(_hgrn_kernel, seq=s, chunk=HGRN_CHUNK, layer=layer, hps=hps, unroll=unroll),
        grid=(b, groups),
        in_specs=[col_spec(0), col_spec(1), col_spec(2), col_spec(3),
                  pl.BlockSpec((depth, width), lambda bi, hi: (0, hi)),
                  pl.BlockSpec((1, LANES), lambda bi, hi: (0, 0))],
        out_specs=pl.BlockSpec((None, s, width), lambda bi, hi: (bi, 0, hi)),
        out_shape=jax.ShapeDtypeStruct((b, s, heads * LANES), BF16),
        compiler_params=_params("parallel", "parallel"),
        name="hgrn_core",
    )(proj, proj, proj, proj, lb_logits, o_norm.reshape(1, LANES))


def _hgrn_mixer(x2, xg, ss, b, s, w_in, lb_logits, o_norm, w_o, layer, next_gain):
    heads = HGRN_HEADS
    d = x2.shape[1]
    assert d == heads * LANES and w_in.shape[1] == 4 * d and s % HGRN_CHUNK == 0
    proj = _mm(xg, ss, w_in, out_dtype=F32, name="hgrn_in_proj")
    o = _hgrn_core(proj.reshape(b, s, 4 * d), lb_logits, o_norm, heads=heads, layer=layer)
    return _mm_res(o.reshape(b * s, d), w_o.astype(BF16), x2, 1.0, next_gain=next_gain,
                   bm_prefs=(1024, 512, 256, 128), bn_prefs=(512, 256, 128), name="hgrn_out_proj")


def kernel(x, ffn_norm, ffn_w_gate, ffn_w_up, ffn_w_down, mix_norm, nsa_w_in, nsa_q_norm, nsa_k_norm, nsa_cmp_pos,
           nsa_cmp_w1, nsa_cmp_w2, nsa_w_o, hgrn_w_in, hgrn_lb_logits, hgrn_o_norm, hgrn_w_o):
    b, s, d = x.shape
    depth = ffn_norm.shape[0]
    x2 = x.reshape(b * s, d)
    w_down = ffn_w_down.astype(BF16)
    xg, ss = _prenorm(x2, ffn_norm[0, 0])
    for layer in range(depth):
        slot = layer // N_MIXERS
        x2, xg, ss = _ffn(x2, xg, ss, ffn_w_gate, ffn_w_up, w_down, (layer, 0), mix_norm[layer])
        if layer % N_MIXERS == 0:
            x2, xg, ss = _nsa_mixer(x2, xg, ss, b, s, nsa_w_in[slot], nsa_q_norm[slot], nsa_k_norm[slot],
                                    nsa_cmp_pos[slot], nsa_cmp_w1[slot], nsa_cmp_w2[slot], nsa_w_o[slot],
                                    ffn_norm[layer, 1])
        else:
            x2, xg, ss = _hgrn_mixer(x2, xg, ss, b, s, hgrn_w_in[slot], hgrn_lb_logits, hgrn_o_norm[slot],
                                     hgrn_w_o[slot], layer, ffn_norm[layer, 1])
        if layer + 1 < depth:
            x2, xg, ss = _ffn(x2, xg, ss, ffn_w_gate, ffn_w_up, w_down, (layer, 1), ffn_norm[layer + 1, 0])
        else:
            x2 = _ffn(x2, xg, ss, ffn_w_gate, ffn_w_up, w_down, (layer, 1), None)
    return x2.reshape(b, s, d)
```

```python
import functools
import math

import jax
import jax.numpy as jnp
import numpy as np
from jax import lax
from jax.experimental import pallas as pl
from jax.experimental.pallas import tpu as pltpu

F32 = jnp.float32
BF16 = jnp.bfloat16

RMS_EPS = 1e-6
NEG_INF = -1e30
FORCED_SCORE = 1e30
FFN_RES_WEIGHT = 0.5
N_MIXERS = 2

NSA_HEADS = 32
NSA_GROUPS = 4
CMP_BLOCK = 32
CMP_STRIDE = 16
SLC_BLOCK = 64
SLC_TOPN = 16
_SLC_SHIFT = SLC_BLOCK.bit_length() - 1
assert 1 << _SLC_SHIFT == SLC_BLOCK
WINDOW = 512
ROPE_THETA = 10000.0

HGRN_HEADS = 32
HGRN_CHUNK = 64

LANES = 128
SOFTMAX_FLOOR = 1e-30
BOUND_SLACK = 1.02
VMEM_LIMIT_BYTES = 56 * 1024 * 1024

NT_DIMS = (((1,), (1,)), ((), ()))
TN_DIMS = (((0,), (0,)), ((), ()))


def _params(*sem):
    return pltpu.CompilerParams(dimension_semantics=sem, vmem_limit_bytes=VMEM_LIMIT_BYTES)


def _pick(n, prefs):
    for p in prefs:
        if n % p == 0:
            return p
    raise ValueError(f"no tile in {prefs} divides {n}")


def _fold_lanes(x):
    out = x[:, :LANES]
    for t in range(1, x.shape[1] // LANES):
        out = out + x[:, t * LANES:(t + 1) * LANES]
    return out


def _inv_rms(ss, width):
    total = jnp.sum(jnp.sum(ss, axis=0), axis=-1, keepdims=True)
    return lax.rsqrt(total * (1.0 / width) + RMS_EPS)


def _prenorm_kernel(x_ref, g_ref, xg_ref, ss_ref):
    x = x_ref[...]
    xg_ref[...] = (x * g_ref[...]).astype(xg_ref.dtype)
    ss_ref[0] = _fold_lanes(x * x)


def _prenorm(x, g):
    m, d = x.shape
    tr = _pick(m, (256, 128, 8))
    return pl.pallas_call(
        _prenorm_kernel,
        grid=(m // tr,),
        in_specs=[pl.BlockSpec((tr, d), lambda i: (i, 0)), pl.BlockSpec((1, d), lambda i: (0, 0))],
        out_specs=[pl.BlockSpec((tr, d), lambda i: (i, 0)), pl.BlockSpec((1, tr, LANES), lambda i: (0, i, 0))],
        out_shape=[jax.ShapeDtypeStruct((m, d), BF16), jax.ShapeDtypeStruct((1, m, LANES), F32)],
        compiler_params=_params("parallel"),
        name="prenorm",
    )(x, g.reshape(1, d))


def _weight_spec(w, lead, bn, first_tile=0):
    k = w.shape[-2]
    return pl.BlockSpec((None,) * len(lead) + (k, bn), lambda i, j: lead + (0, first_tile + j))


def _lhs_spec(bm, k, single_buffer):
    if single_buffer:
        return pl.BlockSpec((bm, k), lambda i, j: (i, 0), pipeline_mode=pl.Buffered(1))
    return pl.BlockSpec((bm, k), lambda i, j: (i, 0))


def _stat_spec(planes, bm):
    return pl.BlockSpec((planes, bm, LANES), lambda i, j: (0, i, 0))


def _gateup_kernel(xg_ref, ss_ref, wg_ref, wu_ref, o_ref, r_ref, *, width):
    @pl.when(pl.program_id(1) == 0)
    def _():
        r_ref[...] = jnp.broadcast_to(_inv_rms(ss_ref[...], width), r_ref.shape)

    xg = xg_ref[...]
    r = jnp.concatenate([r_ref[...]] * (o_ref.shape[1] // LANES), axis=1)
    a = jnp.dot(xg, wg_ref[...].astype(BF16), preferred_element_type=F32) * r
    b = jnp.dot(xg, wu_ref[...].astype(BF16), preferred_element_type=F32) * r
    o_ref[...] = (a * jax.nn.sigmoid(a) * b).astype(o_ref.dtype)


def _gateup(xg, ss, wg, wu, lead):
    m, k = xg.shape
    n = wg.shape[-1]
    bm = _pick(m, (2048, 1024, 512, 256, 128))
    bn = _pick(n, (256, 128))
    return pl.pallas_call(
        functools.partial(_gateup_kernel, width=k),
        grid=(m // bm, n // bn),
        in_specs=[_lhs_spec(bm, k, True), _stat_spec(ss.shape[0], bm), _weight_spec(wg, lead, bn),
                  _weight_spec(wu, lead, bn)],
        out_specs=pl.BlockSpec((bm, bn), lambda i, j: (i, j)),
        out_shape=jax.ShapeDtypeStruct((m, n), BF16),
        scratch_shapes=[pltpu.VMEM((bm, LANES), F32)],
        compiler_params=_params("parallel", "arbitrary"),
        name="ffn_gateup",
    )(xg, ss, wg, wu)


def _mm_res_kernel(a_ref, w_ref, r_ref, *rest, scale, emit_norm):
    acc = jnp.dot(a_ref[...], w_ref[...].astype(BF16), preferred_element_type=F32)
    x = r_ref[...] + scale * acc
    if not emit_norm:
        (o_ref,) = rest
        o_ref[...] = x
        return
    gn_ref, o_ref, xg_ref, ss_ref = rest
    o_ref[...] = x
    xg_ref[...] = (x * gn_ref[...]).astype(xg_ref.dtype)

    @pl.when(pl.program_id(1) == 0)
    def _():
        ss_ref[...] = jnp.zeros_like(ss_ref)

    ss_ref[0] += _fold_lanes(x * x)


def _mm_res(a, w, res, scale, *, lead=(), next_gain=None, bm_prefs, bn_prefs, name):
    m, k = a.shape
    n = w.shape[-1]
    bm = _pick(m, bm_prefs)
    bn = _pick(n, bn_prefs)
    tile = pl.BlockSpec((bm, bn), lambda i, j: (i, j))
    emit_norm = next_gain is not None
    in_specs = [_lhs_spec(bm, k, False), _weight_spec(w, lead, bn), tile]
    out_specs, out_shape, args = tile, jax.ShapeDtypeStruct((m, n), F32), (a, w, res)
    if emit_norm:
        in_specs.append(pl.BlockSpec((1, bn), lambda i, j: (0, j)))
        out_specs = [tile, tile, _stat_spec(1, bm)]
        out_shape = [out_shape, jax.ShapeDtypeStruct((m, n), BF16), jax.ShapeDtypeStruct((1, m, LANES), F32)]
        args = args + (next_gain.reshape(1, n),)
    return pl.pallas_call(
        functools.partial(_mm_res_kernel, scale=scale, emit_norm=emit_norm),
        grid=(m // bm, n // bn),
        in_specs=in_specs,
        out_specs=out_specs,
        out_shape=out_shape,
        compiler_params=_params("parallel", "arbitrary"),
        name=name,
    )(*args)


def _mm_kernel(xg_ref, ss_ref, w_ref, o_ref, *, width, sigmoid):
    acc = jnp.dot(xg_ref[...], w_ref[...].astype(BF16), preferred_element_type=F32) * _inv_rms(ss_ref[...], width)
    if sigmoid:
        acc = jax.nn.sigmoid(acc)
    o_ref[...] = acc.astype(o_ref.dtype)


def _mm(xg, ss, w, *, col0=0, n=None, out_dtype, sigmoid=False, name):
    m, k = xg.shape
    n = w.shape[-1] if n is None else n
    bn = _pick(n, (512, 256, 128))
    assert col0 % bn == 0
    tall = n // bn >= 8
    bm = _pick(m, (2048, 1024, 512, 256, 128) if tall else (1024, 512, 256, 128))
    return pl.pallas_call(
        functools.partial(_mm_kernel, width=k, sigmoid=sigmoid),
        grid=(m // bm, n // bn),
        in_specs=[_lhs_spec(bm, k, tall), _stat_spec(ss.shape[0], bm), _weight_spec(w, (), bn, col0 // bn)],
        out_specs=pl.BlockSpec((bm, bn), lambda i, j: (i, j)),
        out_shape=jax.ShapeDtypeStruct((m, n), out_dtype),
        compiler_params=_params("parallel", "arbitrary"),
        name=name,
    )(xg, ss, w)


def _ffn_down_kernel(a_ref, w_ref, r_ref, *rest, scale, emit_norm):
    x = r_ref[...] + scale * jnp.dot(a_ref[...], w_ref[...], preferred_element_type=F32)
    if not emit_norm:
        (o_ref,) = rest
        o_ref[...] = x
        return
    gn_ref, o_ref, xg_ref, ss_ref = rest
    o_ref[...] = x
    xg_ref[...] = (x * gn_ref[...]).astype(xg_ref.dtype)
    ss_ref[...] = _fold_lanes(x * x)


def _ffn_down(act, w, res, scale, lead, next_gain):
    m, k = act.shape
    n = w.shape[-1]
    bm = _pick(m, (512, 256, 128))
    bn = _pick(n, (1024, 512, 256, 128))
    tile = pl.BlockSpec((bm, bn), lambda j, i: (i, j))
    emit_norm = next_gain is not None
    in_specs = [pl.BlockSpec((bm, k), lambda j, i: (i, 0)),
                pl.BlockSpec((None,) * len(lead) + (k, bn), lambda j, i: lead + (0, j), pipeline_mode=pl.Buffered(1)),
                tile]
    out_specs, out_shape, args = tile, jax.ShapeDtypeStruct((m, n), F32), (act, w, res)
    if emit_norm:
        in_specs.append(pl.BlockSpec((1, bn), lambda j, i: (0, j)))
        out_specs = [tile, tile, pl.BlockSpec((None, bm, LANES), lambda j, i: (j, i, 0))]
        out_shape = [out_shape, jax.ShapeDtypeStruct((m, n), BF16), jax.ShapeDtypeStruct((n // bn, m, LANES), F32)]
        args = args + (next_gain.reshape(1, n),)
    return pl.pallas_call(
        functools.partial(_ffn_down_kernel, scale=scale, emit_norm=emit_norm),
        grid=(n // bn, m // bm),
        in_specs=in_specs,
        out_specs=out_specs,
        out_shape=out_shape,
        compiler_params=_params("parallel", "parallel"),
        name="ffn_down",
    )(*args)


def _ffn(x, xg, ss, w_gate, w_up, w_down_bf16, lead, next_gain):
    act = _gateup(xg, ss, w_gate, w_up, lead)
    return _ffn_down(act, w_down_bf16, x, FFN_RES_WEIGHT, lead, next_gain)


def _nsa_prep_kernel(p_ref, cos_ref, sin_ref, qn_ref, kn_ref, q_ref, kc_ref, vc_ref, ks_ref, vs_ref, kw_ref, vw_ref,
                     *, heads, groups):
    cos = cos_ref[...]
    sin = sin_ref[...]
    scale = LANES ** -0.5
    base = heads * LANES
    kv_w = groups * LANES
    ts = p_ref.shape[0]
    pos = pl.program_id(1) * ts + lax.broadcasted_iota(jnp.int32, (ts, LANES), 0)
    lane = lax.broadcasted_iota(jnp.int32, (ts, LANES), 1)
    blk_onehot = jnp.where(lane == jnp.right_shift(pos, _SLC_SHIFT), 1.0, 0.0).astype(ks_ref.dtype)
    lane0_onehot = jnp.where(lane == 0, 1.0, 0.0).astype(kw_ref.dtype)

    def store_q(h):
        def store(y):
            q_ref[:, h * LANES:(h + 1) * LANES] = y.astype(q_ref.dtype)
        return store

    def store_k(k_out, g, upper):
        def store(y):
            if upper is None:
                k_out[g] = y.astype(k_out.dtype)
            else:
                k_out[g, :, :LANES] = y.astype(k_out.dtype)
                k_out[g, :, LANES:] = upper
        return store

    jobs = [(h * LANES, qn_ref[...], scale, store_q(h)) for h in range(heads)]
    for br, (k_out, upper) in enumerate(((kc_ref, None), (ks_ref, blk_onehot), (kw_ref, lane0_onehot))):
        for g in range(groups):
            jobs.append((base + (2 * br) * kv_w + g * LANES, kn_ref[br:br + 1, :], None, store_k(k_out, g, upper)))
    inv = [lax.rsqrt(jnp.mean(jnp.square(p_ref[:, c:c + LANES]), axis=-1, keepdims=True) + RMS_EPS)
           for c, _, _, _ in jobs]
    for (c, gain, post, store), r in zip(jobs, inv):
        y = p_ref[:, c:c + LANES] * r * gain
        y = y * cos + pltpu.roll(y, LANES // 2, 1) * sin
        store(y if post is None else y * post)

    for br, v_out in enumerate((vc_ref, vs_ref, vw_ref)):
        for g in range(groups):
            cv = base + (2 * br + 1) * kv_w + g * LANES
            v = p_ref[:, cv:cv + LANES].astype(v_out.dtype)
            if br == 0:
                v_out[g] = v
            else:
                v_out[g, :, :LANES] = v
                v_out[g, :, LANES:] = jnp.ones_like(v)


def _nsa_prep(proj, cos, sin_signed, q_norm, k_norm, *, heads, groups):
    b, s, n = proj.shape
    ts = _pick(s, (256, 128))
    kv_spec = pl.BlockSpec((None, groups, ts, LANES), lambda bi, i: (bi, 0, i, 0))
    aug_spec = pl.BlockSpec((None, groups, ts, 2 * LANES), lambda bi, i: (bi, 0, i, 0))
    tab_spec = pl.BlockSpec((ts, LANES), lambda bi, i: (i, 0))

    def kv_shape(dt, width=LANES):
        return jax.ShapeDtypeStruct((b, groups, s, width), dt)

    return pl.pallas_call(
        functools.partial(_nsa_prep_kernel, heads=heads, groups=groups),
        grid=(b, s // ts),
        in_specs=[
            pl.BlockSpec((None, ts, n), lambda bi, i: (bi, i, 0)),
            tab_spec,
            tab_spec,
            pl.BlockSpec((1, LANES), lambda bi, i: (0, 0)),
            pl.BlockSpec((3, LANES), lambda bi, i: (0, 0)),
        ],
        out_specs=[pl.BlockSpec((None, ts, heads * LANES), lambda bi, i: (bi, i, 0)),
                   kv_spec, kv_spec, aug_spec, aug_spec, aug_spec, aug_spec],
        out_shape=[jax.ShapeDtypeStruct((b, s, heads * LANES), BF16),
                   kv_shape(F32), kv_shape(F32), kv_shape(BF16, 2 * LANES), kv_shape(BF16, 2 * LANES),
                   kv_shape(BF16, 2 * LANES), kv_shape(BF16, 2 * LANES)],
        compiler_params=_params("parallel", "parallel"),
        name="nsa_prep",
    )(proj, cos, sin_signed, q_norm.reshape(1, LANES), k_norm)


def _gelu_tanh(x):
    return x * (0.5 * (1.0 + jnp.tanh(math.sqrt(2.0 / math.pi) * (x + 0.044715 * (x * x * x)))))


def _compress_kernel(k_ref, v_ref, pe_ref, w1_ref, w2_ref, ko_ref, vo_ref):
    for which, (x_ref, o_ref) in enumerate(((k_ref, ko_ref), (v_ref, vo_ref))):
        x = x_ref[...]
        nseg = x.shape[0]
        top = jnp.dot((x + pe_ref[which, 0]).astype(BF16), w1_ref[which, 0], preferred_element_type=F32)
        bot = jnp.dot((x + pe_ref[which, 1]).astype(BF16), w1_ref[which, 1], preferred_element_type=F32)
        pre = top + pltpu.roll(bot, nseg - 1, 0)
        hid = _gelu_tanh(pre).astype(BF16)
        o_ref[...] = jnp.dot(hid, w2_ref[which], preferred_element_type=F32).astype(o_ref.dtype)


def _nsa_compress(kcf, vcf, pe, w1, w2):
    b, g, nseg, width = kcf.shape
    x_spec = pl.BlockSpec((None, None, nseg, width), lambda bi, gi: (bi, gi, 0, 0))
    o_spec = pl.BlockSpec((None, None, nseg, LANES), lambda bi, gi: (bi, gi, 0, 0))
    o_shape = jax.ShapeDtypeStruct((b, g, nseg, LANES), BF16)
    return pl.pallas_call(
        _compress_kernel,
        grid=(b, g),
        in_specs=[
            x_spec, x_spec,
            pl.BlockSpec((2, 2, 1, width), lambda bi, gi: (0, 0, 0, 0)),
            pl.BlockSpec((2, 2, width, LANES), lambda bi, gi: (0, 0, 0, 0)),
            pl.BlockSpec((2, LANES, LANES), lambda bi, gi: (0, 0, 0)),
        ],
        out_specs=[o_spec, o_spec],
        out_shape=[o_shape, o_shape],
        compiler_params=_params("parallel", "parallel"),
        name="nsa_compress",
    )(kcf, vcf, pe, w1, w2)


def _stack_heads(q, hpg):
    return jnp.concatenate([q[:, h * LANES:(h + 1) * LANES] for h in range(hpg)], axis=0)


def _cmp_attn_kernel(q_ref, kc_ref, vc_ref, g_ref, ovl_ref, o_ref, sel_ref, *, tq, hpg, n_cmp, n_slc, top_n):
    i = pl.program_id(2)
    q8 = _stack_heads(q_ref[...], hpg)
    rows = hpg * tq
    ncp = kc_ref.shape[0]
    s = lax.dot_general(q8, kc_ref[...], NT_DIMS, preferred_element_type=F32)
    tpos = i * tq + lax.broadcasted_iota(jnp.int32, (tq, 1), 0)
    tpos8 = jnp.concatenate([tpos] * hpg, axis=0)
    ncol = lax.broadcasted_iota(jnp.int32, (1, ncp), 1)
    valid = jnp.where((ncol * CMP_STRIDE + (CMP_BLOCK - 1) <= tpos8) & (ncol < n_cmp), 1.0, 0.0)
    sm = jnp.where(valid > 0.5, s, NEG_INF)
    m = jnp.max(sm, axis=-1, keepdims=True)
    e = jnp.exp(sm - m) * valid
    l = jnp.sum(e, axis=-1, keepdims=True)
    p = e / jnp.where(l > 0.0, l, 1.0)
    o = jnp.dot(p.astype(BF16), vc_ref[...], preferred_element_type=F32)
    gates = g_ref[...]
    for h in range(hpg):
        o_ref[:, h * LANES:(h + 1) * LANES] = o[h * tq:(h + 1) * tq] * gates[:, h:h + 1]

    psum = p[0:tq]
    for h in range(1, hpg):
        psum = psum + p[h * tq:(h + 1) * tq]
    hi = psum.astype(BF16)
    lo = (psum - hi.astype(F32)).astype(BF16)
    ovl = ovl_ref[...]
    imp = (lax.dot_general(ovl, hi, NT_DIMS, preferred_element_type=F32)
           + lax.dot_general(ovl, lo, NT_DIMS, preferred_element_type=F32))
    jidx = lax.broadcasted_iota(jnp.int32, (n_slc, tq), 0)
    tq_pos = i * tq + lax.broadcasted_iota(jnp.int32, (n_slc, tq), 1)
    cur = jnp.right_shift(tq_pos, _SLC_SHIFT)
    forced = (jidx == 0) | (jidx == cur) | (jidx == cur - 1)
    causal = jidx * SLC_BLOCK <= tq_pos
    imp = jnp.where(forced, FORCED_SCORE, jnp.where(causal, imp, NEG_INF))
    sel_rows = []
    for j in range(n_slc):
        row = imp[j:j + 1, :]
        lower = jnp.where(jidx < j, 1.0, 0.0)
        beats = jnp.where(imp > row, 1.0, jnp.where(imp == row, lower, 0.0))
        rank = jnp.sum(beats, axis=0, keepdims=True)
        sel_rows.append(jnp.where(rank < top_n, 0.0, NEG_INF))
    sel_rows.append(jnp.zeros((LANES - n_slc, tq), F32))
    bias_t = jnp.concatenate(sel_rows, axis=0)
    sel_ref[...] = bias_t.T.astype(sel_ref.dtype)


def _nsa_cmp_attn(q, kcmp, vcmp, gates, ovl, *, groups, hpg, n_cmp, n_slc, top_n):
    b, s, d = q.shape
    ncp = kcmp.shape[2]
    tq = _pick(s, (512, 256, 128))
    qo_spec = pl.BlockSpec((None, tq, hpg * LANES), lambda bi, gi, i: (bi, i, gi))
    c_spec = pl.BlockSpec((None, None, ncp, LANES), lambda bi, gi, i: (bi, gi, 0, 0))
    return pl.pallas_call(
        functools.partial(_cmp_attn_kernel, tq=tq, hpg=hpg, n_cmp=n_cmp, n_slc=n_slc, top_n=top_n),
        grid=(b, groups, s // tq),
        in_specs=[
            qo_spec, c_spec, c_spec,
            pl.BlockSpec((None, tq, LANES), lambda bi, gi, i: (bi, i, gi)),
            pl.BlockSpec((n_slc, ncp), lambda bi, gi, i: (0, 0)),
        ],
        out_specs=[qo_spec, pl.BlockSpec((None, None, tq, LANES), lambda bi, gi, i: (bi, gi, i, 0))],
        out_shape=[jax.ShapeDtypeStruct((b, s, d), F32), jax.ShapeDtypeStruct((b, groups, s, LANES), BF16)],
        compiler_params=_params("parallel", "parallel", "parallel"),
        name="nsa_cmp_attn",
    )(q, kcmp, vcmp, gates, ovl)


def _stack_aug(q, upper, hpg):
    return jnp.concatenate(
        [jnp.concatenate([q[:, h * LANES:(h + 1) * LANES], upper[h]], axis=1) for h in range(hpg)], axis=0)


def _score_bounds(q, k_gain, hpg):
    k_bound = BOUND_SLACK * math.sqrt(LANES) * jnp.max(jnp.abs(k_gain), axis=-1, keepdims=True)
    out = []
    for h in range(hpg):
        qh = q[:, h * LANES:(h + 1) * LANES].astype(F32)
        out.append(jnp.sqrt(jnp.sum(qh * qh, axis=-1, keepdims=True)) * k_bound)
    return out


def _gated_store(o_ref, oin_ref, gates, o, first_gate, tq, hpg):
    for h in range(hpg):
        cols = slice(h * LANES, (h + 1) * LANES)
        gate = gates[:, first_gate + h:first_gate + h + 1]
        o_ref[:, cols] = (oin_ref[:, cols] + o[h * tq:(h + 1) * tq] * gate).astype(o_ref.dtype)


def _sel_attn_kernel(q_ref, k_ref, v_ref, sel_ref, kn_ref, g_ref, oin_ref, o_ref, *, tq, tk, hpg):
    i = pl.program_id(2)
    q = q_ref[...]
    bias = sel_ref[...].astype(F32)
    rows = hpg * tq
    tpos = i * tq + lax.broadcasted_iota(jnp.int32, (tq, 1), 0)
    n_kv = ((i + 1) * tq + tk - 1) // tk
    gates = g_ref[...]

    def causal_bias(start):
        kpos = start + lax.broadcasted_iota(jnp.int32, (1, tk), 1)
        return jnp.concatenate([jnp.where(kpos <= tpos, 0.0, NEG_INF)] * hpg, axis=0)

    q8 = _stack_aug(q, [(bias - m).astype(BF16) for m in _score_bounds(q, kn_ref[...], hpg)], hpg)

    def fast_tile(kv, acc, diagonal):
        start = pl.multiple_of(kv * tk, tk)
        s = lax.dot_general(q8, k_ref[pl.ds(start, tk), :], NT_DIMS, preferred_element_type=F32)
        if diagonal:
            s = s + causal_bias(start)
        return acc + jnp.dot(jnp.exp(s).astype(BF16), v_ref[pl.ds(start, tk), :], preferred_element_type=F32)

    acc = lax.fori_loop(0, n_kv - 1, lambda kv, a: fast_tile(kv, a, False), jnp.zeros((rows, 2 * LANES), F32))
    acc = fast_tile(n_kv - 1, acc, True)
    denom = acc[:, LANES:]
    _gated_store(o_ref, oin_ref, gates, acc[:, :LANES] / denom, hpg, tq, hpg)

    @pl.when(jnp.logical_not(jnp.min(denom) >= SOFTMAX_FLOOR))
    def _():
        q8x = _stack_aug(q, [sel_ref[...]] * hpg, hpg)

        def exact_tile(kv, carry, diagonal):
            m, acc = carry
            start = pl.multiple_of(kv * tk, tk)
            s = lax.dot_general(q8x, k_ref[pl.ds(start, tk), :], NT_DIMS, preferred_element_type=F32)
            if diagonal:
                s = s + causal_bias(start)
            m_new = jnp.maximum(m, jnp.max(s, axis=-1, keepdims=True))
            p = jnp.exp(s - m_new).astype(BF16)
            acc = jnp.exp(m - m_new) * acc + jnp.dot(p, v_ref[pl.ds(start, tk), :], preferred_element_type=F32)
            return m_new, acc

        init = (jnp.full((rows, 1), NEG_INF, F32), jnp.zeros((rows, 2 * LANES), F32))
        carry = lax.fori_loop(0, n_kv - 1, lambda kv, c: exact_tile(kv, c, False), init)
        _, acc_x = exact_tile(n_kv - 1, carry, True)
        _gated_store(o_ref, oin_ref, gates, acc_x[:, :LANES] / acc_x[:, LANES:], hpg, tq, hpg)


def _nsa_sel_attn(q, ks, vs, selb, k_gain, gates, o_in, *, groups, hpg):
    b, s, d = q.shape
    tq = _pick(s, (512, 256, 128))
    tk = _pick(s, (512, 256, 128))
    qo_spec = pl.BlockSpec((None, tq, hpg * LANES), lambda bi, gi, i: (bi, i, gi))
    kv_spec = pl.BlockSpec((None, None, s, 2 * LANES), lambda bi, gi, i: (bi, gi, 0, 0))
    return pl.pallas_call(
        functools.partial(_sel_attn_kernel, tq=tq, tk=tk, hpg=hpg),
        grid=(b, groups, s // tq),
        in_specs=[
            qo_spec, kv_spec, kv_spec,
            pl.BlockSpec((None, None, tq, LANES), lambda bi, gi, i: (bi, gi, i, 0)),
            pl.BlockSpec((1, LANES), lambda bi, gi, i: (0, 0)),
            pl.BlockSpec((None, tq, LANES), lambda bi, gi, i: (bi, i, gi)),
            qo_spec,
        ],
        out_specs=qo_spec,
        out_shape=jax.ShapeDtypeStruct((b, s, d), F32),
        compiler_params=_params("parallel", "parallel", "parallel"),
        name="nsa_sel_attn",
    )(q, ks, vs, selb, k_gain.reshape(1, LANES), gates, o_in)


def _win_attn_kernel(q_ref, k_ref, v_ref, kn_ref, g_ref, oin_ref, o_ref, *, tq, span, hpg):
    i = pl.program_id(2)
    q = q_ref[...]
    start = pl.multiple_of(jnp.maximum(i * tq - WINDOW, 0), tq)
    k = k_ref[pl.ds(start, span), :]
    v = v_ref[pl.ds(start, span), :]
    tpos = i * tq + lax.broadcasted_iota(jnp.int32, (tq, 1), 0)
    kpos = start + lax.broadcasted_iota(jnp.int32, (1, span), 1)
    diff = tpos - kpos
    bias = jnp.concatenate([jnp.where((diff >= 0) & (diff < WINDOW), 0.0, NEG_INF)] * hpg, axis=0)
    gates = g_ref[...]

    q8 = _stack_aug(q, [jnp.broadcast_to(-m, (tq, LANES)).astype(BF16) for m in _score_bounds(q, kn_ref[...], hpg)],
                    hpg)
    s = lax.dot_general(q8, k, NT_DIMS, preferred_element_type=F32) + bias
    acc = jnp.dot(jnp.exp(s).astype(BF16), v, preferred_element_type=F32)
    denom = acc[:, LANES:]
    _gated_store(o_ref, oin_ref, gates, acc[:, :LANES] / denom, 2 * hpg, tq, hpg)

    @pl.when(jnp.logical_not(jnp.min(denom) >= SOFTMAX_FLOOR))
    def _():
        q8x = _stack_aug(q, [jnp.zeros((tq, LANES), BF16)] * hpg, hpg)
        sx = lax.dot_general(q8x, k, NT_DIMS, preferred_element_type=F32) + bias
        p = jnp.exp(sx - jnp.max(sx, axis=-1, keepdims=True))
        acc_x = jnp.dot(p.astype(BF16), v, preferred_element_type=F32)
        _gated_store(o_ref, oin_ref, gates, acc_x[:, :LANES] / acc_x[:, LANES:], 2 * hpg, tq, hpg)


def _nsa_win_attn(q, kw, vw, k_gain, gates, o_in, *, groups, hpg):
    b, s, d = q.shape
    tq = _pick(s, (256, 128))
    span = WINDOW + tq
    assert s >= span and WINDOW % tq == 0
    qo_spec = pl.BlockSpec((None, tq, hpg * LANES), lambda bi, gi, i: (bi, i, gi))
    kv_spec = pl.BlockSpec((None, None, s, 2 * LANES), lambda bi, gi, i: (bi, gi, 0, 0))
    return pl.pallas_call(
        functools.partial(_win_attn_kernel, tq=tq, span=span, hpg=hpg),
        grid=(b, groups, s // tq),
        in_specs=[qo_spec, kv_spec, kv_spec, pl.BlockSpec((1, LANES), lambda bi, gi, i: (0, 0)),
                  pl.BlockSpec((None, tq, LANES), lambda bi, gi, i: (bi, i, gi)), qo_spec],
        out_specs=qo_spec,
        out_shape=jax.ShapeDtypeStruct((b, s, d), BF16),
        compiler_params=_params("parallel", "parallel", "parallel"),
        name="nsa_win_attn",
    )(q, kw, vw, k_gain.reshape(1, LANES), gates, o_in)


def _rope_tables(seq, dim):
    inv = ROPE_THETA ** (-jnp.arange(0, dim, 2, dtype=F32) / dim)
    ang = jnp.arange(seq, dtype=F32)[:, None] * inv[None, :]
    ang = jnp.concatenate([ang, ang], axis=-1)
    sign = jnp.where(jnp.arange(dim) < dim // 2, -1.0, 1.0).astype(F32)
    return jnp.cos(ang), jnp.sin(ang) * sign


def _nsa_mixer(x2, xg, ss, b, s, w_in, q_norm, k_norm, cmp_pos, cmp_w1, cmp_w2, w_o, next_gain):
    heads, groups = NSA_HEADS, NSA_GROUPS
    hpg = heads // groups
    d = x2.shape[1]
    assert d == heads * LANES and CMP_BLOCK == 2 * CMP_STRIDE and 3 * hpg <= LANES
    qd, kvd = heads * LANES, groups * LANES
    n_main = qd + 6 * kvd
    nseg = s // CMP_STRIDE
    n_cmp = (s - CMP_BLOCK) // CMP_STRIDE + 1
    n_slc = s // SLC_BLOCK
    top_n = min(SLC_TOPN, n_slc)
    assert n_cmp == nseg - 1 and n_slc % 8 == 0 and n_slc <= LANES and top_n >= 3

    proj = _mm(xg, ss, w_in, n=n_main, out_dtype=F32, name="nsa_in_proj")
    gates = _mm(xg, ss, w_in, col0=n_main, n=LANES, out_dtype=F32, sigmoid=True, name="nsa_gate_proj")
    gates = gates[:, :3 * heads].reshape(b * s, 3, groups, hpg).transpose(0, 2, 1, 3).reshape(b * s, groups, 3 * hpg)
    gates = jnp.pad(gates, ((0, 0), (0, 0), (0, LANES - 3 * hpg))).reshape(b, s, groups * LANES)

    cos, sin_signed = _rope_tables(s, LANES)
    q, kc, vc, ks, vs, kw, vw = _nsa_prep(proj.reshape(b, s, n_main), cos, sin_signed, q_norm, k_norm,
                                          heads=heads, groups=groups)

    seg_w = CMP_STRIDE * LANES
    pe = cmp_pos.reshape(2, 2, 1, seg_w)
    w1 = cmp_w1.reshape(2, 2, seg_w, LANES).astype(BF16)
    kcmp, vcmp = _nsa_compress(kc.reshape(b, groups, nseg, seg_w), vc.reshape(b, groups, nseg, seg_w),
                               pe, w1, cmp_w2.astype(BF16))

    cmp_start = np.arange(nseg) * CMP_STRIDE
    slc_start = np.arange(n_slc) * SLC_BLOCK
    ovl = ((cmp_start[None, :] < slc_start[:, None] + SLC_BLOCK) & (cmp_start[None, :] + CMP_BLOCK > slc_start[:, None])
           & (np.arange(nseg)[None, :] < n_cmp))
    ovl = jnp.asarray(ovl.astype(np.float32), dtype=BF16)

    o1, selm = _nsa_cmp_attn(q, kcmp, vcmp, gates, ovl, groups=groups, hpg=hpg, n_cmp=n_cmp, n_slc=n_slc, top_n=top_n)
    o2 = _nsa_sel_attn(q, ks, vs, selm, k_norm[1], gates, o1, groups=groups, hpg=hpg)
    o3 = _nsa_win_attn(q, kw, vw, k_norm[2], gates, o2, groups=groups, hpg=hpg)
    return _mm_res(o3.reshape(b * s, d), w_o.astype(BF16), x2, 1.0, next_gain=next_gain,
                   bm_prefs=(1024, 512, 256, 128), bn_prefs=(512, 256, 128), name="nsa_out_proj")


def _chunk_cumsum(x, row):
    n = x.shape[0]
    step = 1
    while step < n:
        x = x + jnp.where(row >= step, pltpu.roll(x, step, 0), 0.0)
        step *= 2
    return x


def _hgrn_kernel(q_ref, f_ref, i_ref, g_ref, lbl_ref, on_ref, o_ref, *, seq, chunk, layer, hps, unroll):
    lbl = lbl_ref[...]
    e = jnp.exp(lbl - jnp.max(lbl, axis=0, keepdims=True))
    p = e / jnp.sum(e, axis=0, keepdims=True)
    csum = p[0:1]
    for d in range(1, layer + 1):
        csum = csum + p[d:d + 1]
    lb = csum - p[0:1]
    o_gain = on_ref[...]
    causal = (lax.broadcasted_iota(jnp.int32, (chunk, chunk), 0) >= lax.broadcasted_iota(jnp.int32, (chunk, chunk), 1))
    row = lax.broadcasted_iota(jnp.int32, (chunk, hps * LANES), 0)
    heads = [slice(k * LANES, (k + 1) * LANES) for k in range(hps)]

    def body(it, states):
        states = list(states)
        chunks = []
        for u in range(unroll):
            rows = pl.ds(pl.multiple_of((it * unroll + u) * chunk, chunk), chunk)
            f = lb + (1.0 - lb) * jax.nn.sigmoid(f_ref[rows, :])
            kc = 1.0 - f
            g_cum = _chunk_cumsum(jnp.log(f), row)
            g_last = g_cum[chunk - 1:chunk, :]
            q_dec = (q_ref[rows, :] * jnp.exp(g_cum)).astype(BF16)
            k_inv = (kc * jnp.exp(-g_cum)).astype(BF16)
            k_tail = (kc * jnp.exp(g_last - g_cum)).astype(BF16)
            chunks.append((rows, q_dec, k_inv, k_tail, i_ref[rows, :].astype(BF16), jnp.exp(g_last)))
        a_all = [[lax.dot_general(q_dec[:, h], k_inv[:, h], NT_DIMS, preferred_element_type=F32) for h in heads]
                 for (_, q_dec, k_inv, _, _, _) in chunks]
        ds_all = [[lax.dot_general(v[:, h], k_tail[:, h], TN_DIMS, preferred_element_type=F32) for h in heads]
                  for (_, _, _, k_tail, v, _) in chunks]
        o_all = [[jnp.dot(jnp.where(causal, a, 0.0).astype(BF16), v[:, h], preferred_element_type=F32)
                  for a, h in zip(a_row, heads)]
                 for a_row, (_, _, _, _, v, _) in zip(a_all, chunks)]
        for u, (rows, q_dec, _, _, _, decay) in enumerate(chunks):
            for k, h in enumerate(heads):
                o = o_all[u][k] + lax.dot_general(q_dec[:, h], states[k].astype(BF16), NT_DIMS,
                                                  preferred_element_type=F32)
                states[k] = states[k] * decay[:, h] + ds_all[u][k]
                y = o * lax.rsqrt(jnp.mean(o * o, axis=-1, keepdims=True) + RMS_EPS) * o_gain
                gz = g_ref[rows, h]
                o_ref[rows, h] = (y * (gz * jax.nn.sigmoid(gz))).astype(o_ref.dtype)
        return tuple(states)

    init = tuple(jnp.zeros((LANES, LANES), F32) for _ in range(hps))
    lax.fori_loop(0, seq // (chunk * unroll), body, init)


def _hgrn_core(proj, lb_logits, o_norm, *, heads, layer, hps=4, unroll=4):
    b, s, _ = proj.shape
    depth = lb_logits.shape[0]
    hps = math.gcd(hps, heads)
    width = hps * LANES
    groups = heads // hps
    assert s % (HGRN_CHUNK * unroll) == 0

    def col_spec(part):
        return pl.BlockSpec((None, s, width), lambda bi, hi: (bi, 0, part * groups + hi))

    return pl.pallas_call(
        functools.partial(_hgrn_kernel, seq=s, chunk=HGRN_CHUNK, layer=layer, hps=hps, unroll=unroll),
        grid=(b, groups),
        in_specs=[col_spec(0), col_spec(1), col_spec(2), col_spec(3),
                  pl.BlockSpec((depth, width), lambda bi, hi: (0, hi)),
                  pl.BlockSpec((1, LANES), lambda bi, hi: (0, 0))],
        out_specs=pl.BlockSpec((None, s, width), lambda bi, hi: (bi, 0, hi)),
        out_shape=jax.ShapeDtypeStruct((b, s, heads * LANES), BF16),
        compiler_params=_params("parallel", "parallel"),
        name="hgrn_core",
    )(proj, proj, proj, proj, lb_logits, o_norm.reshape(1, LANES))


def _hgrn_mixer(x2, xg, ss, b, s, w_in, lb_logits, o_norm, w_o, layer, next_gain):
    heads = HGRN_HEADS
    d = x2.shape[1]
    assert d == heads * LANES and w_in.shape[1] == 4 * d and s % HGRN_CHUNK == 0
    proj = _mm(xg, ss, w_in, out_dtype=F32, name="hgrn_in_proj")
    o = _hgrn_core(proj.reshape(b, s, 4 * d), lb_logits, o_norm, heads=heads, layer=layer)
    return _mm_res(o.reshape(b * s, d), w_o.astype(BF16), x2, 1.0, next_gain=next_gain,
                   bm_prefs=(1024, 512, 256, 128), bn_prefs=(512, 256, 128), name="hgrn_out_proj")


def kernel(x, ffn_norm, ffn_w_gate, ffn_w_up, ffn_w_down, mix_norm, nsa_w_in, nsa_q_norm, nsa_k_norm, nsa_cmp_pos,
           nsa_cmp_w1, nsa_cmp_w2, nsa_w_o, hgrn_w_in, hgrn_lb_logits, hgrn_o_norm, hgrn_w_o):
    b, s, d = x.shape
    depth = ffn_norm.shape[0]
    x2 = x.reshape(b * s, d)
    w_down = ffn_w_down.astype(BF16)
    xg, ss = _prenorm(x2, ffn_norm[0, 0])
    for layer in range(depth):
        slot = layer // N_MIXERS
        x2, xg, ss = _ffn(x2, xg, ss, ffn_w_gate, ffn_w_up, w_down, (layer, 0), mix_norm[layer])
        if layer % N_MIXERS == 0:
            x2, xg, ss = _nsa_mixer(x2, xg, ss, b, s, nsa_w_in[slot], nsa_q_norm[slot], nsa_k_norm[slot],
                                    nsa_cmp_pos[slot], nsa_cmp_w1[slot], nsa_cmp_w2[slot], nsa_w_o[slot],
                                    ffn_norm[layer, 1])
        else:
            x2, xg, ss = _hgrn_mixer(x2, xg, ss, b, s, hgrn_w_in[slot], hgrn_lb_logits, hgrn_o_norm[slot],
                                     hgrn_w_o[slot], layer, ffn_norm[layer, 1])
        if layer + 1 < depth:
            x2, xg, ss = _ffn(x2, xg, ss, ffn_w_gate, ffn_w_up, w_down, (layer, 1), ffn_norm[layer + 1, 0])
        else:
            x2 = _ffn(x2, xg, ss, ffn_w_gate, ffn_w_up, w_down, (layer, 1), None)
    return x2.reshape(b, s, d)
```

```python
import functools
import math

import jax
import jax.numpy as jnp
import numpy as np
from jax import lax
from jax.experimental import pallas as pl
from jax.experimental.pallas import tpu as pltpu

F32 = jnp.float32
BF16 = jnp.bfloat16

RMS_EPS = 1e-6
NEG_INF = -1e30
FORCED_SCORE = 1e30
FFN_RES_WEIGHT = 0.5
N_MIXERS = 2

NSA_HEADS = 32
NSA_GROUPS = 4
CMP_BLOCK = 32
CMP_STRIDE = 16
SLC_BLOCK = 64
SLC_TOPN = 16
_SLC_SHIFT = SLC_BLOCK.bit_length() - 1
assert 1 << _SLC_SHIFT == SLC_BLOCK
WINDOW = 512
ROPE_THETA = 10000.0

HGRN_HEADS = 32
HGRN_CHUNK = 64

LANES = 128
SOFTMAX_FLOOR = 1e-30
BOUND_SLACK = 1.02
VMEM_LIMIT_BYTES = 56 * 1024 * 1024

NT_DIMS = (((1,), (1,)), ((), ()))
TN_DIMS = (((0,), (0,)), ((), ()))


def _params(*sem):
    return pltpu.CompilerParams(dimension_semantics=sem, vmem_limit_bytes=VMEM_LIMIT_BYTES)


def _pick(n, prefs):
    for p in prefs:
        if n % p == 0:
            return p
    raise ValueError(f"no tile in {prefs} divides {n}")


def _fold_lanes(x):
    out = x[:, :LANES]
    for t in range(1, x.shape[1] // LANES):
        out = out + x[:, t * LANES:(t + 1) * LANES]
    return out


def _inv_rms(ss, width):
    total = jnp.sum(jnp.sum(ss, axis=0), axis=-1, keepdims=True)
    return lax.rsqrt(total * (1.0 / width) + RMS_EPS)


def _prenorm_kernel(x_ref, g_ref, xg_ref, ss_ref):
    x = x_ref[...]
    xg_ref[...] = (x * g_ref[...]).astype(xg_ref.dtype)
    ss_ref[0] = _fold_lanes(x * x)


def _prenorm(x, g):
    m, d = x.shape
    tr = _pick(m, (256, 128, 8))
    return pl.pallas_call(
        _prenorm_kernel,
        grid=(m // tr,),
        in_specs=[pl.BlockSpec((tr, d), lambda i: (i, 0)), pl.BlockSpec((1, d), lambda i: (0, 0))],
        out_specs=[pl.BlockSpec((tr, d), lambda i: (i, 0)), pl.BlockSpec((1, tr, LANES), lambda i: (0, i, 0))],
        out_shape=[jax.ShapeDtypeStruct((m, d), BF16), jax.ShapeDtypeStruct((1, m, LANES), F32)],
        compiler_params=_params("parallel"),
        name="prenorm",
    )(x, g.reshape(1, d))


def _weight_spec(w, lead, bn, first_tile=0):
    k = w.shape[-2]
    return pl.BlockSpec((None,) * len(lead) + (k, bn), lambda i, j: lead + (0, first_tile + j))


def _lhs_spec(bm, k, single_buffer):
    if single_buffer:
        return pl.BlockSpec((bm, k), lambda i, j: (i, 0), pipeline_mode=pl.Buffered(1))
    return pl.BlockSpec((bm, k), lambda i, j: (i, 0))


def _stat_spec(planes, bm):
    return pl.BlockSpec((planes, bm, LANES), lambda i, j: (0, i, 0))


def _gateup_kernel(xg_ref, ss_ref, wg_ref, wu_ref, *rest, width, cast_down):
    if cast_down:
        wd_ref, o_ref, wd_out_ref, r_ref = rest
        wd_out_ref[...] = wd_ref[...].astype(wd_out_ref.dtype)
    else:
        o_ref, r_ref = rest

    @pl.when(pl.program_id(1) == 0)
    def _():
        r_ref[...] = jnp.broadcast_to(_inv_rms(ss_ref[...], width), r_ref.shape)

    xg = xg_ref[...]
    r = jnp.concatenate([r_ref[...]] * (o_ref.shape[1] // LANES), axis=1)
    a = jnp.dot(xg, wg_ref[...].astype(BF16), preferred_element_type=F32) * r
    b = jnp.dot(xg, wu_ref[...].astype(BF16), preferred_element_type=F32) * r
    o_ref[...] = (a * jax.nn.sigmoid(a) * b).astype(o_ref.dtype)


def _gateup(xg, ss, wg, wu, wd, lead):
    m, k = xg.shape
    n = wg.shape[-1]
    bm = _pick(m, (2048, 1024, 512, 256, 128))
    bn = _pick(n, (256, 128))
    steps_j = n // bn
    steps = (m // bm) * steps_j
    slab = wd.shape[-2] // steps
    cast_down = slab * steps == wd.shape[-2] and slab % 16 == 0
    in_specs = [_lhs_spec(bm, k, True), _stat_spec(ss.shape[0], bm), _weight_spec(wg, lead, bn),
                _weight_spec(wu, lead, bn)]
    out_specs = pl.BlockSpec((bm, bn), lambda i, j: (i, j))
    out_shape = jax.ShapeDtypeStruct((m, n), BF16)
    args = (xg, ss, wg, wu)
    if cast_down:
        d_out = wd.shape[-1]
        in_specs.append(pl.BlockSpec((None,) * len(lead) + (slab, d_out), lambda i, j: lead + (i * steps_j + j, 0)))
        out_specs = [out_specs, pl.BlockSpec((slab, d_out), lambda i, j: (i * steps_j + j, 0))]
        out_shape = [out_shape, jax.ShapeDtypeStruct(wd.shape[-2:], BF16)]
        args = args + (wd,)
    out = pl.pallas_call(
        functools.partial(_gateup_kernel, width=k, cast_down=cast_down),
        grid=(m // bm, steps_j),
        in_specs=in_specs,
        out_specs=out_specs,
        out_shape=out_shape,
        scratch_shapes=[pltpu.VMEM((bm, LANES), F32)],
        compiler_params=_params("parallel", "arbitrary"),
        name="ffn_gateup",
    )(*args)
    return out if cast_down else (out, wd[lead].astype(BF16))


def _mm_res_kernel(a_ref, w_ref, r_ref, *rest, scale, emit_norm):
    acc = jnp.dot(a_ref[...], w_ref[...].astype(BF16), preferred_element_type=F32)
    x = r_ref[...] + scale * acc
    if not emit_norm:
        (o_ref,) = rest
        o_ref[...] = x
        return
    gn_ref, o_ref, xg_ref, ss_ref = rest
    o_ref[...] = x
    xg_ref[...] = (x * gn_ref[...]).astype(xg_ref.dtype)

    @pl.when(pl.program_id(1) == 0)
    def _():
        ss_ref[...] = jnp.zeros_like(ss_ref)

    ss_ref[0] += _fold_lanes(x * x)


def _mm_res(a, w, res, scale, *, lead=(), next_gain=None, bm_prefs, bn_prefs, name):
    m, k = a.shape
    n = w.shape[-1]
    bm = _pick(m, bm_prefs)
    bn = _pick(n, bn_prefs)
    tile = pl.BlockSpec((bm, bn), lambda i, j: (i, j))
    emit_norm = next_gain is not None
    in_specs = [_lhs_spec(bm, k, False), _weight_spec(w, lead, bn), tile]
    out_specs, out_shape, args = tile, jax.ShapeDtypeStruct((m, n), F32), (a, w, res)
    if emit_norm:
        in_specs.append(pl.BlockSpec((1, bn), lambda i, j: (0, j)))
        out_specs = [tile, tile, _stat_spec(1, bm)]
        out_shape = [out_shape, jax.ShapeDtypeStruct((m, n), BF16), jax.ShapeDtypeStruct((1, m, LANES), F32)]
        args = args + (next_gain.reshape(1, n),)
    return pl.pallas_call(
        functools.partial(_mm_res_kernel, scale=scale, emit_norm=emit_norm),
        grid=(m // bm, n // bn),
        in_specs=in_specs,
        out_specs=out_specs,
        out_shape=out_shape,
        compiler_params=_params("parallel", "arbitrary"),
        name=name,
    )(*args)


def _mm_kernel(xg_ref, ss_ref, w_ref, o_ref, *, width, sigmoid):
    acc = jnp.dot(xg_ref[...], w_ref[...].astype(BF16), preferred_element_type=F32) * _inv_rms(ss_ref[...], width)
    if sigmoid:
        acc = jax.nn.sigmoid(acc)
    o_ref[...] = acc.astype(o_ref.dtype)


def _mm(xg, ss, w, *, col0=0, n=None, out_dtype, sigmoid=False, name):
    m, k = xg.shape
    n = w.shape[-1] if n is None else n
    bn = _pick(n, (512, 256, 128))
    assert col0 % bn == 0
    tall = n // bn >= 8
    bm = _pick(m, (2048, 1024, 512, 256, 128) if tall else (1024, 512, 256, 128))
    return pl.pallas_call(
        functools.partial(_mm_kernel, width=k, sigmoid=sigmoid),
        grid=(m // bm, n // bn),
        in_specs=[_lhs_spec(bm, k, tall), _stat_spec(ss.shape[0], bm), _weight_spec(w, (), bn, col0 // bn)],
        out_specs=pl.BlockSpec((bm, bn), lambda i, j: (i, j)),
        out_shape=jax.ShapeDtypeStruct((m, n), out_dtype),
        compiler_params=_params("parallel", "arbitrary"),
        name=name,
    )(xg, ss, w)


def _ffn_down_kernel(a_ref, w_ref, r_ref, *rest, scale, emit_norm):
    x = r_ref[...] + scale * jnp.dot(a_ref[...], w_ref[...], preferred_element_type=F32)
    if not emit_norm:
        (o_ref,) = rest
        o_ref[...] = x
        return
    gn_ref, o_ref, xg_ref, ss_ref = rest
    o_ref[...] = x
    xg_ref[...] = (x * gn_ref[...]).astype(xg_ref.dtype)
    ss_ref[...] = _fold_lanes(x * x)


def _ffn_down(act, w, res, scale, next_gain):
    m, k = act.shape
    n = w.shape[-1]
    bm = _pick(m, (512, 256, 128))
    bn = _pick(n, (1024, 512, 256, 128))
    tile = pl.BlockSpec((bm, bn), lambda j, i: (i, j))
    emit_norm = next_gain is not None
    in_specs = [pl.BlockSpec((bm, k), lambda j, i: (i, 0)),
                pl.BlockSpec((k, bn), lambda j, i: (0, j), pipeline_mode=pl.Buffered(1)),
                tile]
    out_specs, out_shape, args = tile, jax.ShapeDtypeStruct((m, n), F32), (act, w, res)
    if emit_norm:
        in_specs.append(pl.BlockSpec((1, bn), lambda j, i: (0, j)))
        out_specs = [tile, tile, pl.BlockSpec((None, bm, LANES), lambda j, i: (j, i, 0))]
        out_shape = [out_shape, jax.ShapeDtypeStruct((m, n), BF16), jax.ShapeDtypeStruct((n // bn, m, LANES), F32)]
        args = args + (next_gain.reshape(1, n),)
    return pl.pallas_call(
        functools.partial(_ffn_down_kernel, scale=scale, emit_norm=emit_norm),
        grid=(n // bn, m // bm),
        in_specs=in_specs,
        out_specs=out_specs,
        out_shape=out_shape,
        compiler_params=_params("parallel", "parallel"),
        name="ffn_down",
    )(*args)


def _ffn(x, xg, ss, w_gate, w_up, w_down, lead, next_gain):
    act, w_down_bf16 = _gateup(xg, ss, w_gate, w_up, w_down, lead)
    return _ffn_down(act, w_down_bf16, x, FFN_RES_WEIGHT, next_gain)


def _nsa_prep_kernel(p_ref, cos_ref, sin_ref, qn_ref, kn_ref, q_ref, kc_ref, vc_ref, ks_ref, vs_ref, kw_ref, vw_ref,
                     *, heads, groups):
    cos = cos_ref[...]
    sin = sin_ref[...]
    scale = LANES ** -0.5
    base = heads * LANES
    kv_w = groups * LANES
    ts = p_ref.shape[0]
    pos = pl.program_id(1) * ts + lax.broadcasted_iota(jnp.int32, (ts, LANES), 0)
    lane = lax.broadcasted_iota(jnp.int32, (ts, LANES), 1)
    blk_onehot = jnp.where(lane == jnp.right_shift(pos, _SLC_SHIFT), 1.0, 0.0).astype(ks_ref.dtype)
    lane0_onehot = jnp.where(lane == 0, 1.0, 0.0).astype(kw_ref.dtype)

    def store_q(h):
        def store(y):
            q_ref[:, h * LANES:(h + 1) * LANES] = y.astype(q_ref.dtype)
        return store

    def store_k(k_out, g, upper):
        def store(y):
            if upper is None:
                k_out[g] = y.astype(k_out.dtype)
            else:
                k_out[g, :, :LANES] = y.astype(k_out.dtype)
                k_out[g, :, LANES:] = upper
        return store

    jobs = [(h * LANES, qn_ref[...], scale, store_q(h)) for h in range(heads)]
    for br, (k_out, upper) in enumerate(((kc_ref, None), (ks_ref, blk_onehot), (kw_ref, lane0_onehot))):
        for g in range(groups):
            jobs.append((base + (2 * br) * kv_w + g * LANES, kn_ref[br:br + 1, :], None, store_k(k_out, g, upper)))
    inv = [lax.rsqrt(jnp.mean(jnp.square(p_ref[:, c:c + LANES]), axis=-1, keepdims=True) + RMS_EPS)
           for c, _, _, _ in jobs]
    for (c, gain, post, store), r in zip(jobs, inv):
        y = p_ref[:, c:c + LANES] * r * gain
        y = y * cos + pltpu.roll(y, LANES // 2, 1) * sin
        store(y if post is None else y * post)

    for br, v_out in enumerate((vc_ref, vs_ref, vw_ref)):
        for g in range(groups):
            cv = base + (2 * br + 1) * kv_w + g * LANES
            v = p_ref[:, cv:cv + LANES].astype(v_out.dtype)
            if br == 0:
                v_out[g] = v
            else:
                v_out[g, :, :LANES] = v
                v_out[g, :, LANES:] = jnp.ones_like(v)


def _nsa_prep(proj, cos, sin_signed, q_norm, k_norm, *, heads, groups):
    b, s, n = proj.shape
    ts = _pick(s, (256, 128))
    kv_spec = pl.BlockSpec((None, groups, ts, LANES), lambda bi, i: (bi, 0, i, 0))
    aug_spec = pl.BlockSpec((None, groups, ts, 2 * LANES), lambda bi, i: (bi, 0, i, 0))
    tab_spec = pl.BlockSpec((ts, LANES), lambda bi, i: (i, 0))

    def kv_shape(dt, width=LANES):
        return jax.ShapeDtypeStruct((b, groups, s, width), dt)

    return pl.pallas_call(
        functools.partial(_nsa_prep_kernel, heads=heads, groups=groups),
        grid=(b, s // ts),
        in_specs=[
            pl.BlockSpec((None, ts, n), lambda bi, i: (bi, i, 0)),
            tab_spec,
            tab_spec,
            pl.BlockSpec((1, LANES), lambda bi, i: (0, 0)),
            pl.BlockSpec((3, LANES), lambda bi, i: (0, 0)),
        ],
        out_specs=[pl.BlockSpec((None, ts, heads * LANES), lambda bi, i: (bi, i, 0)),
                   kv_spec, kv_spec, aug_spec, aug_spec, aug_spec, aug_spec],
        out_shape=[jax.ShapeDtypeStruct((b, s, heads * LANES), BF16),
                   kv_shape(F32), kv_shape(F32), kv_shape(BF16, 2 * LANES), kv_shape(BF16, 2 * LANES),
                   kv_shape(BF16, 2 * LANES), kv_shape(BF16, 2 * LANES)],
        compiler_params=_params("parallel", "parallel"),
        name="nsa_prep",
    )(proj, cos, sin_signed, q_norm.reshape(1, LANES), k_norm)


def _gelu_tanh(x):
    return x * (0.5 * (1.0 + jnp.tanh(math.sqrt(2.0 / math.pi) * (x + 0.044715 * (x * x * x)))))


def _compress_kernel(k_ref, v_ref, pe_ref, w1_ref, w2_ref, ko_ref, vo_ref):
    for which, (x_ref, o_ref) in enumerate(((k_ref, ko_ref), (v_ref, vo_ref))):
        x = x_ref[...]
        nseg = x.shape[0]
        top = jnp.dot((x + pe_ref[which, 0]).astype(BF16), w1_ref[which, 0], preferred_element_type=F32)
        bot = jnp.dot((x + pe_ref[which, 1]).astype(BF16), w1_ref[which, 1], preferred_element_type=F32)
        pre = top + pltpu.roll(bot, nseg - 1, 0)
        hid = _gelu_tanh(pre).astype(BF16)
        o_ref[...] = jnp.dot(hid, w2_ref[which], preferred_element_type=F32).astype(o_ref.dtype)


def _nsa_compress(kcf, vcf, pe, w1, w2):
    b, g, nseg, width = kcf.shape
    x_spec = pl.BlockSpec((None, None, nseg, width), lambda bi, gi: (bi, gi, 0, 0))
    o_spec = pl.BlockSpec((None, None, nseg, LANES), lambda bi, gi: (bi, gi, 0, 0))
    o_shape = jax.ShapeDtypeStruct((b, g, nseg, LANES), BF16)
    return pl.pallas_call(
        _compress_kernel,
        grid=(b, g),
        in_specs=[
            x_spec, x_spec,
            pl.BlockSpec((2, 2, 1, width), lambda bi, gi: (0, 0, 0, 0)),
            pl.BlockSpec((2, 2, width, LANES), lambda bi, gi: (0, 0, 0, 0)),
            pl.BlockSpec((2, LANES, LANES), lambda bi, gi: (0, 0, 0)),
        ],
        out_specs=[o_spec, o_spec],
        out_shape=[o_shape, o_shape],
        compiler_params=_params("parallel", "parallel"),
        name="nsa_compress",
    )(kcf, vcf, pe, w1, w2)


def _stack_heads(q, hpg):
    return jnp.concatenate([q[:, h * LANES:(h + 1) * LANES] for h in range(hpg)], axis=0)


def _cmp_attn_kernel(q_ref, kc_ref, vc_ref, g_ref, ovl_ref, o_ref, sel_ref, *, tq, hpg, n_cmp, n_slc, top_n):
    i = pl.program_id(2)
    q8 = _stack_heads(q_ref[...], hpg)
    rows = hpg * tq
    ncp = kc_ref.shape[0]
    s = lax.dot_general(q8, kc_ref[...], NT_DIMS, preferred_element_type=F32)
    tpos = i * tq + lax.broadcasted_iota(jnp.int32, (tq, 1), 0)
    tpos8 = jnp.concatenate([tpos] * hpg, axis=0)
    ncol = lax.broadcasted_iota(jnp.int32, (1, ncp), 1)
    valid = jnp.where((ncol * CMP_STRIDE + (CMP_BLOCK - 1) <= tpos8) & (ncol < n_cmp), 1.0, 0.0)
    sm = jnp.where(valid > 0.5, s, NEG_INF)
    m = jnp.max(sm, axis=-1, keepdims=True)
    e = jnp.exp(sm - m) * valid
    l = jnp.sum(e, axis=-1, keepdims=True)
    p = e / jnp.where(l > 0.0, l, 1.0)
    o = jnp.dot(p.astype(BF16), vc_ref[...], preferred_element_type=F32)
    gates = g_ref[...]
    for h in range(hpg):
        o_ref[:, h * LANES:(h + 1) * LANES] = o[h * tq:(h + 1) * tq] * gates[:, h:h + 1]

    psum = p[0:tq]
    for h in range(1, hpg):
        psum = psum + p[h * tq:(h + 1) * tq]
    hi = psum.astype(BF16)
    lo = (psum - hi.astype(F32)).astype(BF16)
    ovl = ovl_ref[...]
    imp = (lax.dot_general(ovl, hi, NT_DIMS, preferred_element_type=F32)
           + lax.dot_general(ovl, lo, NT_DIMS, preferred_element_type=F32))
    jidx = lax.broadcasted_iota(jnp.int32, (n_slc, tq), 0)
    tq_pos = i * tq + lax.broadcasted_iota(jnp.int32, (n_slc, tq), 1)
    cur = jnp.right_shift(tq_pos, _SLC_SHIFT)
    forced = (jidx == 0) | (jidx == cur) | (jidx == cur - 1)
    causal = jidx * SLC_BLOCK <= tq_pos
    imp = jnp.where(forced, FORCED_SCORE, jnp.where(causal, imp, NEG_INF))
    sel_rows = []
    for j in range(n_slc):
        row = imp[j:j + 1, :]
        lower = jnp.where(jidx < j, 1.0, 0.0)
        beats = jnp.where(imp > row, 1.0, jnp.where(imp == row, lower, 0.0))
        rank = jnp.sum(beats, axis=0, keepdims=True)
        sel_rows.append(jnp.where(rank < top_n, 0.0, NEG_INF))
    sel_rows.append(jnp.zeros((LANES - n_slc, tq), F32))
    bias_t = jnp.concatenate(sel_rows, axis=0)
    sel_ref[...] = bias_t.T.astype(sel_ref.dtype)


def _nsa_cmp_attn(q, kcmp, vcmp, gates, ovl, *, groups, hpg, n_cmp, n_slc, top_n):
    b, s, d = q.shape
    ncp = kcmp.shape[2]
    tq = _pick(s, (512, 256, 128))
    qo_spec = pl.BlockSpec((None, tq, hpg * LANES), lambda bi, gi, i: (bi, i, gi))
    c_spec = pl.BlockSpec((None, None, ncp, LANES), lambda bi, gi, i: (bi, gi, 0, 0))
    return pl.pallas_call(
        functools.partial(_cmp_attn_kernel, tq=tq, hpg=hpg, n_cmp=n_cmp, n_slc=n_slc, top_n=top_n),
        grid=(b, groups, s // tq),
        in_specs=[
            qo_spec, c_spec, c_spec,
            pl.BlockSpec((None, tq, LANES), lambda bi, gi, i: (bi, i, gi)),
            pl.BlockSpec((n_slc, ncp), lambda bi, gi, i: (0, 0)),
        ],
        out_specs=[qo_spec, pl.BlockSpec((None, None, tq, LANES), lambda bi, gi, i: (bi, gi, i, 0))],
        out_shape=[jax.ShapeDtypeStruct((b, s, d), F32), jax.ShapeDtypeStruct((b, groups, s, LANES), BF16)],
        compiler_params=_params("parallel", "parallel", "parallel"),
        name="nsa_cmp_attn",
    )(q, kcmp, vcmp, gates, ovl)


def _stack_aug(q, upper, hpg):
    return jnp.concatenate(
        [jnp.concatenate([q[:, h * LANES:(h + 1) * LANES], upper[h]], axis=1) for h in range(hpg)], axis=0)


def _score_bounds(q, k_gain, hpg):
    k_bound = BOUND_SLACK * math.sqrt(LANES) * jnp.max(jnp.abs(k_gain), axis=-1, keepdims=True)
    out = []
    for h in range(hpg):
        qh = q[:, h * LANES:(h + 1) * LANES].astype(F32)
        out.append(jnp.sqrt(jnp.sum(qh * qh, axis=-1, keepdims=True)) * k_bound)
    return out


def _gated_store(o_ref, oin_ref, gates, o, first_gate, tq, hpg):
    for h in range(hpg):
        cols = slice(h * LANES, (h + 1) * LANES)
        gate = gates[:, first_gate + h:first_gate + h + 1]
        o_ref[:, cols] = (oin_ref[:, cols] + o[h * tq:(h + 1) * tq] * gate).astype(o_ref.dtype)


def _sel_attn_kernel(q_ref, k_ref, v_ref, sel_ref, kn_ref, g_ref, oin_ref, o_ref, *, tq, tk, hpg):
    i = pl.program_id(2)
    q = q_ref[...]
    bias = sel_ref[...].astype(F32)
    rows = hpg * tq
    tpos = i * tq + lax.broadcasted_iota(jnp.int32, (tq, 1), 0)
    n_kv = ((i + 1) * tq + tk - 1) // tk
    gates = g_ref[...]

    def causal_bias(start):
        kpos = start + lax.broadcasted_iota(jnp.int32, (1, tk), 1)
        return jnp.concatenate([jnp.where(kpos <= tpos, 0.0, NEG_INF)] * hpg, axis=0)

    q8 = _stack_aug(q, [(bias - m).astype(BF16) for m in _score_bounds(q, kn_ref[...], hpg)], hpg)

    def fast_tile(kv, acc, diagonal):
        start = pl.multiple_of(kv * tk, tk)
        s = lax.dot_general(q8, k_ref[pl.ds(start, tk), :], NT_DIMS, preferred_element_type=F32)
        if diagonal:
            s = s + causal_bias(start)
        return acc + jnp.dot(jnp.exp(s).astype(BF16), v_ref[pl.ds(start, tk), :], preferred_element_type=F32)

    acc = lax.fori_loop(0, n_kv - 1, lambda kv, a: fast_tile(kv, a, False), jnp.zeros((rows, 2 * LANES), F32))
    acc = fast_tile(n_kv - 1, acc, True)
    denom = acc[:, LANES:]
    _gated_store(o_ref, oin_ref, gates, acc[:, :LANES] / denom, hpg, tq, hpg)

    @pl.when(jnp.logical_not(jnp.min(denom) >= SOFTMAX_FLOOR))
    def _():
        q8x = _stack_aug(q, [sel_ref[...]] * hpg, hpg)

        def exact_tile(kv, carry, diagonal):
            m, acc = carry
            start = pl.multiple_of(kv * tk, tk)
            s = lax.dot_general(q8x, k_ref[pl.ds(start, tk), :], NT_DIMS, preferred_element_type=F32)
            if diagonal:
                s = s + causal_bias(start)
            m_new = jnp.maximum(m, jnp.max(s, axis=-1, keepdims=True))
            p = jnp.exp(s - m_new).astype(BF16)
            acc = jnp.exp(m - m_new) * acc + jnp.dot(p, v_ref[pl.ds(start, tk), :], preferred_element_type=F32)
            return m_new, acc

        init = (jnp.full((rows, 1), NEG_INF, F32), jnp.zeros((rows, 2 * LANES), F32))
        carry = lax.fori_loop(0, n_kv - 1, lambda kv, c: exact_tile(kv, c, False), init)
        _, acc_x = exact_tile(n_kv - 1, carry, True)
        _gated_store(o_ref, oin_ref, gates, acc_x[:, :LANES] / acc_x[:, LANES:], hpg, tq, hpg)


def _nsa_sel_attn(q, ks, vs, selb, k_gain, gates, o_in, *, groups, hpg):
    b, s, d = q.shape
    tq = _pick(s, (512, 256, 128))
    tk = _pick(s, (512, 256, 128))
    qo_spec = pl.BlockSpec((None, tq, hpg * LANES), lambda bi, gi, i: (bi, i, gi))
    kv_spec = pl.BlockSpec((None, None, s, 2 * LANES), lambda bi, gi, i: (bi, gi, 0, 0))
    return pl.pallas_call(
        functools.partial(_sel_attn_kernel, tq=tq, tk=tk, hpg=hpg),
        grid=(b, groups, s // tq),
        in_specs=[
            qo_spec, kv_spec, kv_spec,
            pl.BlockSpec((None, None, tq, LANES), lambda bi, gi, i: (bi, gi, i, 0)),
            pl.BlockSpec((1, LANES), lambda bi, gi, i: (0, 0)),
            pl.BlockSpec((None, tq, LANES), lambda bi, gi, i: (bi, i, gi)),
            qo_spec,
        ],
        out_specs=qo_spec,
        out_shape=jax.ShapeDtypeStruct((b, s, d), F32),
        compiler_params=_params("parallel", "parallel", "parallel"),
        name="nsa_sel_attn",
    )(q, ks, vs, selb, k_gain.reshape(1, LANES), gates, o_in)


def _win_attn_kernel(q_ref, k_ref, v_ref, kn_ref, g_ref, oin_ref, o_ref, *, tq, span, hpg):
    i = pl.program_id(2)
    q = q_ref[...]
    start = pl.multiple_of(jnp.maximum(i * tq - WINDOW, 0), tq)
    k = k_ref[pl.ds(start, span), :]
    v = v_ref[pl.ds(start, span), :]
    tpos = i * tq + lax.broadcasted_iota(jnp.int32, (tq, 1), 0)
    kpos = start + lax.broadcasted_iota(jnp.int32, (1, span), 1)
    diff = tpos - kpos
    bias = jnp.concatenate([jnp.where((diff >= 0) & (diff < WINDOW), 0.0, NEG_INF)] * hpg, axis=0)
    gates = g_ref[...]

    q8 = _stack_aug(q, [jnp.broadcast_to(-m, (tq, LANES)).astype(BF16) for m in _score_bounds(q, kn_ref[...], hpg)],
                    hpg)
    s = lax.dot_general(q8, k, NT_DIMS, preferred_element_type=F32) + bias
    acc = jnp.dot(jnp.exp(s).astype(BF16), v, preferred_element_type=F32)
    denom = acc[:, LANES:]
    _gated_store(o_ref, oin_ref, gates, acc[:, :LANES] / denom, 2 * hpg, tq, hpg)

    @pl.when(jnp.logical_not(jnp.min(denom) >= SOFTMAX_FLOOR))
    def _():
        q8x = _stack_aug(q, [jnp.zeros((tq, LANES), BF16)] * hpg, hpg)
        sx = lax.dot_general(q8x, k, NT_DIMS, preferred_element_type=F32) + bias
        p = jnp.exp(sx - jnp.max(sx, axis=-1, keepdims=True))
        acc_x = jnp.dot(p.astype(BF16), v, preferred_element_type=F32)
        _gated_store(o_ref, oin_ref, gates, acc_x[:, :LANES] / acc_x[:, LANES:], 2 * hpg, tq, hpg)


def _nsa_win_attn(q, kw, vw, k_gain, gates, o_in, *, groups, hpg):
    b, s, d = q.shape
    tq = _pick(s, (256, 128))
    span = WINDOW + tq
    assert s >= span and WINDOW % tq == 0
    qo_spec = pl.BlockSpec((None, tq, hpg * LANES), lambda bi, gi, i: (bi, i, gi))
    kv_spec = pl.BlockSpec((None, None, s, 2 * LANES), lambda bi, gi, i: (bi, gi, 0, 0))
    return pl.pallas_call(
        functools.partial(_win_attn_kernel, tq=tq, span=span, hpg=hpg),
        grid=(b, groups, s // tq),
        in_specs=[qo_spec, kv_spec, kv_spec, pl.BlockSpec((1, LANES), lambda bi, gi, i: (0, 0)),
                  pl.BlockSpec((None, tq, LANES), lambda bi, gi, i: (bi, i, gi)), qo_spec],
        out_specs=qo_spec,
        out_shape=jax.ShapeDtypeStruct((b, s, d), BF16),
        compiler_params=_params("parallel", "parallel", "parallel"),
        name="nsa_win_attn",
    )(q, kw, vw, k_gain.reshape(1, LANES), gates, o_in)


def _rope_tables(seq, dim):
    inv = ROPE_THETA ** (-jnp.arange(0, dim, 2, dtype=F32) / dim)
    ang = jnp.arange(seq, dtype=F32)[:, None] * inv[None, :]
    ang = jnp.concatenate([ang, ang], axis=-1)
    sign = jnp.where(jnp.arange(dim) < dim // 2, -1.0, 1.0).astype(F32)
    return jnp.cos(ang), jnp.sin(ang) * sign


def _nsa_mixer(x2, xg, ss, b, s, w_in, q_norm, k_norm, cmp_pos, cmp_w1, cmp_w2, w_o, next_gain):
    heads, groups = NSA_HEADS, NSA_GROUPS
    hpg = heads // groups
    d = x2.shape[1]
    assert d == heads * LANES and CMP_BLOCK == 2 * CMP_STRIDE and 3 * hpg <= LANES
    qd, kvd = heads * LANES, groups * LANES
    n_main = qd + 6 * kvd
    nseg = s // CMP_STRIDE
    n_cmp = (s - CMP_BLOCK) // CMP_STRIDE + 1
    n_slc = s // SLC_BLOCK
    top_n = min(SLC_TOPN, n_slc)
    assert n_cmp == nseg - 1 and n_slc % 8 == 0 and n_slc <= LANES and top_n >= 3

    proj = _mm(xg, ss, w_in, n=n_main, out_dtype=F32, name="nsa_in_proj")
    gates = _mm(xg, ss, w_in, col0=n_main, n=LANES, out_dtype=F32, sigmoid=True, name="nsa_gate_proj")
    gates = gates[:, :3 * heads].reshape(b * s, 3, groups, hpg).transpose(0, 2, 1, 3).reshape(b * s, groups, 3 * hpg)
    gates = jnp.pad(gates, ((0, 0), (0, 0), (0, LANES - 3 * hpg))).reshape(b, s, groups * LANES)

    cos, sin_signed = _rope_tables(s, LANES)
    q, kc, vc, ks, vs, kw, vw = _nsa_prep(proj.reshape(b, s, n_main), cos, sin_signed, q_norm, k_norm,
                                          heads=heads, groups=groups)

    seg_w = CMP_STRIDE * LANES
    pe = cmp_pos.reshape(2, 2, 1, seg_w)
    w1 = cmp_w1.reshape(2, 2, seg_w, LANES).astype(BF16)
    kcmp, vcmp = _nsa_compress(kc.reshape(b, groups, nseg, seg_w), vc.reshape(b, groups, nseg, seg_w),
                               pe, w1, cmp_w2.astype(BF16))

    cmp_start = np.arange(nseg) * CMP_STRIDE
    slc_start = np.arange(n_slc) * SLC_BLOCK
    ovl = ((cmp_start[None, :] < slc_start[:, None] + SLC_BLOCK) & (cmp_start[None, :] + CMP_BLOCK > slc_start[:, None])
           & (np.arange(nseg)[None, :] < n_cmp))
    ovl = jnp.asarray(ovl.astype(np.float32), dtype=BF16)

    o1, selm = _nsa_cmp_attn(q, kcmp, vcmp, gates, ovl, groups=groups, hpg=hpg, n_cmp=n_cmp, n_slc=n_slc, top_n=top_n)
    o2 = _nsa_sel_attn(q, ks, vs, selm, k_norm[1], gates, o1, groups=groups, hpg=hpg)
    o3 = _nsa_win_attn(q, kw, vw, k_norm[2], gates, o2, groups=groups, hpg=hpg)
    return _mm_res(o3.reshape(b * s, d), w_o.astype(BF16), x2, 1.0, next_gain=next_gain,
                   bm_prefs=(1024, 512, 256, 128), bn_prefs=(512, 256, 128), name="nsa_out_proj")


def _chunk_cumsum(x, row):
    n = x.shape[0]
    step = 1
    while step < n:
        x = x + jnp.where(row >= step, pltpu.roll(x, step, 0), 0.0)
        step *= 2
    return x


def _hgrn_kernel(q_ref, f_ref, i_ref, g_ref, lbl_ref, on_ref, o_ref, *, seq, chunk, layer, hps, unroll):
    lbl = lbl_ref[...]
    e = jnp.exp(lbl - jnp.max(lbl, axis=0, keepdims=True))
    p = e / jnp.sum(e, axis=0, keepdims=True)
    csum = p[0:1]
    for d in range(1, layer + 1):
        csum = csum + p[d:d + 1]
    lb = csum - p[0:1]
    o_gain = on_ref[...]
    causal = (lax.broadcasted_iota(jnp.int32, (chunk, chunk), 0) >= lax.broadcasted_iota(jnp.int32, (chunk, chunk), 1))
    row = lax.broadcasted_iota(jnp.int32, (chunk, hps * LANES), 0)
    heads = [slice(k * LANES, (k + 1) * LANES) for k in range(hps)]

    def body(it, states):
        states = list(states)
        chunks = []
        for u in range(unroll):
            rows = pl.ds(pl.multiple_of((it * unroll + u) * chunk, chunk), chunk)
            f = lb + (1.0 - lb) * jax.nn.sigmoid(f_ref[rows, :])
            kc = 1.0 - f
            g_cum = _chunk_cumsum(jnp.log(f), row)
            g_last = g_cum[chunk - 1:chunk, :]
            q_dec = (q_ref[rows, :] * jnp.exp(g_cum)).astype(BF16)
            k_inv = (kc * jnp.exp(-g_cum)).astype(BF16)
            k_tail = (kc * jnp.exp(g_last - g_cum)).astype(BF16)
            chunks.append((rows, q_dec, k_inv, k_tail, i_ref[rows, :].astype(BF16), jnp.exp(g_last)))
        a_all = [[lax.dot_general(q_dec[:, h], k_inv[:, h], NT_DIMS, preferred_element_type=F32) for h in heads]
                 for (_, q_dec, k_inv, _, _, _) in chunks]
        ds_all = [[lax.dot_general(v[:, h], k_tail[:, h], TN_DIMS, preferred_element_type=F32) for h in heads]
                  for (_, _, _, k_tail, v, _) in chunks]
        o_all = [[jnp.dot(jnp.where(causal, a, 0.0).astype(BF16), v[:, h], preferred_element_type=F32)
                  for a, h in zip(a_row, heads)]
                 for a_row, (_, _, _, _, v, _) in zip(a_all, chunks)]
        for u, (rows, q_dec, _, _, _, decay) in enumerate(chunks):
            for k, h in enumerate(heads):
                o = o_all[u][k] + lax.dot_general(q_dec[:, h], states[k].astype(BF16), NT_DIMS,
                                                  preferred_element_type=F32)
                states[k] = states[k] * decay[:, h] + ds_all[u][k]
                y = o * lax.rsqrt(jnp.mean(o * o, axis=-1, keepdims=True) + RMS_EPS) * o_gain
                gz = g_ref[rows, h]
                o_ref[rows, h] = (y * (gz * jax.nn.sigmoid(gz))).astype(o_ref.dtype)
        return tuple(states)

    init = tuple(jnp.zeros((LANES, LANES), F32) for _ in range(hps))
    lax.fori_loop(0, seq // (chunk * unroll), body, init)


def _hgrn_core(proj, lb_logits, o_norm, *, heads, layer, hps=4, unroll=4):
    b, s, _ = proj.shape
    depth = lb_logits.shape[0]
    hps = math.gcd(hps, heads)
    width = hps * LANES
    groups = heads // hps
    assert s % (HGRN_CHUNK * unroll) == 0

    def col_spec(part):
        return pl.BlockSpec((None, s, width), lambda bi, hi: (bi, 0, part * groups + hi))

    return pl.pallas_call(
        functools.partial(_hgrn_kernel, seq=s, chunk=HGRN_CHUNK, layer=layer, hps=hps, unroll=unroll),
        grid=(b, groups),
        in_specs=[col_spec(0), col_spec(1), col_spec(2), col_spec(3),
                  pl.BlockSpec((depth, width), lambda bi, hi: (0, hi)),
                  pl.BlockSpec((1, LANES), lambda bi, hi: (0, 0))],
        out_specs=pl.BlockSpec((None, s, width), lambda bi, hi: (bi, 0, hi)),
        out_shape=jax.ShapeDtypeStruct((b, s, heads * LANES), BF16),
        compiler_params=_params("parallel", "parallel"),
        name="hgrn_core",
    )(proj, proj, proj, proj, lb_logits, o_norm.reshape(1, LANES))


def _hgrn_mixer(x2, xg, ss, b, s, w_in, lb_logits, o_norm, w_o, layer, next_gain):
    heads = HGRN_HEADS
    d = x2.shape[1]
    assert d == heads * LANES and w_in.shape[1] == 4 * d and s % HGRN_CHUNK == 0
    proj = _mm(xg, ss, w_in, out_dtype=F32, name="hgrn_in_proj")
    o = _hgrn_core(proj.reshape(b, s, 4 * d), lb_logits, o_norm, heads=heads, layer=layer)
    return _mm_res(o.reshape(b * s, d), w_o.astype(BF16), x2, 1.0, next_gain=next_gain,
                   bm_prefs=(1024, 512, 256, 128), bn_prefs=(512, 256, 128), name="hgrn_out_proj")


def kernel(x, ffn_norm, ffn_w_gate, ffn_w_up, ffn_w_down, mix_norm, nsa_w_in, nsa_q_norm, nsa_k_norm, nsa_cmp_pos,
           nsa_cmp_w1, nsa_cmp_w2, nsa_w_o, hgrn_w_in, hgrn_lb_logits, hgrn_o_norm, hgrn_w_o):
    b, s, d = x.shape
    depth = ffn_norm.shape[0]
    x2 = x.reshape(b * s, d)
    xg, ss = _prenorm(x2, ffn_norm[0, 0])
    for layer in range(depth):
        slot = layer // N_MIXERS
        x2, xg, ss = _ffn(x2, xg, ss, ffn_w_gate, ffn_w_up, ffn_w_down, (layer, 0), mix_norm[layer])
        if layer % N_MIXERS == 0:
            x2, xg, ss = _nsa_mixer(x2, xg, ss, b, s, nsa_w_in[slot], nsa_q_norm[slot], nsa_k_norm[slot],
                                    nsa_cmp_pos[slot], nsa_cmp_w1[slot], nsa_cmp_w2[slot], nsa_w_o[slot],
                                    ffn_norm[layer, 1])
        else:
            x2, xg, ss = _hgrn_mixer(x2, xg, ss, b, s, hgrn_w_in[slot], hgrn_lb_logits, hgrn_o_norm[slot],
                                     hgrn_w_o[slot], layer, ffn_norm[layer, 1])
        if layer + 1 < depth:
            x2, xg, ss = _ffn(x2, xg, ss, ffn_w_gate, ffn_w_up, ffn_w_down, (layer, 1), ffn_norm[layer + 1, 0])
        else:
            x2 = _ffn(x2, xg, ss, ffn_w_gate, ffn_w_up, ffn_w_down, (layer, 1), None)
    return x2.reshape(b, s, d)
```

```python
import functools
import math

import jax
import jax.numpy as jnp
import numpy as np
from jax import lax
from jax.experimental import pallas as pl
from jax.experimental.pallas import tpu as pltpu

F32 = jnp.float32
BF16 = jnp.bfloat16

RMS_EPS = 1e-6
NEG_INF = -1e30
FORCED_SCORE = 1e30
FFN_RES_WEIGHT = 0.5
N_MIXERS = 2

NSA_HEADS = 32
NSA_GROUPS = 4
CMP_BLOCK = 32
CMP_STRIDE = 16
SLC_BLOCK = 64
SLC_TOPN = 16
_SLC_SHIFT = SLC_BLOCK.bit_length() - 1
assert 1 << _SLC_SHIFT == SLC_BLOCK
WINDOW = 512
ROPE_THETA = 10000.0

HGRN_HEADS = 32
HGRN_CHUNK = 64

LANES = 128
SOFTMAX_FLOOR = 1e-30
BOUND_SLACK = 1.02
VMEM_LIMIT_BYTES = 56 * 1024 * 1024

NT_DIMS = (((1,), (1,)), ((), ()))
TN_DIMS = (((0,), (0,)), ((), ()))


def _params(*sem):
    return pltpu.CompilerParams(dimension_semantics=sem, vmem_limit_bytes=VMEM_LIMIT_BYTES)


def _pick(n, prefs):
    for p in prefs:
        if n % p == 0:
            return p
    raise ValueError(f"no tile in {prefs} divides {n}")


def _fold_lanes(x):
    out = x[:, :LANES]
    for t in range(1, x.shape[1] // LANES):
        out = out + x[:, t * LANES:(t + 1) * LANES]
    return out


def _inv_rms(ss, width):
    total = jnp.sum(jnp.sum(ss, axis=0), axis=-1, keepdims=True)
    return lax.rsqrt(total * (1.0 / width) + RMS_EPS)


def _prenorm_kernel(x_ref, g_ref, xg_ref, ss_ref):
    x = x_ref[...]
    xg_ref[...] = (x * g_ref[...]).astype(xg_ref.dtype)
    ss_ref[0] = _fold_lanes(x * x)


def _prenorm(x, g):
    m, d = x.shape
    tr = _pick(m, (256, 128, 8))
    return pl.pallas_call(
        _prenorm_kernel,
        grid=(m // tr,),
        in_specs=[pl.BlockSpec((tr, d), lambda i: (i, 0)), pl.BlockSpec((1, d), lambda i: (0, 0))],
        out_specs=[pl.BlockSpec((tr, d), lambda i: (i, 0)), pl.BlockSpec((1, tr, LANES), lambda i: (0, i, 0))],
        out_shape=[jax.ShapeDtypeStruct((m, d), BF16), jax.ShapeDtypeStruct((1, m, LANES), F32)],
        compiler_params=_params("parallel"),
        name="prenorm",
    )(x, g.reshape(1, d))


def _weight_spec(w, lead, bn, first_tile=0):
    k = w.shape[-2]
    return pl.BlockSpec((None,) * len(lead) + (k, bn), lambda i, j: lead + (0, first_tile + j))


def _lhs_spec(bm, k, single_buffer):
    if single_buffer:
        return pl.BlockSpec((bm, k), lambda i, j: (i, 0), pipeline_mode=pl.Buffered(1))
    return pl.BlockSpec((bm, k), lambda i, j: (i, 0))


def _stat_spec(planes, bm):
    return pl.BlockSpec((planes, bm, LANES), lambda i, j: (0, i, 0))


def _gateup_kernel(xg_ref, ss_ref, wg_ref, wu_ref, *rest, width, cast_down):
    if cast_down:
        wd_ref, o_ref, wd_out_ref, r_ref = rest
        wd_out_ref[...] = wd_ref[...].astype(wd_out_ref.dtype)
    else:
        o_ref, r_ref = rest

    @pl.when(pl.program_id(1) == 0)
    def _():
        r_ref[...] = jnp.broadcast_to(_inv_rms(ss_ref[...], width), r_ref.shape)

    xg = xg_ref[...]
    r = jnp.concatenate([r_ref[...]] * (o_ref.shape[1] // LANES), axis=1)
    a = jnp.dot(xg, wg_ref[...].astype(BF16), preferred_element_type=F32) * r
    b = jnp.dot(xg, wu_ref[...].astype(BF16), preferred_element_type=F32) * r
    o_ref[...] = (a * jax.nn.sigmoid(a) * b).astype(o_ref.dtype)


def _gateup(xg, ss, wg, wu, wd, lead):
    m, k = xg.shape
    n = wg.shape[-1]
    bm = _pick(m, (2048, 1024, 512, 256, 128))
    bn = _pick(n, (256, 128))
    steps_j = n // bn
    steps = (m // bm) * steps_j
    slab = wd.shape[-2] // steps
    cast_down = slab * steps == wd.shape[-2] and slab % 16 == 0
    in_specs = [_lhs_spec(bm, k, True), _stat_spec(ss.shape[0], bm), _weight_spec(wg, lead, bn),
                _weight_spec(wu, lead, bn)]
    out_specs = pl.BlockSpec((bm, bn), lambda i, j: (i, j))
    out_shape = jax.ShapeDtypeStruct((m, n), BF16)
    args = (xg, ss, wg, wu)
    if cast_down:
        d_out = wd.shape[-1]
        in_specs.append(pl.BlockSpec((None,) * len(lead) + (slab, d_out), lambda i, j: lead + (i * steps_j + j, 0)))
        out_specs = [out_specs, pl.BlockSpec((slab, d_out), lambda i, j: (i * steps_j + j, 0))]
        out_shape = [out_shape, jax.ShapeDtypeStruct(wd.shape[-2:], BF16)]
        args = args + (wd,)
    out = pl.pallas_call(
        functools.partial(_gateup_kernel, width=k, cast_down=cast_down),
        grid=(m // bm, steps_j),
        in_specs=in_specs,
        out_specs=out_specs,
        out_shape=out_shape,
        scratch_shapes=[pltpu.VMEM((bm, LANES), F32)],
        compiler_params=_params("parallel", "arbitrary"),
        name="ffn_gateup",
    )(*args)
    return out if cast_down else (out, wd[lead].astype(BF16))


def _out_proj_kernel(a_ref, w_ref, r_ref, gn_ref, o_ref, xg_ref, ss_ref):
    x = r_ref[...] + jnp.dot(a_ref[...], w_ref[...], preferred_element_type=F32)
    o_ref[...] = x
    xg_ref[...] = (x * gn_ref[...]).astype(xg_ref.dtype)

    @pl.when(pl.program_id(1) == 0)
    def _():
        ss_ref[...] = jnp.zeros_like(ss_ref)

    ss_ref[0] += _fold_lanes(x * x)


def _out_proj(a, w, res, next_gain, *, name):
    m, k = a.shape
    n = w.shape[-1]
    bm = _pick(m, (1024, 512, 256, 128))
    bn = _pick(n, (512, 256, 128))
    tile = pl.BlockSpec((bm, bn), lambda i, j: (i, j))
    return pl.pallas_call(
        _out_proj_kernel,
        grid=(m // bm, n // bn),
        in_specs=[_lhs_spec(bm, k, False), _weight_spec(w, (), bn), tile, pl.BlockSpec((1, bn), lambda i, j: (0, j))],
        out_specs=[tile, tile, _stat_spec(1, bm)],
        out_shape=[jax.ShapeDtypeStruct((m, n), F32), jax.ShapeDtypeStruct((m, n), BF16),
                   jax.ShapeDtypeStruct((1, m, LANES), F32)],
        compiler_params=_params("parallel", "arbitrary"),
        name=name,
    )(a, w, res, next_gain.reshape(1, n))


def _mm_kernel(xg_ref, ss_ref, w_ref, o_ref, *, width, sigmoid):
    acc = jnp.dot(xg_ref[...], w_ref[...].astype(BF16), preferred_element_type=F32) * _inv_rms(ss_ref[...], width)
    if sigmoid:
        acc = jax.nn.sigmoid(acc)
    o_ref[...] = acc.astype(o_ref.dtype)


def _mm(xg, ss, w, *, col0=0, n=None, out_dtype, sigmoid=False, name):
    m, k = xg.shape
    n = w.shape[-1] if n is None else n
    bn = _pick(n, (512, 256, 128))
    assert col0 % bn == 0
    tall = n // bn >= 8
    bm = _pick(m, (2048, 1024, 512, 256, 128) if tall else (1024, 512, 256, 128))
    return pl.pallas_call(
        functools.partial(_mm_kernel, width=k, sigmoid=sigmoid),
        grid=(m // bm, n // bn),
        in_specs=[_lhs_spec(bm, k, tall), _stat_spec(ss.shape[0], bm), _weight_spec(w, (), bn, col0 // bn)],
        out_specs=pl.BlockSpec((bm, bn), lambda i, j: (i, j)),
        out_shape=jax.ShapeDtypeStruct((m, n), out_dtype),
        compiler_params=_params("parallel", "arbitrary"),
        name=name,
    )(xg, ss, w)


def _ffn_down_kernel(a_ref, w_ref, r_ref, *rest, scale, emit_norm):
    x = r_ref[...] + scale * jnp.dot(a_ref[...], w_ref[...], preferred_element_type=F32)
    if not emit_norm:
        (o_ref,) = rest
        o_ref[...] = x
        return
    gn_ref, o_ref, xg_ref, ss_ref = rest
    o_ref[...] = x
    xg_ref[...] = (x * gn_ref[...]).astype(xg_ref.dtype)
    ss_ref[...] = _fold_lanes(x * x)


def _ffn_down(act, w, res, scale, next_gain):
    m, k = act.shape
    n = w.shape[-1]
    bm = _pick(m, (512, 256, 128))
    bn = _pick(n, (1024, 512, 256, 128))
    tile = pl.BlockSpec((bm, bn), lambda j, i: (i, j))
    emit_norm = next_gain is not None
    in_specs = [pl.BlockSpec((bm, k), lambda j, i: (i, 0)),
                pl.BlockSpec((k, bn), lambda j, i: (0, j), pipeline_mode=pl.Buffered(1)),
                tile]
    out_specs, out_shape, args = tile, jax.ShapeDtypeStruct((m, n), F32), (act, w, res)
    if emit_norm:
        in_specs.append(pl.BlockSpec((1, bn), lambda j, i: (0, j)))
        out_specs = [tile, tile, pl.BlockSpec((None, bm, LANES), lambda j, i: (j, i, 0))]
        out_shape = [out_shape, jax.ShapeDtypeStruct((m, n), BF16), jax.ShapeDtypeStruct((n // bn, m, LANES), F32)]
        args = args + (next_gain.reshape(1, n),)
    return pl.pallas_call(
        functools.partial(_ffn_down_kernel, scale=scale, emit_norm=emit_norm),
        grid=(n // bn, m // bm),
        in_specs=in_specs,
        out_specs=out_specs,
        out_shape=out_shape,
        compiler_params=_params("parallel", "parallel"),
        name="ffn_down",
    )(*args)


def _ffn(x, xg, ss, w_gate, w_up, w_down, lead, next_gain):
    act, w_down_bf16 = _gateup(xg, ss, w_gate, w_up, w_down, lead)
    return _ffn_down(act, w_down_bf16, x, FFN_RES_WEIGHT, next_gain)


def _nsa_prep_kernel(p_ref, cos_ref, sin_ref, qn_ref, kn_ref, q_ref, kc_ref, vc_ref, ks_ref, vs_ref, kw_ref, vw_ref,
                     *, heads, groups):
    cos = cos_ref[...]
    sin = sin_ref[...]
    scale = LANES ** -0.5
    base = heads * LANES
    kv_w = groups * LANES
    ts = p_ref.shape[0]
    pos = pl.program_id(1) * ts + lax.broadcasted_iota(jnp.int32, (ts, LANES), 0)
    lane = lax.broadcasted_iota(jnp.int32, (ts, LANES), 1)
    blk_onehot = jnp.where(lane == jnp.right_shift(pos, _SLC_SHIFT), 1.0, 0.0).astype(ks_ref.dtype)
    lane0_onehot = jnp.where(lane == 0, 1.0, 0.0).astype(kw_ref.dtype)

    def store_q(h):
        def store(y):
            q_ref[:, h * LANES:(h + 1) * LANES] = y.astype(q_ref.dtype)
        return store

    def store_k(k_out, g, upper):
        def store(y):
            if upper is None:
                k_out[g] = y.astype(k_out.dtype)
            else:
                k_out[g, :, :LANES] = y.astype(k_out.dtype)
                k_out[g, :, LANES:] = upper
        return store

    jobs = [(h * LANES, qn_ref[...], scale, store_q(h)) for h in range(heads)]
    for br, (k_out, upper) in enumerate(((kc_ref, None), (ks_ref, blk_onehot), (kw_ref, lane0_onehot))):
        for g in range(groups):
            jobs.append((base + (2 * br) * kv_w + g * LANES, kn_ref[br:br + 1, :], None, store_k(k_out, g, upper)))
    inv = [lax.rsqrt(jnp.mean(jnp.square(p_ref[:, c:c + LANES]), axis=-1, keepdims=True) + RMS_EPS)
           for c, _, _, _ in jobs]
    for (c, gain, post, store), r in zip(jobs, inv):
        y = p_ref[:, c:c + LANES] * r * gain
        y = y * cos + pltpu.roll(y, LANES // 2, 1) * sin
        store(y if post is None else y * post)

    for br, v_out in enumerate((vc_ref, vs_ref, vw_ref)):
        for g in range(groups):
            cv = base + (2 * br + 1) * kv_w + g * LANES
            v = p_ref[:, cv:cv + LANES].astype(v_out.dtype)
            if br == 0:
                v_out[g] = v
            else:
                v_out[g, :, :LANES] = v
                v_out[g, :, LANES:] = jnp.ones_like(v)


def _nsa_prep(proj, cos, sin_signed, q_norm, k_norm, *, heads, groups):
    b, s, n = proj.shape
    ts = _pick(s, (256, 128))
    kv_spec = pl.BlockSpec((None, groups, ts, LANES), lambda bi, i: (bi, 0, i, 0))
    aug_spec = pl.BlockSpec((None, groups, ts, 2 * LANES), lambda bi, i: (bi, 0, i, 0))
    tab_spec = pl.BlockSpec((ts, LANES), lambda bi, i: (i, 0))

    def kv_shape(dt, width=LANES):
        return jax.ShapeDtypeStruct((b, groups, s, width), dt)

    return pl.pallas_call(
        functools.partial(_nsa_prep_kernel, heads=heads, groups=groups),
        grid=(b, s // ts),
        in_specs=[
            pl.BlockSpec((None, ts, n), lambda bi, i: (bi, i, 0)),
            tab_spec,
            tab_spec,
            pl.BlockSpec((1, LANES), lambda bi, i: (0, 0)),
            pl.BlockSpec((3, LANES), lambda bi, i: (0, 0)),
        ],
        out_specs=[pl.BlockSpec((None, ts, heads * LANES), lambda bi, i: (bi, i, 0)),
                   kv_spec, kv_spec, aug_spec, aug_spec, aug_spec, aug_spec],
        out_shape=[jax.ShapeDtypeStruct((b, s, heads * LANES), BF16),
                   kv_shape(F32), kv_shape(F32), kv_shape(BF16, 2 * LANES), kv_shape(BF16, 2 * LANES),
                   kv_shape(BF16, 2 * LANES), kv_shape(BF16, 2 * LANES)],
        compiler_params=_params("parallel", "parallel"),
        name="nsa_prep",
    )(proj, cos, sin_signed, q_norm.reshape(1, LANES), k_norm)


def _gelu_tanh(x):
    return x * (0.5 * (1.0 + jnp.tanh(math.sqrt(2.0 / math.pi) * (x + 0.044715 * (x * x * x)))))


def _compress_kernel(k_ref, v_ref, pe_ref, w1_ref, w2_ref, ko_ref, vo_ref):
    nseg = ko_ref.shape[0]
    for which, (x_ref, o_ref) in enumerate(((k_ref, ko_ref), (v_ref, vo_ref))):
        top = jnp.zeros((nseg, LANES), F32)
        bot = jnp.zeros((nseg, LANES), F32)
        for r in range(CMP_STRIDE):
            rows = x_ref[pl.ds(r, nseg, stride=CMP_STRIDE), :]
            top = top + jnp.dot((rows + pe_ref[which, r:r + 1, :]).astype(BF16), w1_ref[which, r],
                                preferred_element_type=F32)
            bot = bot + jnp.dot((rows + pe_ref[which, CMP_STRIDE + r:CMP_STRIDE + r + 1, :]).astype(BF16),
                                w1_ref[which, CMP_STRIDE + r], preferred_element_type=F32)
        pre = top + pltpu.roll(bot, nseg - 1, 0)
        hid = _gelu_tanh(pre).astype(BF16)
        o_ref[...] = jnp.dot(hid, w2_ref[which], preferred_element_type=F32).astype(o_ref.dtype)


def _nsa_compress(kc, vc, pe, w1, w2):
    b, g, s, _ = kc.shape
    nseg = s // CMP_STRIDE
    x_spec = pl.BlockSpec((None, None, s, LANES), lambda bi, gi: (bi, gi, 0, 0))
    o_spec = pl.BlockSpec((None, None, nseg, LANES), lambda bi, gi: (bi, gi, 0, 0))
    o_shape = jax.ShapeDtypeStruct((b, g, nseg, LANES), BF16)
    return pl.pallas_call(
        _compress_kernel,
        grid=(b, g),
        in_specs=[
            x_spec, x_spec,
            pl.BlockSpec((2, CMP_BLOCK, LANES), lambda bi, gi: (0, 0, 0)),
            pl.BlockSpec((2, CMP_BLOCK, LANES, LANES), lambda bi, gi: (0, 0, 0, 0)),
            pl.BlockSpec((2, LANES, LANES), lambda bi, gi: (0, 0, 0)),
        ],
        out_specs=[o_spec, o_spec],
        out_shape=[o_shape, o_shape],
        compiler_params=_params("parallel", "parallel"),
        name="nsa_compress",
    )(kc, vc, pe, w1, w2)


def _stack_heads(q, hpg):
    return jnp.concatenate([q[:, h * LANES:(h + 1) * LANES] for h in range(hpg)], axis=0)


def _cmp_attn_kernel(q_ref, kc_ref, vc_ref, g_ref, ovl_ref, o_ref, sel_ref, *, tq, hpg, n_cmp, n_slc, top_n):
    i = pl.program_id(2)
    q8 = _stack_heads(q_ref[...], hpg)
    rows = hpg * tq
    ncp = kc_ref.shape[0]
    s = lax.dot_general(q8, kc_ref[...], NT_DIMS, preferred_element_type=F32)
    tpos = i * tq + lax.broadcasted_iota(jnp.int32, (tq, 1), 0)
    tpos8 = jnp.concatenate([tpos] * hpg, axis=0)
    ncol = lax.broadcasted_iota(jnp.int32, (1, ncp), 1)
    valid = jnp.where((ncol * CMP_STRIDE + (CMP_BLOCK - 1) <= tpos8) & (ncol < n_cmp), 1.0, 0.0)
    sm = jnp.where(valid > 0.5, s, NEG_INF)
    m = jnp.max(sm, axis=-1, keepdims=True)
    e = jnp.exp(sm - m) * valid
    l = jnp.sum(e, axis=-1, keepdims=True)
    p = e / jnp.where(l > 0.0, l, 1.0)
    o = jnp.dot(p.astype(BF16), vc_ref[...], preferred_element_type=F32)
    gates = g_ref[...]
    for h in range(hpg):
        o_ref[:, h * LANES:(h + 1) * LANES] = o[h * tq:(h + 1) * tq] * gates[:, h:h + 1]

    psum = p[0:tq]
    for h in range(1, hpg):
        psum = psum + p[h * tq:(h + 1) * tq]
    hi = psum.astype(BF16)
    lo = (psum - hi.astype(F32)).astype(BF16)
    ovl = ovl_ref[...]
    imp = (lax.dot_general(ovl, hi, NT_DIMS, preferred_element_type=F32)
           + lax.dot_general(ovl, lo, NT_DIMS, preferred_element_type=F32))
    jidx = lax.broadcasted_iota(jnp.int32, (n_slc, tq), 0)
    tq_pos = i * tq + lax.broadcasted_iota(jnp.int32, (n_slc, tq), 1)
    cur = jnp.right_shift(tq_pos, _SLC_SHIFT)
    forced = (jidx == 0) | (jidx == cur) | (jidx == cur - 1)
    causal = jidx * SLC_BLOCK <= tq_pos
    imp = jnp.where(forced, FORCED_SCORE, jnp.where(causal, imp, NEG_INF))
    sel_rows = []
    for j in range(n_slc):
        row = imp[j:j + 1, :]
        lower = jnp.where(jidx < j, 1.0, 0.0)
        beats = jnp.where(imp > row, 1.0, jnp.where(imp == row, lower, 0.0))
        rank = jnp.sum(beats, axis=0, keepdims=True)
        sel_rows.append(jnp.where(rank < top_n, 0.0, NEG_INF))
    sel_rows.append(jnp.zeros((LANES - n_slc, tq), F32))
    bias_t = jnp.concatenate(sel_rows, axis=0)
    sel_ref[...] = bias_t.T.astype(sel_ref.dtype)


def _nsa_cmp_attn(q, kcmp, vcmp, gates, ovl, *, groups, hpg, n_cmp, n_slc, top_n):
    b, s, d = q.shape
    ncp = kcmp.shape[2]
    tq = _pick(s, (512, 256, 128))
    qo_spec = pl.BlockSpec((None, tq, hpg * LANES), lambda bi, gi, i: (bi, i, gi))
    c_spec = pl.BlockSpec((None, None, ncp, LANES), lambda bi, gi, i: (bi, gi, 0, 0))
    return pl.pallas_call(
        functools.partial(_cmp_attn_kernel, tq=tq, hpg=hpg, n_cmp=n_cmp, n_slc=n_slc, top_n=top_n),
        grid=(b, groups, s // tq),
        in_specs=[
            qo_spec, c_spec, c_spec,
            pl.BlockSpec((None, tq, LANES), lambda bi, gi, i: (bi, i, gi)),
            pl.BlockSpec((n_slc, ncp), lambda bi, gi, i: (0, 0)),
        ],
        out_specs=[qo_spec, pl.BlockSpec((None, None, tq, LANES), lambda bi, gi, i: (bi, gi, i, 0))],
        out_shape=[jax.ShapeDtypeStruct((b, s, d), F32), jax.ShapeDtypeStruct((b, groups, s, LANES), BF16)],
        compiler_params=_params("parallel", "parallel", "parallel"),
        name="nsa_cmp_attn",
    )(q, kcmp, vcmp, gates, ovl)


def _stack_aug(q, upper, hpg):
    return jnp.concatenate(
        [jnp.concatenate([q[:, h * LANES:(h + 1) * LANES], upper[h]], axis=1) for h in range(hpg)], axis=0)


def _score_bounds(q, k_gain, hpg):
    k_bound = BOUND_SLACK * math.sqrt(LANES) * jnp.max(jnp.abs(k_gain), axis=-1, keepdims=True)
    out = []
    for h in range(hpg):
        qh = q[:, h * LANES:(h + 1) * LANES].astype(F32)
        out.append(jnp.sqrt(jnp.sum(qh * qh, axis=-1, keepdims=True)) * k_bound)
    return out


def _gated_store(o_ref, oin_ref, gates, o, first_gate, tq, hpg):
    for h in range(hpg):
        cols = slice(h * LANES, (h + 1) * LANES)
        gate = gates[:, first_gate + h:first_gate + h + 1]
        o_ref[:, cols] = (oin_ref[:, cols] + o[h * tq:(h + 1) * tq] * gate).astype(o_ref.dtype)


def _sel_attn_kernel(q_ref, k_ref, v_ref, sel_ref, kn_ref, g_ref, oin_ref, o_ref, *, tq, tk, hpg):
    i = pl.program_id(2)
    q = q_ref[...]
    bias = sel_ref[...].astype(F32)
    rows = hpg * tq
    tpos = i * tq + lax.broadcasted_iota(jnp.int32, (tq, 1), 0)
    n_kv = ((i + 1) * tq + tk - 1) // tk
    gates = g_ref[...]

    def causal_bias(start):
        kpos = start + lax.broadcasted_iota(jnp.int32, (1, tk), 1)
        return jnp.concatenate([jnp.where(kpos <= tpos, 0.0, NEG_INF)] * hpg, axis=0)

    q8 = _stack_aug(q, [(bias - m).astype(BF16) for m in _score_bounds(q, kn_ref[...], hpg)], hpg)

    def fast_tile(kv, acc, diagonal):
        start = pl.multiple_of(kv * tk, tk)
        s = lax.dot_general(q8, k_ref[pl.ds(start, tk), :], NT_DIMS, preferred_element_type=F32)
        if diagonal:
            s = s + causal_bias(start)
        return acc + jnp.dot(jnp.exp(s).astype(BF16), v_ref[pl.ds(start, tk), :], preferred_element_type=F32)

    acc = lax.fori_loop(0, n_kv - 1, lambda kv, a: fast_tile(kv, a, False), jnp.zeros((rows, 2 * LANES), F32))
    acc = fast_tile(n_kv - 1, acc, True)
    denom = acc[:, LANES:]
    _gated_store(o_ref, oin_ref, gates, acc[:, :LANES] / denom, hpg, tq, hpg)

    @pl.when(jnp.logical_not(jnp.min(denom) >= SOFTMAX_FLOOR))
    def _():
        q8x = _stack_aug(q, [sel_ref[...]] * hpg, hpg)

        def exact_tile(kv, carry, diagonal):
            m, acc = carry
            start = pl.multiple_of(kv * tk, tk)
            s = lax.dot_general(q8x, k_ref[pl.ds(start, tk), :], NT_DIMS, preferred_element_type=F32)
            if diagonal:
                s = s + causal_bias(start)
            m_new = jnp.maximum(m, jnp.max(s, axis=-1, keepdims=True))
            p = jnp.exp(s - m_new).astype(BF16)
            acc = jnp.exp(m - m_new) * acc + jnp.dot(p, v_ref[pl.ds(start, tk), :], preferred_element_type=F32)
            return m_new, acc

        init = (jnp.full((rows, 1), NEG_INF, F32), jnp.zeros((rows, 2 * LANES), F32))
        carry = lax.fori_loop(0, n_kv - 1, lambda kv, c: exact_tile(kv, c, False), init)
        _, acc_x = exact_tile(n_kv - 1, carry, True)
        _gated_store(o_ref, oin_ref, gates, acc_x[:, :LANES] / acc_x[:, LANES:], hpg, tq, hpg)


def _nsa_sel_attn(q, ks, vs, selb, k_gain, gates, o_in, *, groups, hpg):
    b, s, d = q.shape
    tq = _pick(s, (512, 256, 128))
    tk = _pick(s, (512, 256, 128))
    qo_spec = pl.BlockSpec((None, tq, hpg * LANES), lambda bi, gi, i: (bi, i, gi))
    kv_spec = pl.BlockSpec((None, None, s, 2 * LANES), lambda bi, gi, i: (bi, gi, 0, 0))
    return pl.pallas_call(
        functools.partial(_sel_attn_kernel, tq=tq, tk=tk, hpg=hpg),
        grid=(b, groups, s // tq),
        in_specs=[
            qo_spec, kv_spec, kv_spec,
            pl.BlockSpec((None, None, tq, LANES), lambda bi, gi, i: (bi, gi, i, 0)),
            pl.BlockSpec((1, LANES), lambda bi, gi, i: (0, 0)),
            pl.BlockSpec((None, tq, LANES), lambda bi, gi, i: (bi, i, gi)),
            qo_spec,
        ],
        out_specs=qo_spec,
        out_shape=jax.ShapeDtypeStruct((b, s, d), F32),
        compiler_params=_params("parallel", "parallel", "parallel"),
        name="nsa_sel_attn",
    )(q, ks, vs, selb, k_gain.reshape(1, LANES), gates, o_in)


def _win_attn_kernel(q_ref, k_ref, v_ref, kn_ref, g_ref, oin_ref, o_ref, *, tq, span, hpg):
    i = pl.program_id(2)
    q = q_ref[...]
    start = pl.multiple_of(jnp.maximum(i * tq - WINDOW, 0), tq)
    k = k_ref[pl.ds(start, span), :]
    v = v_ref[pl.ds(start, span), :]
    tpos = i * tq + lax.broadcasted_iota(jnp.int32, (tq, 1), 0)
    kpos = start + lax.broadcasted_iota(jnp.int32, (1, span), 1)
    diff = tpos - kpos
    bias = jnp.concatenate([jnp.where((diff >= 0) & (diff < WINDOW), 0.0, NEG_INF)] * hpg, axis=0)
    gates = g_ref[...]

    q8 = _stack_aug(q, [jnp.broadcast_to(-m, (tq, LANES)).astype(BF16) for m in _score_bounds(q, kn_ref[...], hpg)],
                    hpg)
    s = lax.dot_general(q8, k, NT_DIMS, preferred_element_type=F32) + bias
    acc = jnp.dot(jnp.exp(s).astype(BF16), v, preferred_element_type=F32)
    denom = acc[:, LANES:]
    _gated_store(o_ref, oin_ref, gates, acc[:, :LANES] / denom, 2 * hpg, tq, hpg)

    @pl.when(jnp.logical_not(jnp.min(denom) >= SOFTMAX_FLOOR))
    def _():
        q8x = _stack_aug(q, [jnp.zeros((tq, LANES), BF16)] * hpg, hpg)
        sx = lax.dot_general(q8x, k, NT_DIMS, preferred_element_type=F32) + bias
        p = jnp.exp(sx - jnp.max(sx, axis=-1, keepdims=True))
        acc_x = jnp.dot(p.astype(BF16), v, preferred_element_type=F32)
        _gated_store(o_ref, oin_ref, gates, acc_x[:, :LANES] / acc_x[:, LANES:], 2 * hpg, tq, hpg)


def _nsa_win_attn(q, kw, vw, k_gain, gates, o_in, *, groups, hpg):
    b, s, d = q.shape
    tq = _pick(s, (256, 128))
    span = WINDOW + tq
    assert s >= span and WINDOW % tq == 0
    qo_spec = pl.BlockSpec((None, tq, hpg * LANES), lambda bi, gi, i: (bi, i, gi))
    kv_spec = pl.BlockSpec((None, None, s, 2 * LANES), lambda bi, gi, i: (bi, gi, 0, 0))
    return pl.pallas_call(
        functools.partial(_win_attn_kernel, tq=tq, span=span, hpg=hpg),
        grid=(b, groups, s // tq),
        in_specs=[qo_spec, kv_spec, kv_spec, pl.BlockSpec((1, LANES), lambda bi, gi, i: (0, 0)),
                  pl.BlockSpec((None, tq, LANES), lambda bi, gi, i: (bi, i, gi)), qo_spec],
        out_specs=qo_spec,
        out_shape=jax.ShapeDtypeStruct((b, s, d), BF16),
        compiler_params=_params("parallel", "parallel", "parallel"),
        name="nsa_win_attn",
    )(q, kw, vw, k_gain.reshape(1, LANES), gates, o_in)


def _rope_tables(seq, dim):
    inv = ROPE_THETA ** (-jnp.arange(0, dim, 2, dtype=F32) / dim)
    ang = jnp.arange(seq, dtype=F32)[:, None] * inv[None, :]
    ang = jnp.concatenate([ang, ang], axis=-1)
    sign = jnp.where(jnp.arange(dim) < dim // 2, -1.0, 1.0).astype(F32)
    return jnp.cos(ang), jnp.sin(ang) * sign


def _nsa_mixer(x2, xg, ss, b, s, w_in, q_norm, k_norm, cmp_pos, cmp_w1, cmp_w2, w_o, next_gain):
    heads, groups = NSA_HEADS, NSA_GROUPS
    hpg = heads // groups
    d = x2.shape[1]
    assert d == heads * LANES and CMP_BLOCK == 2 * CMP_STRIDE and 3 * hpg <= LANES
    qd, kvd = heads * LANES, groups * LANES
    n_main = qd + 6 * kvd
    nseg = s // CMP_STRIDE
    n_cmp = (s - CMP_BLOCK) // CMP_STRIDE + 1
    n_slc = s // SLC_BLOCK
    top_n = min(SLC_TOPN, n_slc)
    assert n_cmp == nseg - 1 and n_slc % 8 == 0 and n_slc <= LANES and top_n >= 3

    proj = _mm(xg, ss, w_in, n=n_main, out_dtype=F32, name="nsa_in_proj")
    gates = _mm(xg, ss, w_in, col0=n_main, n=LANES, out_dtype=F32, sigmoid=True, name="nsa_gate_proj")
    gates = gates[:, :3 * heads].reshape(b * s, 3, groups, hpg).transpose(0, 2, 1, 3).reshape(b * s, groups, 3 * hpg)
    gates = jnp.pad(gates, ((0, 0), (0, 0), (0, LANES - 3 * hpg))).reshape(b, s, groups * LANES)

    cos, sin_signed = _rope_tables(s, LANES)
    q, kc, vc, ks, vs, kw, vw = _nsa_prep(proj.reshape(b, s, n_main), cos, sin_signed, q_norm, k_norm,
                                          heads=heads, groups=groups)

    w1 = cmp_w1.reshape(2, CMP_BLOCK, LANES, LANES).astype(BF16)
    kcmp, vcmp = _nsa_compress(kc, vc, cmp_pos, w1, cmp_w2.astype(BF16))

    cmp_start = np.arange(nseg) * CMP_STRIDE
    slc_start = np.arange(n_slc) * SLC_BLOCK
    ovl = ((cmp_start[None, :] < slc_start[:, None] + SLC_BLOCK) & (cmp_start[None, :] + CMP_BLOCK > slc_start[:, None])
           & (np.arange(nseg)[None, :] < n_cmp))
    ovl = jnp.asarray(ovl.astype(np.float32), dtype=BF16)

    o1, selm = _nsa_cmp_attn(q, kcmp, vcmp, gates, ovl, groups=groups, hpg=hpg, n_cmp=n_cmp, n_slc=n_slc, top_n=top_n)
    o2 = _nsa_sel_attn(q, ks, vs, selm, k_norm[1], gates, o1, groups=groups, hpg=hpg)
    o3 = _nsa_win_attn(q, kw, vw, k_norm[2], gates, o2, groups=groups, hpg=hpg)
    return _out_proj(o3.reshape(b * s, d), w_o.astype(BF16), x2, next_gain, name="nsa_out_proj")


def _chunk_cumsum(x, tri):
    w = x.shape[1]
    h1 = x.astype(BF16)
    r1 = x - h1.astype(F32)
    h2 = r1.astype(BF16)
    h3 = (r1 - h2.astype(F32)).astype(BF16)
    g3 = jnp.dot(tri, jnp.concatenate([h1, h2, h3], axis=1), preferred_element_type=F32)
    return g3[:, :w] + g3[:, w:2 * w] + g3[:, 2 * w:]


def _hgrn_kernel(q_ref, f_ref, i_ref, g_ref, lbl_ref, on_ref, o_ref, *, seq, chunk, layer, hps, unroll):
    lbl = lbl_ref[...]
    e = jnp.exp(lbl - jnp.max(lbl, axis=0, keepdims=True))
    p = e / jnp.sum(e, axis=0, keepdims=True)
    csum = p[0:1]
    for d in range(1, layer + 1):
        csum = csum + p[d:d + 1]
    lb = csum - p[0:1]
    o_gain = on_ref[...]
    causal = (lax.broadcasted_iota(jnp.int32, (chunk, chunk), 0) >= lax.broadcasted_iota(jnp.int32, (chunk, chunk), 1))
    tri = jnp.where(causal, 1.0, 0.0).astype(BF16)
    heads = [slice(k * LANES, (k + 1) * LANES) for k in range(hps)]

    def body(it, states):
        states = list(states)
        chunks = []
        for u in range(unroll):
            rows = pl.ds(pl.multiple_of((it * unroll + u) * chunk, chunk), chunk)
            f = lb + (1.0 - lb) * jax.nn.sigmoid(f_ref[rows, :])
            kc = 1.0 - f
            g_cum = _chunk_cumsum(jnp.log(f), tri)
            g_last = g_cum[chunk - 1:chunk, :]
            q_dec = (q_ref[rows, :] * jnp.exp(g_cum)).astype(BF16)
            k_inv = (kc * jnp.exp(-g_cum)).astype(BF16)
            k_tail = (kc * jnp.exp(g_last - g_cum)).astype(BF16)
            chunks.append((rows, q_dec, k_inv, k_tail, i_ref[rows, :].astype(BF16), jnp.exp(g_last)))
        a_all = [[lax.dot_general(q_dec[:, h], k_inv[:, h], NT_DIMS, preferred_element_type=F32) for h in heads]
                 for (_, q_dec, k_inv, _, _, _) in chunks]
        ds_all = [[lax.dot_general(v[:, h], k_tail[:, h], TN_DIMS, preferred_element_type=F32) for h in heads]
                  for (_, _, _, k_tail, v, _) in chunks]
        o_all = [[jnp.dot(jnp.where(causal, a, 0.0).astype(BF16), v[:, h], preferred_element_type=F32)
                  for a, h in zip(a_row, heads)]
                 for a_row, (_, _, _, _, v, _) in zip(a_all, chunks)]
        for u, (rows, q_dec, _, _, _, decay) in enumerate(chunks):
            for k, h in enumerate(heads):
                o = o_all[u][k] + lax.dot_general(q_dec[:, h], states[k].astype(BF16), NT_DIMS,
                                                  preferred_element_type=F32)
                states[k] = states[k] * decay[:, h] + ds_all[u][k]
                y = o * lax.rsqrt(jnp.mean(o * o, axis=-1, keepdims=True) + RMS_EPS) * o_gain
                gz = g_ref[rows, h]
                o_ref[rows, h] = (y * (gz * jax.nn.sigmoid(gz))).astype(o_ref.dtype)
        return tuple(states)

    init = tuple(jnp.zeros((LANES, LANES), F32) for _ in range(hps))
    lax.fori_loop(0, seq // (chunk * unroll), body, init)


def _hgrn_core(proj, lb_logits, o_norm, *, heads, layer, hps=4, unroll=4):
    b, s, _ = proj.shape
    depth = lb_logits.shape[0]
    hps = math.gcd(hps, heads)
    width = hps * LANES
    groups = heads // hps
    assert s % (HGRN_CHUNK * unroll) == 0

    def col_spec(part):
        return pl.BlockSpec((None, s, width), lambda bi, hi: (bi, 0, part * groups + hi))

    return pl.pallas_call(
        functools.partial(_hgrn_kernel, seq=s, chunk=HGRN_CHUNK, layer=layer, hps=hps, unroll=unroll),
        grid=(b, groups),
        in_specs=[col_spec(0), col_spec(1), col_spec(2), col_spec(3),
                  pl.BlockSpec((depth, width), lambda bi, hi: (0, hi)),
                  pl.BlockSpec((1, LANES), lambda bi, hi: (0, 0))],
        out_specs=pl.BlockSpec((None, s, width), lambda bi, hi: (bi, 0, hi)),
        out_shape=jax.ShapeDtypeStruct((b, s, heads * LANES), BF16),
        compiler_params=_params("parallel", "parallel"),
        name="hgrn_core",
    )(proj, proj, proj, proj, lb_logits, o_norm.reshape(1, LANES))


def _hgrn_mixer(x2, xg, ss, b, s, w_in, lb_logits, o_norm, w_o, layer, next_gain):
    heads = HGRN_HEADS
    d = x2.shape[1]
    assert d == heads * LANES and w_in.shape[1] == 4 * d and s % HGRN_CHUNK == 0
    proj = _mm(xg, ss, w_in, out_dtype=F32, name="hgrn_in_proj")
    o = _hgrn_core(proj.reshape(b, s, 4 * d), lb_logits, o_norm, heads=heads, layer=layer)
    return _out_proj(o.reshape(b * s, d), w_o.astype(BF16), x2, next_gain, name="hgrn_out_proj")


def kernel(x, ffn_norm, ffn_w_gate, ffn_w_up, ffn_w_down, mix_norm, nsa_w_in, nsa_q_norm, nsa_k_norm, nsa_cmp_pos,
           nsa_cmp_w1, nsa_cmp_w2, nsa_w_o, hgrn_w_in, hgrn_lb_logits, hgrn_o_norm, hgrn_w_o):
    b, s, d = x.shape
    depth = ffn_norm.shape[0]
    x2 = x.reshape(b * s, d)
    xg, ss = _prenorm(x2, ffn_norm[0, 0])
    for layer in range(depth):
        slot = layer // N_MIXERS
        x2, xg, ss = _ffn(x2, xg, ss, ffn_w_gate, ffn_w_up, ffn_w_down, (layer, 0), mix_norm[layer])
        if layer % N_MIXERS == 0:
            x2, xg, ss = _nsa_mixer(x2, xg, ss, b, s, nsa_w_in[slot], nsa_q_norm[slot], nsa_k_norm[slot],
                                    nsa_cmp_pos[slot], nsa_cmp_w1[slot], nsa_cmp_w2[slot], nsa_w_o[slot],
                                    ffn_norm[layer, 1])
        else:
            x2, xg, ss = _hgrn_mixer(x2, xg, ss, b, s, hgrn_w_in[slot], hgrn_lb_logits, hgrn_o_norm[slot],
                                     hgrn_w_o[slot], layer, ffn_norm[layer, 1])
        if layer + 1 < depth:
            x2, xg, ss = _ffn(x2, xg, ss, ffn_w_gate, ffn_w_up, ffn_w_down, (layer, 1), ffn_norm[layer + 1, 0])
        else:
            x2 = _ffn(x2, xg, ss, ffn_w_gate, ffn_w_up, ffn_w_down, (layer, 1), None)
    return x2.reshape(b, s, d)
```

```python
import functools
import math

import jax
import jax.numpy as jnp
import numpy as np
from jax import lax
from jax.experimental import pallas as pl
from jax.experimental.pallas import tpu as pltpu

F32 = jnp.float32
BF16 = jnp.bfloat16

RMS_EPS = 1e-6
NEG_INF = -1e30
FORCED_SCORE = 1e30
FFN_RES_WEIGHT = 0.5
N_MIXERS = 2

NSA_HEADS = 32
NSA_GROUPS = 4
CMP_BLOCK = 32
CMP_STRIDE = 16
SLC_BLOCK = 64
SLC_TOPN = 16
_SLC_SHIFT = SLC_BLOCK.bit_length() - 1
assert 1 << _SLC_SHIFT == SLC_BLOCK
WINDOW = 512
ROPE_THETA = 10000.0

HGRN_HEADS = 32
HGRN_CHUNK = 64

LANES = 128
SOFTMAX_FLOOR = 1e-30
BOUND_SLACK = 1.02
VMEM_LIMIT_BYTES = 56 * 1024 * 1024

NT_DIMS = (((1,), (1,)), ((), ()))
TN_DIMS = (((0,), (0,)), ((), ()))


def _params(*sem):
    return pltpu.CompilerParams(dimension_semantics=sem, vmem_limit_bytes=VMEM_LIMIT_BYTES)


def _pick(n, prefs):
    for p in prefs:
        if n % p == 0:
            return p
    raise ValueError(f"no tile in {prefs} divides {n}")


def _fold_lanes(x):
    out = x[:, :LANES]
    for t in range(1, x.shape[1] // LANES):
        out = out + x[:, t * LANES:(t + 1) * LANES]
    return out


def _inv_rms(ss, width):
    total = jnp.sum(jnp.sum(ss, axis=0), axis=-1, keepdims=True)
    return lax.rsqrt(total * (1.0 / width) + RMS_EPS)


def _prenorm_kernel(x_ref, g_ref, xg_ref, ss_ref):
    x = x_ref[...]
    xg_ref[...] = (x * g_ref[...]).astype(xg_ref.dtype)
    ss_ref[0] = _fold_lanes(x * x)


def _prenorm(x, g):
    m, d = x.shape
    tr = _pick(m, (256, 128, 8))
    return pl.pallas_call(
        _prenorm_kernel,
        grid=(m // tr,),
        in_specs=[pl.BlockSpec((tr, d), lambda i: (i, 0)), pl.BlockSpec((1, d), lambda i: (0, 0))],
        out_specs=[pl.BlockSpec((tr, d), lambda i: (i, 0)), pl.BlockSpec((1, tr, LANES), lambda i: (0, i, 0))],
        out_shape=[jax.ShapeDtypeStruct((m, d), BF16), jax.ShapeDtypeStruct((1, m, LANES), F32)],
        compiler_params=_params("parallel"),
        name="prenorm",
    )(x, g.reshape(1, d))


def _weight_spec(w, lead, bn, first_tile=0):
    k = w.shape[-2]
    return pl.BlockSpec((None,) * len(lead) + (k, bn), lambda i, j: lead + (0, first_tile + j))


def _lhs_spec(bm, k, single_buffer):
    if single_buffer:
        return pl.BlockSpec((bm, k), lambda i, j: (i, 0), pipeline_mode=pl.Buffered(1))
    return pl.BlockSpec((bm, k), lambda i, j: (i, 0))


def _stat_spec(planes, bm):
    return pl.BlockSpec((planes, bm, LANES), lambda i, j: (0, i, 0))


def _gateup_kernel(xg_ref, ss_ref, wg_ref, wu_ref, *rest, width, cast_down):
    if cast_down:
        wd_ref, o_ref, wd_out_ref, r_ref = rest
        wd_out_ref[...] = wd_ref[...].astype(wd_out_ref.dtype)
    else:
        o_ref, r_ref = rest

    @pl.when(pl.program_id(1) == 0)
    def _():
        r_ref[...] = jnp.broadcast_to(_inv_rms(ss_ref[...], width), r_ref.shape)

    xg = xg_ref[...]
    r = jnp.concatenate([r_ref[...]] * (o_ref.shape[1] // LANES), axis=1)
    a = jnp.dot(xg, wg_ref[...].astype(BF16), preferred_element_type=F32) * r
    b = jnp.dot(xg, wu_ref[...].astype(BF16), preferred_element_type=F32) * r
    o_ref[...] = (a * jax.nn.sigmoid(a) * b).astype(o_ref.dtype)


def _gateup(xg, ss, wg, wu, wd, lead):
    m, k = xg.shape
    n = wg.shape[-1]
    bm = _pick(m, (2048, 1024, 512, 256, 128))
    bn = _pick(n, (256, 128))
    steps_j = n // bn
    steps = (m // bm) * steps_j
    slab = wd.shape[-2] // steps
    cast_down = slab * steps == wd.shape[-2] and slab % 16 == 0
    in_specs = [_lhs_spec(bm, k, True), _stat_spec(ss.shape[0], bm), _weight_spec(wg, lead, bn),
                _weight_spec(wu, lead, bn)]
    out_specs = pl.BlockSpec((bm, bn), lambda i, j: (i, j))
    out_shape = jax.ShapeDtypeStruct((m, n), BF16)
    args = (xg, ss, wg, wu)
    if cast_down:
        d_out = wd.shape[-1]
        in_specs.append(pl.BlockSpec((None,) * len(lead) + (slab, d_out), lambda i, j: lead + (i * steps_j + j, 0)))
        out_specs = [out_specs, pl.BlockSpec((slab, d_out), lambda i, j: (i * steps_j + j, 0))]
        out_shape = [out_shape, jax.ShapeDtypeStruct(wd.shape[-2:], BF16)]
        args = args + (wd,)
    out = pl.pallas_call(
        functools.partial(_gateup_kernel, width=k, cast_down=cast_down),
        grid=(m // bm, steps_j),
        in_specs=in_specs,
        out_specs=out_specs,
        out_shape=out_shape,
        scratch_shapes=[pltpu.VMEM((bm, LANES), F32)],
        compiler_params=_params("parallel", "arbitrary"),
        name="ffn_gateup",
    )(*args)
    return out if cast_down else (out, wd[lead].astype(BF16))


def _out_proj_kernel(a_ref, w_ref, r_ref, gn_ref, o_ref, xg_ref, ss_ref):
    x = r_ref[...] + jnp.dot(a_ref[...], w_ref[...], preferred_element_type=F32)
    o_ref[...] = x
    xg_ref[...] = (x * gn_ref[...]).astype(xg_ref.dtype)

    @pl.when(pl.program_id(1) == 0)
    def _():
        ss_ref[...] = jnp.zeros_like(ss_ref)

    ss_ref[0] += _fold_lanes(x * x)


def _out_proj(a, w, res, next_gain, *, name):
    m, k = a.shape
    n = w.shape[-1]
    bm = _pick(m, (1024, 512, 256, 128))
    bn = _pick(n, (512, 256, 128))
    tile = pl.BlockSpec((bm, bn), lambda i, j: (i, j))
    return pl.pallas_call(
        _out_proj_kernel,
        grid=(m // bm, n // bn),
        in_specs=[_lhs_spec(bm, k, False), _weight_spec(w, (), bn), tile, pl.BlockSpec((1, bn), lambda i, j: (0, j))],
        out_specs=[tile, tile, _stat_spec(1, bm)],
        out_shape=[jax.ShapeDtypeStruct((m, n), F32), jax.ShapeDtypeStruct((m, n), BF16),
                   jax.ShapeDtypeStruct((1, m, LANES), F32)],
        compiler_params=_params("parallel", "arbitrary"),
        name=name,
    )(a, w, res, next_gain.reshape(1, n))


def _mm_kernel(xg_ref, ss_ref, w_ref, o_ref, *, width, sigmoid):
    acc = jnp.dot(xg_ref[...], w_ref[...].astype(BF16), preferred_element_type=F32) * _inv_rms(ss_ref[...], width)
    if sigmoid:
        acc = jax.nn.sigmoid(acc)
    o_ref[...] = acc.astype(o_ref.dtype)


def _mm(xg, ss, w, *, col0=0, n=None, out_dtype, sigmoid=False, name):
    m, k = xg.shape
    n = w.shape[-1] if n is None else n
    bn = _pick(n, (512, 256, 128))
    assert col0 % bn == 0
    tall = n // bn >= 8
    bm = _pick(m, (2048, 1024, 512, 256, 128) if tall else (1024, 512, 256, 128))
    return pl.pallas_call(
        functools.partial(_mm_kernel, width=k, sigmoid=sigmoid),
        grid=(m // bm, n // bn),
        in_specs=[_lhs_spec(bm, k, tall), _stat_spec(ss.shape[0], bm), _weight_spec(w, (), bn, col0 // bn)],
        out_specs=pl.BlockSpec((bm, bn), lambda i, j: (i, j)),
        out_shape=jax.ShapeDtypeStruct((m, n), out_dtype),
        compiler_params=_params("parallel", "arbitrary"),
        name=name,
    )(xg, ss, w)


def _ffn_down_kernel(a_ref, w_ref, r_ref, *rest, scale, emit_norm):
    x = r_ref[...] + scale * jnp.dot(a_ref[...], w_ref[...], preferred_element_type=F32)
    if not emit_norm:
        (o_ref,) = rest
        o_ref[...] = x
        return
    gn_ref, o_ref, xg_ref, ss_ref = rest
    o_ref[...] = x
    xg_ref[...] = (x * gn_ref[...]).astype(xg_ref.dtype)
    ss_ref[...] = _fold_lanes(x * x)


def _ffn_down(act, w, res, scale, next_gain):
    m, k = act.shape
    n = w.shape[-1]
    bm = _pick(m, (512, 256, 128))
    bn = _pick(n, (1024, 512, 256, 128))
    tile = pl.BlockSpec((bm, bn), lambda j, i: (i, j))
    emit_norm = next_gain is not None
    in_specs = [pl.BlockSpec((bm, k), lambda j, i: (i, 0)),
                pl.BlockSpec((k, bn), lambda j, i: (0, j), pipeline_mode=pl.Buffered(1)),
                tile]
    out_specs, out_shape, args = tile, jax.ShapeDtypeStruct((m, n), F32), (act, w, res)
    if emit_norm:
        in_specs.append(pl.BlockSpec((1, bn), lambda j, i: (0, j)))
        out_specs = [tile, tile, pl.BlockSpec((None, bm, LANES), lambda j, i: (j, i, 0))]
        out_shape = [out_shape, jax.ShapeDtypeStruct((m, n), BF16), jax.ShapeDtypeStruct((n // bn, m, LANES), F32)]
        args = args + (next_gain.reshape(1, n),)
    return pl.pallas_call(
        functools.partial(_ffn_down_kernel, scale=scale, emit_norm=emit_norm),
        grid=(n // bn, m // bm),
        in_specs=in_specs,
        out_specs=out_specs,
        out_shape=out_shape,
        compiler_params=_params("parallel", "parallel"),
        name="ffn_down",
    )(*args)


def _ffn(x, xg, ss, w_gate, w_up, w_down, lead, next_gain):
    act, w_down_bf16 = _gateup(xg, ss, w_gate, w_up, w_down, lead)
    return _ffn_down(act, w_down_bf16, x, FFN_RES_WEIGHT, next_gain)


def _nsa_prep_kernel(p_ref, cos_ref, sin_ref, qn_ref, kn_ref, q_ref, kc_ref, vc_ref, ks_ref, vs_ref, kw_ref, vw_ref,
                     *, heads, groups):
    cos = cos_ref[...]
    sin = sin_ref[...]
    scale = LANES ** -0.5
    base = heads * LANES
    kv_w = groups * LANES
    ts = p_ref.shape[0]
    pos = pl.program_id(1) * ts + lax.broadcasted_iota(jnp.int32, (ts, LANES), 0)
    lane = lax.broadcasted_iota(jnp.int32, (ts, LANES), 1)
    blk_onehot = jnp.where(lane == jnp.right_shift(pos, _SLC_SHIFT), 1.0, 0.0).astype(ks_ref.dtype)
    lane0_onehot = jnp.where(lane == 0, 1.0, 0.0).astype(kw_ref.dtype)

    def store_q(h):
        def store(y):
            q_ref[:, h * LANES:(h + 1) * LANES] = y.astype(q_ref.dtype)
        return store

    def store_k(k_out, g, upper):
        def store(y):
            if upper is None:
                k_out[g] = y.astype(k_out.dtype)
            else:
                k_out[g, :, :LANES] = y.astype(k_out.dtype)
                k_out[g, :, LANES:] = upper
        return store

    jobs = [(h * LANES, qn_ref[...], scale, store_q(h)) for h in range(heads)]
    for br, (k_out, upper) in enumerate(((kc_ref, None), (ks_ref, blk_onehot), (kw_ref, lane0_onehot))):
        for g in range(groups):
            jobs.append((base + (2 * br) * kv_w + g * LANES, kn_ref[br:br + 1, :], None, store_k(k_out, g, upper)))
    inv = [lax.rsqrt(jnp.mean(jnp.square(p_ref[:, c:c + LANES]), axis=-1, keepdims=True) + RMS_EPS)
           for c, _, _, _ in jobs]
    for (c, gain, post, store), r in zip(jobs, inv):
        y = p_ref[:, c:c + LANES] * r * gain
        y = y * cos + pltpu.roll(y, LANES // 2, 1) * sin
        store(y if post is None else y * post)

    for br, v_out in enumerate((vc_ref, vs_ref, vw_ref)):
        for g in range(groups):
            cv = base + (2 * br + 1) * kv_w + g * LANES
            v = p_ref[:, cv:cv + LANES].astype(v_out.dtype)
            if br == 0:
                v_out[g] = v
            else:
                v_out[g, :, :LANES] = v
                v_out[g, :, LANES:] = jnp.ones_like(v)


def _nsa_prep(proj, cos, sin_signed, q_norm, k_norm, *, heads, groups):
    b, s, n = proj.shape
    ts = _pick(s, (256, 128))
    kv_spec = pl.BlockSpec((None, groups, ts, LANES), lambda bi, i: (bi, 0, i, 0))
    aug_spec = pl.BlockSpec((None, groups, ts, 2 * LANES), lambda bi, i: (bi, 0, i, 0))
    tab_spec = pl.BlockSpec((ts, LANES), lambda bi, i: (i, 0))

    def kv_shape(dt, width=LANES):
        return jax.ShapeDtypeStruct((b, groups, s, width), dt)

    return pl.pallas_call(
        functools.partial(_nsa_prep_kernel, heads=heads, groups=groups),
        grid=(b, s // ts),
        in_specs=[
            pl.BlockSpec((None, ts, n), lambda bi, i: (bi, i, 0)),
            tab_spec,
            tab_spec,
            pl.BlockSpec((1, LANES), lambda bi, i: (0, 0)),
            pl.BlockSpec((3, LANES), lambda bi, i: (0, 0)),
        ],
        out_specs=[pl.BlockSpec((None, ts, heads * LANES), lambda bi, i: (bi, i, 0)),
                   kv_spec, kv_spec, aug_spec, aug_spec, aug_spec, aug_spec],
        out_shape=[jax.ShapeDtypeStruct((b, s, heads * LANES), BF16),
                   kv_shape(F32), kv_shape(F32), kv_shape(BF16, 2 * LANES), kv_shape(BF16, 2 * LANES),
                   kv_shape(BF16, 2 * LANES), kv_shape(BF16, 2 * LANES)],
        compiler_params=_params("parallel", "parallel"),
        name="nsa_prep",
    )(proj, cos, sin_signed, q_norm.reshape(1, LANES), k_norm)


def _gelu_tanh(x):
    return x * (0.5 * (1.0 + jnp.tanh(math.sqrt(2.0 / math.pi) * (x + 0.044715 * (x * x * x)))))


def _compress_kernel(k_ref, v_ref, pe_ref, w1_ref, w2_ref, ko_ref, vo_ref):
    nseg = ko_ref.shape[0]
    for which, (x_ref, o_ref) in enumerate(((k_ref, ko_ref), (v_ref, vo_ref))):
        top = jnp.zeros((nseg, LANES), F32)
        bot = jnp.zeros((nseg, LANES), F32)
        for r in range(CMP_STRIDE):
            rows = x_ref[pl.ds(r, nseg, stride=CMP_STRIDE), :]
            top = top + jnp.dot((rows + pe_ref[which, r:r + 1, :]).astype(BF16), w1_ref[which, r],
                                preferred_element_type=F32)
            bot = bot + jnp.dot((rows + pe_ref[which, CMP_STRIDE + r:CMP_STRIDE + r + 1, :]).astype(BF16),
                                w1_ref[which, CMP_STRIDE + r], preferred_element_type=F32)
        pre = top + pltpu.roll(bot, nseg - 1, 0)
        hid = _gelu_tanh(pre).astype(BF16)
        o_ref[...] = jnp.dot(hid, w2_ref[which], preferred_element_type=F32).astype(o_ref.dtype)


def _nsa_compress(kc, vc, pe, w1, w2):
    b, g, s, _ = kc.shape
    nseg = s // CMP_STRIDE
    x_spec = pl.BlockSpec((None, None, s, LANES), lambda bi, gi: (bi, gi, 0, 0))
    o_spec = pl.BlockSpec((None, None, nseg, LANES), lambda bi, gi: (bi, gi, 0, 0))
    o_shape = jax.ShapeDtypeStruct((b, g, nseg, LANES), BF16)
    return pl.pallas_call(
        _compress_kernel,
        grid=(b, g),
        in_specs=[
            x_spec, x_spec,
            pl.BlockSpec((2, CMP_BLOCK, LANES), lambda bi, gi: (0, 0, 0)),
            pl.BlockSpec((2, CMP_BLOCK, LANES, LANES), lambda bi, gi: (0, 0, 0, 0)),
            pl.BlockSpec((2, LANES, LANES), lambda bi, gi: (0, 0, 0)),
        ],
        out_specs=[o_spec, o_spec],
        out_shape=[o_shape, o_shape],
        compiler_params=_params("parallel", "parallel"),
        name="nsa_compress",
    )(kc, vc, pe, w1, w2)


def _stack_heads(q, hpg):
    return jnp.concatenate([q[:, h * LANES:(h + 1) * LANES] for h in range(hpg)], axis=0)


def _cmp_attn_kernel(q_ref, kc_ref, vc_ref, g_ref, ovl_ref, o_ref, sel_ref, *, tq, hpg, n_cmp, n_slc, top_n):
    i = pl.program_id(2)
    q8 = _stack_heads(q_ref[...], hpg)
    rows = hpg * tq
    ncp = kc_ref.shape[0]
    s = lax.dot_general(q8, kc_ref[...], NT_DIMS, preferred_element_type=F32)
    tpos = i * tq + lax.broadcasted_iota(jnp.int32, (tq, 1), 0)
    tpos8 = jnp.concatenate([tpos] * hpg, axis=0)
    ncol = lax.broadcasted_iota(jnp.int32, (1, ncp), 1)
    valid = jnp.where((ncol * CMP_STRIDE + (CMP_BLOCK - 1) <= tpos8) & (ncol < n_cmp), 1.0, 0.0)
    sm = jnp.where(valid > 0.5, s, NEG_INF)
    m = jnp.max(sm, axis=-1, keepdims=True)
    e = jnp.exp(sm - m) * valid
    l = jnp.sum(e, axis=-1, keepdims=True)
    p = e / jnp.where(l > 0.0, l, 1.0)
    o = jnp.dot(p.astype(BF16), vc_ref[...], preferred_element_type=F32)
    gates = g_ref[...]
    for h in range(hpg):
        o_ref[:, h * LANES:(h + 1) * LANES] = o[h * tq:(h + 1) * tq] * gates[:, h:h + 1]

    psum = p[0:tq]
    for h in range(1, hpg):
        psum = psum + p[h * tq:(h + 1) * tq]
    hi = psum.astype(BF16)
    lo = (psum - hi.astype(F32)).astype(BF16)
    ovl = ovl_ref[...]
    imp = (lax.dot_general(ovl, hi, NT_DIMS, preferred_element_type=F32)
           + lax.dot_general(ovl, lo, NT_DIMS, preferred_element_type=F32))
    jidx = lax.broadcasted_iota(jnp.int32, (n_slc, tq), 0)
    tq_pos = i * tq + lax.broadcasted_iota(jnp.int32, (n_slc, tq), 1)
    cur = jnp.right_shift(tq_pos, _SLC_SHIFT)
    forced = (jidx == 0) | (jidx == cur) | (jidx == cur - 1)
    causal = jidx * SLC_BLOCK <= tq_pos
    imp = jnp.where(forced, FORCED_SCORE, jnp.where(causal, imp, NEG_INF))
    sel_rows = []
    for j in range(n_slc):
        row = imp[j:j + 1, :]
        lower = jnp.where(jidx < j, 1.0, 0.0)
        beats = jnp.where(imp > row, 1.0, jnp.where(imp == row, lower, 0.0))
        rank = jnp.sum(beats, axis=0, keepdims=True)
        sel_rows.append(jnp.where(rank < top_n, 0.0, NEG_INF))
    sel_rows.append(jnp.zeros((LANES - n_slc, tq), F32))
    bias_t = jnp.concatenate(sel_rows, axis=0)
    sel_ref[...] = bias_t.T.astype(sel_ref.dtype)


def _nsa_cmp_attn(q, kcmp, vcmp, gates, ovl, *, groups, hpg, n_cmp, n_slc, top_n):
    b, s, d = q.shape
    ncp = kcmp.shape[2]
    tq = _pick(s, (512, 256, 128))
    qo_spec = pl.BlockSpec((None, tq, hpg * LANES), lambda bi, gi, i: (bi, i, gi))
    c_spec = pl.BlockSpec((None, None, ncp, LANES), lambda bi, gi, i: (bi, gi, 0, 0))
    return pl.pallas_call(
        functools.partial(_cmp_attn_kernel, tq=tq, hpg=hpg, n_cmp=n_cmp, n_slc=n_slc, top_n=top_n),
        grid=(b, groups, s // tq),
        in_specs=[
            qo_spec, c_spec, c_spec,
            pl.BlockSpec((None, tq, LANES), lambda bi, gi, i: (bi, i, gi)),
            pl.BlockSpec((n_slc, ncp), lambda bi, gi, i: (0, 0)),
        ],
        out_specs=[qo_spec, pl.BlockSpec((None, None, tq, LANES), lambda bi, gi, i: (bi, gi, i, 0))],
        out_shape=[jax.ShapeDtypeStruct((b, s, d), F32), jax.ShapeDtypeStruct((b, groups, s, LANES), BF16)],
        compiler_params=_params("parallel", "parallel", "parallel"),
        name="nsa_cmp_attn",
    )(q, kcmp, vcmp, gates, ovl)


def _stack_aug(q, upper, hpg):
    return jnp.concatenate(
        [jnp.concatenate([q[:, h * LANES:(h + 1) * LANES], upper[h]], axis=1) for h in range(hpg)], axis=0)


def _score_bounds(q, k_gain, hpg):
    k_bound = BOUND_SLACK * math.sqrt(LANES) * jnp.max(jnp.abs(k_gain), axis=-1, keepdims=True)
    out = []
    for h in range(hpg):
        qh = q[:, h * LANES:(h + 1) * LANES].astype(F32)
        out.append(jnp.sqrt(jnp.sum(qh * qh, axis=-1, keepdims=True)) * k_bound)
    return out


def _gated_store(o_ref, oin_ref, gates, o, first_gate, tq, hpg):
    for h in range(hpg):
        cols = slice(h * LANES, (h + 1) * LANES)
        gate = gates[:, first_gate + h:first_gate + h + 1]
        o_ref[:, cols] = (oin_ref[:, cols] + o[h * tq:(h + 1) * tq] * gate).astype(o_ref.dtype)


def _sel_attn_kernel(q_ref, k_ref, v_ref, sel_ref, kn_ref, g_ref, oin_ref, o_ref, *, tq, tk, hpg):
    i = pl.program_id(2)
    q = q_ref[...]
    bias = sel_ref[...].astype(F32)
    rows = hpg * tq
    tpos = i * tq + lax.broadcasted_iota(jnp.int32, (tq, 1), 0)
    n_kv = ((i + 1) * tq + tk - 1) // tk
    gates = g_ref[...]

    def causal_bias(start):
        kpos = start + lax.broadcasted_iota(jnp.int32, (1, tk), 1)
        return jnp.concatenate([jnp.where(kpos <= tpos, 0.0, NEG_INF)] * hpg, axis=0)

    q8 = _stack_aug(q, [(bias - m).astype(BF16) for m in _score_bounds(q, kn_ref[...], hpg)], hpg)

    def fast_tile(kv, acc, diagonal):
        start = pl.multiple_of(kv * tk, tk)
        s = lax.dot_general(q8, k_ref[pl.ds(start, tk), :], NT_DIMS, preferred_element_type=F32)
        if diagonal:
            s = s + causal_bias(start)
        return acc + jnp.dot(jnp.exp(s).astype(BF16), v_ref[pl.ds(start, tk), :], preferred_element_type=F32)

    acc = lax.fori_loop(0, n_kv - 1, lambda kv, a: fast_tile(kv, a, False), jnp.zeros((rows, 2 * LANES), F32))
    acc = fast_tile(n_kv - 1, acc, True)
    denom = acc[:, LANES:]
    _gated_store(o_ref, oin_ref, gates, acc[:, :LANES] / denom, hpg, tq, hpg)

    @pl.when(jnp.logical_not(jnp.min(denom) >= SOFTMAX_FLOOR))
    def _():
        q8x = _stack_aug(q, [sel_ref[...]] * hpg, hpg)

        def exact_tile(kv, carry, diagonal):
            m, acc = carry
            start = pl.multiple_of(kv * tk, tk)
            s = lax.dot_general(q8x, k_ref[pl.ds(start, tk), :], NT_DIMS, preferred_element_type=F32)
            if diagonal:
                s = s + causal_bias(start)
            m_new = jnp.maximum(m, jnp.max(s, axis=-1, keepdims=True))
            p = jnp.exp(s - m_new).astype(BF16)
            acc = jnp.exp(m - m_new) * acc + jnp.dot(p, v_ref[pl.ds(start, tk), :], preferred_element_type=F32)
            return m_new, acc

        init = (jnp.full((rows, 1), NEG_INF, F32), jnp.zeros((rows, 2 * LANES), F32))
        carry = lax.fori_loop(0, n_kv - 1, lambda kv, c: exact_tile(kv, c, False), init)
        _, acc_x = exact_tile(n_kv - 1, carry, True)
        _gated_store(o_ref, oin_ref, gates, acc_x[:, :LANES] / acc_x[:, LANES:], hpg, tq, hpg)


def _nsa_sel_attn(q, ks, vs, selb, k_gain, gates, o_in, *, groups, hpg):
    b, s, d = q.shape
    tq = _pick(s, (512, 256, 128))
    tk = _pick(s, (512, 256, 128))
    qo_spec = pl.BlockSpec((None, tq, hpg * LANES), lambda bi, gi, i: (bi, i, gi))
    kv_spec = pl.BlockSpec((None, None, s, 2 * LANES), lambda bi, gi, i: (bi, gi, 0, 0))
    return pl.pallas_call(
        functools.partial(_sel_attn_kernel, tq=tq, tk=tk, hpg=hpg),
        grid=(b, groups, s // tq),
        in_specs=[
            qo_spec, kv_spec, kv_spec,
            pl.BlockSpec((None, None, tq, LANES), lambda bi, gi, i: (bi, gi, i, 0)),
            pl.BlockSpec((1, LANES), lambda bi, gi, i: (0, 0)),
            pl.BlockSpec((None, tq, LANES), lambda bi, gi, i: (bi, i, gi)),
            qo_spec,
        ],
        out_specs=qo_spec,
        out_shape=jax.ShapeDtypeStruct((b, s, d), F32),
        compiler_params=_params("parallel", "parallel", "parallel"),
        name="nsa_sel_attn",
    )(q, ks, vs, selb, k_gain.reshape(1, LANES), gates, o_in)


def _win_attn_kernel(q_ref, k_ref, v_ref, kn_ref, g_ref, oin_ref, o_ref, *, tq, span, hpg):
    i = pl.program_id(2)
    q = q_ref[...]
    start = pl.multiple_of(jnp.maximum(i * tq - WINDOW, 0), tq)
    k = k_ref[pl.ds(start, span), :]
    v = v_ref[pl.ds(start, span), :]
    tpos = i * tq + lax.broadcasted_iota(jnp.int32, (tq, 1), 0)
    kpos = start + lax.broadcasted_iota(jnp.int32, (1, span), 1)
    diff = tpos - kpos
    bias = jnp.concatenate([jnp.where((diff >= 0) & (diff < WINDOW), 0.0, NEG_INF)] * hpg, axis=0)
    gates = g_ref[...]

    q8 = _stack_aug(q, [jnp.broadcast_to(-m, (tq, LANES)).astype(BF16) for m in _score_bounds(q, kn_ref[...], hpg)],
                    hpg)
    s = lax.dot_general(q8, k, NT_DIMS, preferred_element_type=F32) + bias
    acc = jnp.dot(jnp.exp(s).astype(BF16), v, preferred_element_type=F32)
    denom = acc[:, LANES:]
    _gated_store(o_ref, oin_ref, gates, acc[:, :LANES] / denom, 2 * hpg, tq, hpg)

    @pl.when(jnp.logical_not(jnp.min(denom) >= SOFTMAX_FLOOR))
    def _():
        q8x = _stack_aug(q, [jnp.zeros((tq, LANES), BF16)] * hpg, hpg)
        sx = lax.dot_general(q8x, k, NT_DIMS, preferred_element_type=F32) + bias
        p = jnp.exp(sx - jnp.max(sx, axis=-1, keepdims=True))
        acc_x = jnp.dot(p.astype(BF16), v, preferred_element_type=F32)
        _gated_store(o_ref, oin_ref, gates, acc_x[:, :LANES] / acc_x[:, LANES:], 2 * hpg, tq, hpg)


def _nsa_win_attn(q, kw, vw, k_gain, gates, o_in, *, groups, hpg):
    b, s, d = q.shape
    tq = _pick(s, (256, 128))
    span = WINDOW + tq
    assert s >= span and WINDOW % tq == 0
    qo_spec = pl.BlockSpec((None, tq, hpg * LANES), lambda bi, gi, i: (bi, i, gi))
    kv_spec = pl.BlockSpec((None, None, s, 2 * LANES), lambda bi, gi, i: (bi, gi, 0, 0))
    return pl.pallas_call(
        functools.partial(_win_attn_kernel, tq=tq, span=span, hpg=hpg),
        grid=(b, groups, s // tq),
        in_specs=[qo_spec, kv_spec, kv_spec, pl.BlockSpec((1, LANES), lambda bi, gi, i: (0, 0)),
                  pl.BlockSpec((None, tq, LANES), lambda bi, gi, i: (bi, i, gi)), qo_spec],
        out_specs=qo_spec,
        out_shape=jax.ShapeDtypeStruct((b, s, d), BF16),
        compiler_params=_params("parallel", "parallel", "parallel"),
        name="nsa_win_attn",
    )(q, kw, vw, k_gain.reshape(1, LANES), gates, o_in)


def _rope_tables(seq, dim):
    inv = ROPE_THETA ** (-jnp.arange(0, dim, 2, dtype=F32) / dim)
    ang = jnp.arange(seq, dtype=F32)[:, None] * inv[None, :]
    ang = jnp.concatenate([ang, ang], axis=-1)
    sign = jnp.where(jnp.arange(dim) < dim // 2, -1.0, 1.0).astype(F32)
    return jnp.cos(ang), jnp.sin(ang) * sign


def _nsa_mixer(x2, xg, ss, b, s, w_in, q_norm, k_norm, cmp_pos, cmp_w1, cmp_w2, w_o, next_gain):
    heads, groups = NSA_HEADS, NSA_GROUPS
    hpg = heads // groups
    d = x2.shape[1]
    assert d == heads * LANES and CMP_BLOCK == 2 * CMP_STRIDE and 3 * hpg <= LANES
    qd, kvd = heads * LANES, groups * LANES
    n_main = qd + 6 * kvd
    nseg = s // CMP_STRIDE
    n_cmp = (s - CMP_BLOCK) // CMP_STRIDE + 1
    n_slc = s // SLC_BLOCK
    top_n = min(SLC_TOPN, n_slc)
    assert n_cmp == nseg - 1 and n_slc % 8 == 0 and n_slc <= LANES and top_n >= 3

    proj = _mm(xg, ss, w_in, n=n_main, out_dtype=F32, name="nsa_in_proj")
    gates = _mm(xg, ss, w_in, col0=n_main, n=LANES, out_dtype=F32, sigmoid=True, name="nsa_gate_proj")
    gates = gates[:, :3 * heads].reshape(b * s, 3, groups, hpg).transpose(0, 2, 1, 3).reshape(b * s, groups, 3 * hpg)
    gates = jnp.pad(gates, ((0, 0), (0, 0), (0, LANES - 3 * hpg))).reshape(b, s, groups * LANES)

    cos, sin_signed = _rope_tables(s, LANES)
    q, kc, vc, ks, vs, kw, vw = _nsa_prep(proj.reshape(b, s, n_main), cos, sin_signed, q_norm, k_norm,
                                          heads=heads, groups=groups)

    w1 = cmp_w1.reshape(2, CMP_BLOCK, LANES, LANES).astype(BF16)
    kcmp, vcmp = _nsa_compress(kc, vc, cmp_pos, w1, cmp_w2.astype(BF16))

    cmp_start = np.arange(nseg) * CMP_STRIDE
    slc_start = np.arange(n_slc) * SLC_BLOCK
    ovl = ((cmp_start[None, :] < slc_start[:, None] + SLC_BLOCK) & (cmp_start[None, :] + CMP_BLOCK > slc_start[:, None])
           & (np.arange(nseg)[None, :] < n_cmp))
    ovl = jnp.asarray(ovl.astype(np.float32), dtype=BF16)

    o1, selm = _nsa_cmp_attn(q, kcmp, vcmp, gates, ovl, groups=groups, hpg=hpg, n_cmp=n_cmp, n_slc=n_slc, top_n=top_n)
    o2 = _nsa_sel_attn(q, ks, vs, selm, k_norm[1], gates, o1, groups=groups, hpg=hpg)
    o3 = _nsa_win_attn(q, kw, vw, k_norm[2], gates, o2, groups=groups, hpg=hpg)
    return _out_proj(o3.reshape(b * s, d), w_o.astype(BF16), x2, next_gain, name="nsa_out_proj")


def _chunk_cumsum(x, tri):
    w = x.shape[1]
    h1 = x.astype(BF16)
    r1 = x - h1.astype(F32)
    h2 = r1.astype(BF16)
    h3 = (r1 - h2.astype(F32)).astype(BF16)
    g3 = jnp.dot(tri, jnp.concatenate([h1, h2, h3], axis=1), preferred_element_type=F32)
    return g3[:, :w] + g3[:, w:2 * w] + g3[:, 2 * w:]


def _hgrn_kernel(q_ref, f_ref, i_ref, g_ref, lbl_ref, on_ref, o_ref, *, seq, chunk, layer, hps, unroll):
    lbl = lbl_ref[...]
    e = jnp.exp(lbl - jnp.max(lbl, axis=0, keepdims=True))
    p = e / jnp.sum(e, axis=0, keepdims=True)
    csum = p[0:1]
    for d in range(1, layer + 1):
        csum = csum + p[d:d + 1]
    lb = csum - p[0:1]
    o_gain = on_ref[...]
    causal = (lax.broadcasted_iota(jnp.int32, (chunk, chunk), 0) >= lax.broadcasted_iota(jnp.int32, (chunk, chunk), 1))
    tri = jnp.where(causal, 1.0, 0.0).astype(BF16)
    heads = [slice(k * LANES, (k + 1) * LANES) for k in range(hps)]

    def body(it, states):
        states = list(states)
        chunks = []
        for u in range(unroll):
            rows = pl.ds(pl.multiple_of((it * unroll + u) * chunk, chunk), chunk)
            f = lb + (1.0 - lb) * jax.nn.sigmoid(f_ref[rows, :])
            kc = 1.0 - f
            g_cum = _chunk_cumsum(jnp.log(f), tri)
            g_last = g_cum[chunk - 1:chunk, :]
            q_dec = (q_ref[rows, :] * jnp.exp(g_cum)).astype(BF16)
            k_inv = (kc * jnp.exp(-g_cum)).astype(BF16)
            k_tail = (kc * jnp.exp(g_last - g_cum)).astype(BF16)
            chunks.append((rows, q_dec, k_inv, k_tail, i_ref[rows, :].astype(BF16), jnp.exp(g_last)))
        a_all = [[lax.dot_general(q_dec[:, h], k_inv[:, h], NT_DIMS, preferred_element_type=F32) for h in heads]
                 for (_, q_dec, k_inv, _, _, _) in chunks]
        ds_all = [[lax.dot_general(v[:, h], k_tail[:, h], TN_DIMS, preferred_element_type=F32) for h in heads]
                  for (_, _, _, k_tail, v, _) in chunks]
        o_all = [[jnp.dot(jnp.where(causal, a, 0.0).astype(BF16), v[:, h], preferred_element_type=F32)
                  for a, h in zip(a_row, heads)]
                 for a_row, (_, _, _, _, v, _) in zip(a_all, chunks)]
        for u, (rows, q_dec, _, _, _, decay) in enumerate(chunks):
            for k, h in enumerate(heads):
                o = o_all[u][k] + lax.dot_general(q_dec[:, h], states[k].astype(BF16), NT_DIMS,
                                                  preferred_element_type=F32)
                states[k] = states[k] * decay[:, h] + ds_all[u][k]
                y = o * lax.rsqrt(jnp.mean(o * o, axis=-1, keepdims=True) + RMS_EPS) * o_gain
                gz = g_ref[rows, h]
                o_ref[rows, h] = (y * (gz * jax.nn.sigmoid(gz))).astype(o_ref.dtype)
        return tuple(states)

    init = tuple(jnp.zeros((LANES, LANES), F32) for _ in range(hps))
    lax.fori_loop(0, seq // (chunk * unroll), body, init)


def _hgrn_core(proj, lb_logits, o_norm, *, heads, layer, hps=4, unroll=8):
    b, s, _ = proj.shape
    depth = lb_logits.shape[0]
    hps = math.gcd(hps, heads)
    width = hps * LANES
    groups = heads // hps
    assert s % (HGRN_CHUNK * unroll) == 0

    def col_spec(part):
        return pl.BlockSpec((None, s, width), lambda bi, hi: (bi, 0, part * groups + hi))

    return pl.pallas_call(
        functools.partial(_hgrn_kernel, seq=s, chunk=HGRN_CHUNK, layer=layer, hps=hps, unroll=unroll),
        grid=(b, groups),
        in_specs=[col_spec(0), col_spec(1), col_spec(2), col_spec(3),
                  pl.BlockSpec((depth, width), lambda bi, hi: (0, hi)),
                  pl.BlockSpec((1, LANES), lambda bi, hi: (0, 0))],
        out_specs=pl.BlockSpec((None, s, width), lambda bi, hi: (bi, 0, hi)),
        out_shape=jax.ShapeDtypeStruct((b, s, heads * LANES), BF16),
        compiler_params=_params("parallel", "parallel"),
        name="hgrn_core",
    )(proj, proj, proj, proj, lb_logits, o_norm.reshape(1, LANES))


def _hgrn_mixer(x2, xg, ss, b, s, w_in, lb_logits, o_norm, w_o, layer, next_gain):
    heads = HGRN_HEADS
    d = x2.shape[1]
    assert d == heads * LANES and w_in.shape[1] == 4 * d and s % HGRN_CHUNK == 0
    proj = _mm(xg, ss, w_in, out_dtype=F32, name="hgrn_in_proj")
    o = _hgrn_core(proj.reshape(b, s, 4 * d), lb_logits, o_norm, heads=heads, layer=layer)
    return _out_proj(o.reshape(b * s, d), w_o.astype(BF16), x2, next_gain, name="hgrn_out_proj")


def kernel(x, ffn_norm, ffn_w_gate, ffn_w_up, ffn_w_down, mix_norm, nsa_w_in, nsa_q_norm, nsa_k_norm, nsa_cmp_pos,
           nsa_cmp_w1, nsa_cmp_w2, nsa_w_o, hgrn_w_in, hgrn_lb_logits, hgrn_o_norm, hgrn_w_o):
    b, s, d = x.shape
    depth = ffn_norm.shape[0]
    x2 = x.reshape(b * s, d)
    xg, ss = _prenorm(x2, ffn_norm[0, 0])
    for layer in range(depth):
        slot = layer // N_MIXERS
        x2, xg, ss = _ffn(x2, xg, ss, ffn_w_gate, ffn_w_up, ffn_w_down, (layer, 0), mix_norm[layer])
        if layer % N_MIXERS == 0:
            x2, xg, ss = _nsa_mixer(x2, xg, ss, b, s, nsa_w_in[slot], nsa_q_norm[slot], nsa_k_norm[slot],
                                    nsa_cmp_pos[slot], nsa_cmp_w1[slot], nsa_cmp_w2[slot], nsa_w_o[slot],
                                    ffn_norm[layer, 1])
        else:
            x2, xg, ss = _hgrn_mixer(x2, xg, ss, b, s, hgrn_w_in[slot], hgrn_lb_logits, hgrn_o_norm[slot],
                                     hgrn_w_o[slot], layer, ffn_norm[layer, 1])
        if layer + 1 < depth:
            x2, xg, ss = _ffn(x2, xg, ss, ffn_w_gate, ffn_w_up, ffn_w_down, (layer, 1), ffn_norm[layer + 1, 0])
        else:
            x2 = _ffn(x2, xg, ss, ffn_w_gate, ffn_w_up, ffn_w_down, (layer, 1), None)
    return x2.reshape(b, s, d)
```

```python
import functools
import math

import jax
import jax.numpy as jnp
import numpy as np
from jax import lax
from jax.experimental import pallas as pl
from jax.experimental.pallas import tpu as pltpu

F32 = jnp.float32
BF16 = jnp.bfloat16

RMS_EPS = 1e-6
NEG_INF = -1e30
FORCED_SCORE = 1e30
FFN_RES_WEIGHT = 0.5
N_MIXERS = 2

NSA_HEADS = 32
NSA_GROUPS = 4
CMP_BLOCK = 32
CMP_STRIDE = 16
SLC_BLOCK = 64
SLC_TOPN = 16
_SLC_SHIFT = SLC_BLOCK.bit_length() - 1
assert 1 << _SLC_SHIFT == SLC_BLOCK
WINDOW = 512
ROPE_THETA = 10000.0

HGRN_HEADS = 32
HGRN_CHUNK = 64

LANES = 128
SOFTMAX_FLOOR = 1e-30
BOUND_SLACK = 1.02
VMEM_LIMIT_BYTES = 56 * 1024 * 1024

NT_DIMS = (((1,), (1,)), ((), ()))
TN_DIMS = (((0,), (0,)), ((), ()))


def _params(*sem):
    return pltpu.CompilerParams(dimension_semantics=sem, vmem_limit_bytes=VMEM_LIMIT_BYTES)


def _pick(n, prefs):
    for p in prefs:
        if n % p == 0:
            return p
    raise ValueError(f"no tile in {prefs} divides {n}")


def _fold_lanes(x):
    out = x[:, :LANES]
    for t in range(1, x.shape[1] // LANES):
        out = out + x[:, t * LANES:(t + 1) * LANES]
    return out


def _inv_rms(ss, width):
    total = jnp.sum(jnp.sum(ss, axis=0), axis=-1, keepdims=True)
    return lax.rsqrt(total * (1.0 / width) + RMS_EPS)


def _prenorm_kernel(x_ref, g_ref, xg_ref, ss_ref):
    x = x_ref[...]
    xg_ref[...] = (x * g_ref[...]).astype(xg_ref.dtype)
    ss_ref[0] = _fold_lanes(x * x)


def _prenorm(x, g):
    m, d = x.shape
    tr = _pick(m, (256, 128, 8))
    return pl.pallas_call(
        _prenorm_kernel,
        grid=(m // tr,),
        in_specs=[pl.BlockSpec((tr, d), lambda i: (i, 0)), pl.BlockSpec((1, d), lambda i: (0, 0))],
        out_specs=[pl.BlockSpec((tr, d), lambda i: (i, 0)), pl.BlockSpec((1, tr, LANES), lambda i: (0, i, 0))],
        out_shape=[jax.ShapeDtypeStruct((m, d), BF16), jax.ShapeDtypeStruct((1, m, LANES), F32)],
        compiler_params=_params("parallel"),
        name="prenorm",
    )(x, g.reshape(1, d))


def _weight_spec(w, lead, bn, first_tile=0):
    k = w.shape[-2]
    return pl.BlockSpec((None,) * len(lead) + (k, bn), lambda i, j: lead + (0, first_tile + j))


def _lhs_spec(bm, k, single_buffer):
    if single_buffer:
        return pl.BlockSpec((bm, k), lambda i, j: (i, 0), pipeline_mode=pl.Buffered(1))
    return pl.BlockSpec((bm, k), lambda i, j: (i, 0))


def _stat_spec(planes, bm):
    return pl.BlockSpec((planes, bm, LANES), lambda i, j: (0, i, 0))


def _gateup_kernel(xg_ref, ss_ref, wg_ref, wu_ref, *rest, width, cast_down):
    if cast_down:
        wd_ref, o_ref, wd_out_ref, r_ref = rest
        wd_out_ref[...] = wd_ref[...].astype(wd_out_ref.dtype)
    else:
        o_ref, r_ref = rest

    @pl.when(pl.program_id(1) == 0)
    def _():
        r_ref[...] = jnp.broadcast_to(_inv_rms(ss_ref[...], width), r_ref.shape)

    xg = xg_ref[...]
    r = jnp.concatenate([r_ref[...]] * (o_ref.shape[1] // LANES), axis=1)
    a = jnp.dot(xg, wg_ref[...].astype(BF16), preferred_element_type=F32) * r
    b = jnp.dot(xg, wu_ref[...].astype(BF16), preferred_element_type=F32) * r
    o_ref[...] = (a * jax.nn.sigmoid(a) * b).astype(o_ref.dtype)


def _gateup(xg, ss, wg, wu, wd, lead):
    m, k = xg.shape
    n = wg.shape[-1]
    bm = _pick(m, (2048, 1024, 512, 256, 128))
    bn = _pick(n, (256, 128))
    steps_j = n // bn
    steps = (m // bm) * steps_j
    slab = wd.shape[-2] // steps
    cast_down = slab * steps == wd.shape[-2] and slab % 16 == 0
    if lead == (0, 1):
        wg = wg[lead].astype(BF16).reshape(k, n // bn, bn).transpose(1, 0, 2)
        wu = wu[lead].astype(BF16).reshape(k, n // bn, bn).transpose(1, 0, 2)
        tiled = pl.BlockSpec((None, k, bn), lambda i, j: (j, 0, 0))
        in_specs = [_lhs_spec(bm, k, True), _stat_spec(ss.shape[0], bm), tiled, tiled]
    else:
        in_specs = [_lhs_spec(bm, k, True), _stat_spec(ss.shape[0], bm), _weight_spec(wg, lead, bn),
                    _weight_spec(wu, lead, bn)]
    out_specs = pl.BlockSpec((bm, bn), lambda i, j: (i, j))
    out_shape = jax.ShapeDtypeStruct((m, n), BF16)
    args = (xg, ss, wg, wu)
    if cast_down:
        d_out = wd.shape[-1]
        in_specs.append(pl.BlockSpec((None,) * len(lead) + (slab, d_out), lambda i, j: lead + (i * steps_j + j, 0)))
        out_specs = [out_specs, pl.BlockSpec((slab, d_out), lambda i, j: (i * steps_j + j, 0))]
        out_shape = [out_shape, jax.ShapeDtypeStruct(wd.shape[-2:], BF16)]
        args = args + (wd,)
    out = pl.pallas_call(
        functools.partial(_gateup_kernel, width=k, cast_down=cast_down),
        grid=(m // bm, steps_j),
        in_specs=in_specs,
        out_specs=out_specs,
        out_shape=out_shape,
        scratch_shapes=[pltpu.VMEM((bm, LANES), F32)],
        compiler_params=_params("parallel", "arbitrary"),
        name="ffn_gateup",
    )(*args)
    return out if cast_down else (out, wd[lead].astype(BF16))


def _out_proj_kernel(a_ref, w_ref, r_ref, gn_ref, o_ref, xg_ref, ss_ref):
    x = r_ref[...] + jnp.dot(a_ref[...], w_ref[...], preferred_element_type=F32)
    o_ref[...] = x
    xg_ref[...] = (x * gn_ref[...]).astype(xg_ref.dtype)

    @pl.when(pl.program_id(1) == 0)
    def _():
        ss_ref[...] = jnp.zeros_like(ss_ref)

    ss_ref[0] += _fold_lanes(x * x)


def _out_proj(a, w, res, next_gain, *, name):
    m, k = a.shape
    n = w.shape[-1]
    bm = _pick(m, (1024, 512, 256, 128))
    bn = _pick(n, (512, 256, 128))
    tile = pl.BlockSpec((bm, bn), lambda i, j: (i, j))
    return pl.pallas_call(
        _out_proj_kernel,
        grid=(m // bm, n // bn),
        in_specs=[_lhs_spec(bm, k, False), _weight_spec(w, (), bn), tile, pl.BlockSpec((1, bn), lambda i, j: (0, j))],
        out_specs=[tile, tile, _stat_spec(1, bm)],
        out_shape=[jax.ShapeDtypeStruct((m, n), F32), jax.ShapeDtypeStruct((m, n), BF16),
                   jax.ShapeDtypeStruct((1, m, LANES), F32)],
        compiler_params=_params("parallel", "arbitrary"),
        name=name,
    )(a, w, res, next_gain.reshape(1, n))


def _mm_kernel(xg_ref, ss_ref, w_ref, o_ref, *, width, sigmoid):
    acc = jnp.dot(xg_ref[...], w_ref[...].astype(BF16), preferred_element_type=F32) * _inv_rms(ss_ref[...], width)
    if sigmoid:
        acc = jax.nn.sigmoid(acc)
    o_ref[...] = acc.astype(o_ref.dtype)


def _mm(xg, ss, w, *, col0=0, n=None, out_dtype, sigmoid=False, name):
    m, k = xg.shape
    n = w.shape[-1] if n is None else n
    bn = _pick(n, (512, 256, 128))
    assert col0 % bn == 0
    tall = n // bn >= 8
    bm = _pick(m, (2048, 1024, 512, 256, 128) if tall else (1024, 512, 256, 128))
    return pl.pallas_call(
        functools.partial(_mm_kernel, width=k, sigmoid=sigmoid),
        grid=(m // bm, n // bn),
        in_specs=[_lhs_spec(bm, k, tall), _stat_spec(ss.shape[0], bm), _weight_spec(w, (), bn, col0 // bn)],
        out_specs=pl.BlockSpec((bm, bn), lambda i, j: (i, j)),
        out_shape=jax.ShapeDtypeStruct((m, n), out_dtype),
        compiler_params=_params("parallel", "arbitrary"),
        name=name,
    )(xg, ss, w)


def _ffn_down_kernel(a_ref, w_ref, r_ref, *rest, scale, emit_norm):
    x = r_ref[...] + scale * jnp.dot(a_ref[...], w_ref[...], preferred_element_type=F32)
    if not emit_norm:
        (o_ref,) = rest
        o_ref[...] = x
        return
    gn_ref, o_ref, xg_ref, ss_ref = rest
    o_ref[...] = x
    xg_ref[...] = (x * gn_ref[...]).astype(xg_ref.dtype)
    ss_ref[...] = _fold_lanes(x * x)


def _ffn_down(act, w, res, scale, next_gain):
    m, k = act.shape
    n = w.shape[-1]
    bm = _pick(m, (512, 256, 128))
    bn = _pick(n, (1024, 512, 256, 128))
    tile = pl.BlockSpec((bm, bn), lambda j, i: (i, j))
    emit_norm = next_gain is not None
    in_specs = [pl.BlockSpec((bm, k), lambda j, i: (i, 0)),
                pl.BlockSpec((k, bn), lambda j, i: (0, j), pipeline_mode=pl.Buffered(1)),
                tile]
    out_specs, out_shape, args = tile, jax.ShapeDtypeStruct((m, n), F32), (act, w, res)
    if emit_norm:
        in_specs.append(pl.BlockSpec((1, bn), lambda j, i: (0, j)))
        out_specs = [tile, tile, pl.BlockSpec((None, bm, LANES), lambda j, i: (j, i, 0))]
        out_shape = [out_shape, jax.ShapeDtypeStruct((m, n), BF16), jax.ShapeDtypeStruct((n // bn, m, LANES), F32)]
        args = args + (next_gain.reshape(1, n),)
    return pl.pallas_call(
        functools.partial(_ffn_down_kernel, scale=scale, emit_norm=emit_norm),
        grid=(n // bn, m // bm),
        in_specs=in_specs,
        out_specs=out_specs,
        out_shape=out_shape,
        compiler_params=_params("parallel", "parallel"),
        name="ffn_down",
    )(*args)


def _ffn(x, xg, ss, w_gate, w_up, w_down, lead, next_gain):
    act, w_down_bf16 = _gateup(xg, ss, w_gate, w_up, w_down, lead)
    return _ffn_down(act, w_down_bf16, x, FFN_RES_WEIGHT, next_gain)


def _nsa_prep_kernel(p_ref, cos_ref, sin_ref, qn_ref, kn_ref, q_ref, kc_ref, vc_ref, ks_ref, vs_ref, kw_ref, vw_ref,
                     *, heads, groups):
    cos = cos_ref[...]
    sin = sin_ref[...]
    scale = LANES ** -0.5
    base = heads * LANES
    kv_w = groups * LANES
    ts = p_ref.shape[0]
    pos = pl.program_id(1) * ts + lax.broadcasted_iota(jnp.int32, (ts, LANES), 0)
    lane = lax.broadcasted_iota(jnp.int32, (ts, LANES), 1)
    blk_onehot = jnp.where(lane == jnp.right_shift(pos, _SLC_SHIFT), 1.0, 0.0).astype(ks_ref.dtype)
    lane0_onehot = jnp.where(lane == 0, 1.0, 0.0).astype(kw_ref.dtype)

    def store_q(h):
        def store(y):
            q_ref[:, h * LANES:(h + 1) * LANES] = y.astype(q_ref.dtype)
        return store

    def store_k(k_out, g, upper):
        def store(y):
            if upper is None:
                k_out[g] = y.astype(k_out.dtype)
            else:
                k_out[g, :, :LANES] = y.astype(k_out.dtype)
                k_out[g, :, LANES:] = upper
        return store

    jobs = [(h * LANES, qn_ref[...], scale, store_q(h)) for h in range(heads)]
    for br, (k_out, upper) in enumerate(((kc_ref, None), (ks_ref, blk_onehot), (kw_ref, lane0_onehot))):
        for g in range(groups):
            jobs.append((base + (2 * br) * kv_w + g * LANES, kn_ref[br:br + 1, :], None, store_k(k_out, g, upper)))
    inv = [lax.rsqrt(jnp.mean(jnp.square(p_ref[:, c:c + LANES]), axis=-1, keepdims=True) + RMS_EPS)
           for c, _, _, _ in jobs]
    for (c, gain, post, store), r in zip(jobs, inv):
        y = p_ref[:, c:c + LANES] * r * gain
        y = y * cos + pltpu.roll(y, LANES // 2, 1) * sin
        store(y if post is None else y * post)

    for br, v_out in enumerate((vc_ref, vs_ref, vw_ref)):
        for g in range(groups):
            cv = base + (2 * br + 1) * kv_w + g * LANES
            v = p_ref[:, cv:cv + LANES].astype(v_out.dtype)
            if br == 0:
                v_out[g] = v
            else:
                v_out[g, :, :LANES] = v
                v_out[g, :, LANES:] = jnp.ones_like(v)


def _nsa_prep(proj, cos, sin_signed, q_norm, k_norm, *, heads, groups):
    b, s, n = proj.shape
    ts = _pick(s, (256, 128))
    kv_spec = pl.BlockSpec((None, groups, ts, LANES), lambda bi, i: (bi, 0, i, 0))
    aug_spec = pl.BlockSpec((None, groups, ts, 2 * LANES), lambda bi, i: (bi, 0, i, 0))
    tab_spec = pl.BlockSpec((ts, LANES), lambda bi, i: (i, 0))

    def kv_shape(dt, width=LANES):
        return jax.ShapeDtypeStruct((b, groups, s, width), dt)

    return pl.pallas_call(
        functools.partial(_nsa_prep_kernel, heads=heads, groups=groups),
        grid=(b, s // ts),
        in_specs=[
            pl.BlockSpec((None, ts, n), lambda bi, i: (bi, i, 0)),
            tab_spec,
            tab_spec,
            pl.BlockSpec((1, LANES), lambda bi, i: (0, 0)),
            pl.BlockSpec((3, LANES), lambda bi, i: (0, 0)),
        ],
        out_specs=[pl.BlockSpec((None, ts, heads * LANES), lambda bi, i: (bi, i, 0)),
                   kv_spec, kv_spec, aug_spec, aug_spec, aug_spec, aug_spec],
        out_shape=[jax.ShapeDtypeStruct((b, s, heads * LANES), BF16),
                   kv_shape(F32), kv_shape(F32), kv_shape(BF16, 2 * LANES), kv_shape(BF16, 2 * LANES),
                   kv_shape(BF16, 2 * LANES), kv_shape(BF16, 2 * LANES)],
        compiler_params=_params("parallel", "parallel"),
        name="nsa_prep",
    )(proj, cos, sin_signed, q_norm.reshape(1, LANES), k_norm)


def _gelu_tanh(x):
    return x * (0.5 * (1.0 + jnp.tanh(math.sqrt(2.0 / math.pi) * (x + 0.044715 * (x * x * x)))))


def _compress_kernel(k_ref, v_ref, pe_ref, w1_ref, w2_ref, ko_ref, vo_ref):
    nseg = ko_ref.shape[0]
    for which, (x_ref, o_ref) in enumerate(((k_ref, ko_ref), (v_ref, vo_ref))):
        top = jnp.zeros((nseg, LANES), F32)
        bot = jnp.zeros((nseg, LANES), F32)
        for r in range(CMP_STRIDE):
            rows = x_ref[pl.ds(r, nseg, stride=CMP_STRIDE), :]
            top = top + jnp.dot((rows + pe_ref[which, r:r + 1, :]).astype(BF16), w1_ref[which, r],
                                preferred_element_type=F32)
            bot = bot + jnp.dot((rows + pe_ref[which, CMP_STRIDE + r:CMP_STRIDE + r + 1, :]).astype(BF16),
                                w1_ref[which, CMP_STRIDE + r], preferred_element_type=F32)
        pre = top + pltpu.roll(bot, nseg - 1, 0)
        hid = _gelu_tanh(pre).astype(BF16)
        o_ref[...] = jnp.dot(hid, w2_ref[which], preferred_element_type=F32).astype(o_ref.dtype)


def _nsa_compress(kc, vc, pe, w1, w2):
    b, g, s, _ = kc.shape
    nseg = s // CMP_STRIDE
    x_spec = pl.BlockSpec((None, None, s, LANES), lambda bi, gi: (bi, gi, 0, 0))
    o_spec = pl.BlockSpec((None, None, nseg, LANES), lambda bi, gi: (bi, gi, 0, 0))
    o_shape = jax.ShapeDtypeStruct((b, g, nseg, LANES), BF16)
    return pl.pallas_call(
        _compress_kernel,
        grid=(b, g),
        in_specs=[
            x_spec, x_spec,
            pl.BlockSpec((2, CMP_BLOCK, LANES), lambda bi, gi: (0, 0, 0)),
            pl.BlockSpec((2, CMP_BLOCK, LANES, LANES), lambda bi, gi: (0, 0, 0, 0)),
            pl.BlockSpec((2, LANES, LANES), lambda bi, gi: (0, 0, 0)),
        ],
        out_specs=[o_spec, o_spec],
        out_shape=[o_shape, o_shape],
        compiler_params=_params("parallel", "parallel"),
        name="nsa_compress",
    )(kc, vc, pe, w1, w2)


def _stack_heads(q, hpg):
    return jnp.concatenate([q[:, h * LANES:(h + 1) * LANES] for h in range(hpg)], axis=0)


def _cmp_attn_kernel(q_ref, kc_ref, vc_ref, g_ref, ovl_ref, o_ref, sel_ref, *, tq, hpg, n_cmp, n_slc, top_n):
    i = pl.program_id(2)
    q8 = _stack_heads(q_ref[...], hpg)
    rows = hpg * tq
    ncp = kc_ref.shape[0]
    s = lax.dot_general(q8, kc_ref[...], NT_DIMS, preferred_element_type=F32)
    tpos = i * tq + lax.broadcasted_iota(jnp.int32, (tq, 1), 0)
    tpos8 = jnp.concatenate([tpos] * hpg, axis=0)
    ncol = lax.broadcasted_iota(jnp.int32, (1, ncp), 1)
    valid = jnp.where((ncol * CMP_STRIDE + (CMP_BLOCK - 1) <= tpos8) & (ncol < n_cmp), 1.0, 0.0)
    sm = jnp.where(valid > 0.5, s, NEG_INF)
    m = jnp.max(sm, axis=-1, keepdims=True)
    e = jnp.exp(sm - m) * valid
    l = jnp.sum(e, axis=-1, keepdims=True)
    p = e / jnp.where(l > 0.0, l, 1.0)
    o = jnp.dot(p.astype(BF16), vc_ref[...], preferred_element_type=F32)
    gates = g_ref[...]
    for h in range(hpg):
        o_ref[:, h * LANES:(h + 1) * LANES] = o[h * tq:(h + 1) * tq] * gates[:, h:h + 1]

    psum = p[0:tq]
    for h in range(1, hpg):
        psum = psum + p[h * tq:(h + 1) * tq]
    hi = psum.astype(BF16)
    lo = (psum - hi.astype(F32)).astype(BF16)
    ovl = ovl_ref[...]
    imp = (lax.dot_general(ovl, hi, NT_DIMS, preferred_element_type=F32)
           + lax.dot_general(ovl, lo, NT_DIMS, preferred_element_type=F32))
    jidx = lax.broadcasted_iota(jnp.int32, (n_slc, tq), 0)
    tq_pos = i * tq + lax.broadcasted_iota(jnp.int32, (n_slc, tq), 1)
    cur = jnp.right_shift(tq_pos, _SLC_SHIFT)
    forced = (jidx == 0) | (jidx == cur) | (jidx == cur - 1)
    causal = jidx * SLC_BLOCK <= tq_pos
    imp = jnp.where(forced, FORCED_SCORE, jnp.where(causal, imp, NEG_INF))
    sel_rows = []
    for j in range(n_slc):
        row = imp[j:j + 1, :]
        lower = jnp.where(jidx < j, 1.0, 0.0)
        beats = jnp.where(imp > row, 1.0, jnp.where(imp == row, lower, 0.0))
        rank = jnp.sum(beats, axis=0, keepdims=True)
        sel_rows.append(jnp.where(rank < top_n, 0.0, NEG_INF))
    sel_rows.append(jnp.zeros((LANES - n_slc, tq), F32))
    bias_t = jnp.concatenate(sel_rows, axis=0)
    sel_ref[...] = bias_t.T.astype(sel_ref.dtype)


def _nsa_cmp_attn(q, kcmp, vcmp, gates, ovl, *, groups, hpg, n_cmp, n_slc, top_n):
    b, s, d = q.shape
    ncp = kcmp.shape[2]
    tq = _pick(s, (512, 256, 128))
    qo_spec = pl.BlockSpec((None, tq, hpg * LANES), lambda bi, gi, i: (bi, i, gi))
    c_spec = pl.BlockSpec((None, None, ncp, LANES), lambda bi, gi, i: (bi, gi, 0, 0))
    return pl.pallas_call(
        functools.partial(_cmp_attn_kernel, tq=tq, hpg=hpg, n_cmp=n_cmp, n_slc=n_slc, top_n=top_n),
        grid=(b, groups, s // tq),
        in_specs=[
            qo_spec, c_spec, c_spec,
            pl.BlockSpec((None, tq, LANES), lambda bi, gi, i: (bi, i, gi)),
            pl.BlockSpec((n_slc, ncp), lambda bi, gi, i: (0, 0)),
        ],
        out_specs=[qo_spec, pl.BlockSpec((None, None, tq, LANES), lambda bi, gi, i: (bi, gi, i, 0))],
        out_shape=[jax.ShapeDtypeStruct((b, s, d), F32), jax.ShapeDtypeStruct((b, groups, s, LANES), BF16)],
        compiler_params=_params("parallel", "parallel", "parallel"),
        name="nsa_cmp_attn",
    )(q, kcmp, vcmp, gates, ovl)


def _stack_aug(q, upper, hpg):
    return jnp.concatenate(
        [jnp.concatenate([q[:, h * LANES:(h + 1) * LANES], upper[h]], axis=1) for h in range(hpg)], axis=0)


def _score_bounds(q, k_gain, hpg):
    k_bound = BOUND_SLACK * math.sqrt(LANES) * jnp.max(jnp.abs(k_gain), axis=-1, keepdims=True)
    out = []
    for h in range(hpg):
        qh = q[:, h * LANES:(h + 1) * LANES].astype(F32)
        out.append(jnp.sqrt(jnp.sum(qh * qh, axis=-1, keepdims=True)) * k_bound)
    return out


def _gated_store(o_ref, oin_ref, gates, o, first_gate, tq, hpg):
    for h in range(hpg):
        cols = slice(h * LANES, (h + 1) * LANES)
        gate = gates[:, first_gate + h:first_gate + h + 1]
        o_ref[:, cols] = (oin_ref[:, cols] + o[h * tq:(h + 1) * tq] * gate).astype(o_ref.dtype)


def _sel_attn_kernel(q_ref, k_ref, v_ref, sel_ref, kn_ref, g_ref, oin_ref, o_ref, *, tq, tk, hpg):
    i = pl.program_id(2)
    q = q_ref[...]
    bias = sel_ref[...].astype(F32)
    rows = hpg * tq
    tpos = i * tq + lax.broadcasted_iota(jnp.int32, (tq, 1), 0)
    n_kv = ((i + 1) * tq + tk - 1) // tk
    gates = g_ref[...]

    def causal_bias(start):
        kpos = start + lax.broadcasted_iota(jnp.int32, (1, tk), 1)
        return jnp.concatenate([jnp.where(kpos <= tpos, 0.0, NEG_INF)] * hpg, axis=0)

    q8 = _stack_aug(q, [(bias - m).astype(BF16) for m in _score_bounds(q, kn_ref[...], hpg)], hpg)

    def fast_tile(kv, acc, diagonal):
        start = pl.multiple_of(kv * tk, tk)
        s = lax.dot_general(q8, k_ref[pl.ds(start, tk), :], NT_DIMS, preferred_element_type=F32)
        if diagonal:
            s = s + causal_bias(start)
        return acc + jnp.dot(jnp.exp(s).astype(BF16), v_ref[pl.ds(start, tk), :], preferred_element_type=F32)

    acc = lax.fori_loop(0, n_kv - 1, lambda kv, a: fast_tile(kv, a, False), jnp.zeros((rows, 2 * LANES), F32))
    acc = fast_tile(n_kv - 1, acc, True)
    denom = acc[:, LANES:]
    _gated_store(o_ref, oin_ref, gates, acc[:, :LANES] / denom, hpg, tq, hpg)

    @pl.when(jnp.logical_not(jnp.min(denom) >= SOFTMAX_FLOOR))
    def _():
        q8x = _stack_aug(q, [sel_ref[...]] * hpg, hpg)

        def exact_tile(kv, carry, diagonal):
            m, acc = carry
            start = pl.multiple_of(kv * tk, tk)
            s = lax.dot_general(q8x, k_ref[pl.ds(start, tk), :], NT_DIMS, preferred_element_type=F32)
            if diagonal:
                s = s + causal_bias(start)
            m_new = jnp.maximum(m, jnp.max(s, axis=-1, keepdims=True))
            p = jnp.exp(s - m_new).astype(BF16)
            acc = jnp.exp(m - m_new) * acc + jnp.dot(p, v_ref[pl.ds(start, tk), :], preferred_element_type=F32)
            return m_new, acc

        init = (jnp.full((rows, 1), NEG_INF, F32), jnp.zeros((rows, 2 * LANES), F32))
        carry = lax.fori_loop(0, n_kv - 1, lambda kv, c: exact_tile(kv, c, False), init)
        _, acc_x = exact_tile(n_kv - 1, carry, True)
        _gated_store(o_ref, oin_ref, gates, acc_x[:, :LANES] / acc_x[:, LANES:], hpg, tq, hpg)


def _nsa_sel_attn(q, ks, vs, selb, k_gain, gates, o_in, *, groups, hpg):
    b, s, d = q.shape
    tq = _pick(s, (512, 256, 128))
    tk = _pick(s, (512, 256, 128))
    qo_spec = pl.BlockSpec((None, tq, hpg * LANES), lambda bi, gi, i: (bi, i, gi))
    kv_spec = pl.BlockSpec((None, None, s, 2 * LANES), lambda bi, gi, i: (bi, gi, 0, 0))
    return pl.pallas_call(
        functools.partial(_sel_attn_kernel, tq=tq, tk=tk, hpg=hpg),
        grid=(b, groups, s // tq),
        in_specs=[
            qo_spec, kv_spec, kv_spec,
            pl.BlockSpec((None, None, tq, LANES), lambda bi, gi, i: (bi, gi, i, 0)),
            pl.BlockSpec((1, LANES), lambda bi, gi, i: (0, 0)),
            pl.BlockSpec((None, tq, LANES), lambda bi, gi, i: (bi, i, gi)),
            qo_spec,
        ],
        out_specs=qo_spec,
        out_shape=jax.ShapeDtypeStruct((b, s, d), F32),
        compiler_params=_params("parallel", "parallel", "parallel"),
        name="nsa_sel_attn",
    )(q, ks, vs, selb, k_gain.reshape(1, LANES), gates, o_in)


def _win_attn_kernel(q_ref, k_ref, v_ref, kn_ref, g_ref, oin_ref, o_ref, *, tq, span, hpg):
    i = pl.program_id(2)
    q = q_ref[...]
    start = pl.multiple_of(jnp.maximum(i * tq - WINDOW, 0), tq)
    k = k_ref[pl.ds(start, span), :]
    v = v_ref[pl.ds(start, span), :]
    tpos = i * tq + lax.broadcasted_iota(jnp.int32, (tq, 1), 0)
    kpos = start + lax.broadcasted_iota(jnp.int32, (1, span), 1)
    diff = tpos - kpos
    bias = jnp.concatenate([jnp.where((diff >= 0) & (diff < WINDOW), 0.0, NEG_INF)] * hpg, axis=0)
    gates = g_ref[...]

    q8 = _stack_aug(q, [jnp.broadcast_to(-m, (tq, LANES)).astype(BF16) for m in _score_bounds(q, kn_ref[...], hpg)],
                    hpg)
    s = lax.dot_general(q8, k, NT_DIMS, preferred_element_type=F32) + bias
    acc = jnp.dot(jnp.exp(s).astype(BF16), v, preferred_element_type=F32)
    denom = acc[:, LANES:]
    _gated_store(o_ref, oin_ref, gates, acc[:, :LANES] / denom, 2 * hpg, tq, hpg)

    @pl.when(jnp.logical_not(jnp.min(denom) >= SOFTMAX_FLOOR))
    def _():
        q8x = _stack_aug(q, [jnp.zeros((tq, LANES), BF16)] * hpg, hpg)
        sx = lax.dot_general(q8x, k, NT_DIMS, preferred_element_type=F32) + bias
        p = jnp.exp(sx - jnp.max(sx, axis=-1, keepdims=True))
        acc_x = jnp.dot(p.astype(BF16), v, preferred_element_type=F32)
        _gated_store(o_ref, oin_ref, gates, acc_x[:, :LANES] / acc_x[:, LANES:], 2 * hpg, tq, hpg)


def _nsa_win_attn(q, kw, vw, k_gain, gates, o_in, *, groups, hpg):
    b, s, d = q.shape
    tq = _pick(s, (256, 128))
    span = WINDOW + tq
    assert s >= span and WINDOW % tq == 0
    qo_spec = pl.BlockSpec((None, tq, hpg * LANES), lambda bi, gi, i: (bi, i, gi))
    kv_spec = pl.BlockSpec((None, None, s, 2 * LANES), lambda bi, gi, i: (bi, gi, 0, 0))
    return pl.pallas_call(
        functools.partial(_win_attn_kernel, tq=tq, span=span, hpg=hpg),
        grid=(b, groups, s // tq),
        in_specs=[qo_spec, kv_spec, kv_spec, pl.BlockSpec((1, LANES), lambda bi, gi, i: (0, 0)),
                  pl.BlockSpec((None, tq, LANES), lambda bi, gi, i: (bi, i, gi)), qo_spec],
        out_specs=qo_spec,
        out_shape=jax.ShapeDtypeStruct((b, s, d), BF16),
        compiler_params=_params("parallel", "parallel", "parallel"),
        name="nsa_win_attn",
    )(q, kw, vw, k_gain.reshape(1, LANES), gates, o_in)


def _rope_tables(seq, dim):
    inv = ROPE_THETA ** (-jnp.arange(0, dim, 2, dtype=F32) / dim)
    ang = jnp.arange(seq, dtype=F32)[:, None] * inv[None, :]
    ang = jnp.concatenate([ang, ang], axis=-1)
    sign = jnp.where(jnp.arange(dim) < dim // 2, -1.0, 1.0).astype(F32)
    return jnp.cos(ang), jnp.sin(ang) * sign


def _nsa_mixer(x2, xg, ss, b, s, w_in, q_norm, k_norm, cmp_pos, cmp_w1, cmp_w2, w_o, next_gain):
    heads, groups = NSA_HEADS, NSA_GROUPS
    hpg = heads // groups
    d = x2.shape[1]
    assert d == heads * LANES and CMP_BLOCK == 2 * CMP_STRIDE and 3 * hpg <= LANES
    qd, kvd = heads * LANES, groups * LANES
    n_main = qd + 6 * kvd
    nseg = s // CMP_STRIDE
    n_cmp = (s - CMP_BLOCK) // CMP_STRIDE + 1
    n_slc = s // SLC_BLOCK
    top_n = min(SLC_TOPN, n_slc)
    assert n_cmp == nseg - 1 and n_slc % 8 == 0 and n_slc <= LANES and top_n >= 3

    proj = _mm(xg, ss, w_in, n=n_main, out_dtype=F32, name="nsa_in_proj")
    gates = _mm(xg, ss, w_in, col0=n_main, n=LANES, out_dtype=F32, sigmoid=True, name="nsa_gate_proj")
    gates = gates[:, :3 * heads].reshape(b * s, 3, groups, hpg).transpose(0, 2, 1, 3).reshape(b * s, groups, 3 * hpg)
    gates = jnp.pad(gates, ((0, 0), (0, 0), (0, LANES - 3 * hpg))).reshape(b, s, groups * LANES)

    cos, sin_signed = _rope_tables(s, LANES)
    q, kc, vc, ks, vs, kw, vw = _nsa_prep(proj.reshape(b, s, n_main), cos, sin_signed, q_norm, k_norm,
                                          heads=heads, groups=groups)

    w1 = cmp_w1.reshape(2, CMP_BLOCK, LANES, LANES).astype(BF16)
    kcmp, vcmp = _nsa_compress(kc, vc, cmp_pos, w1, cmp_w2.astype(BF16))

    cmp_start = np.arange(nseg) * CMP_STRIDE
    slc_start = np.arange(n_slc) * SLC_BLOCK
    ovl = ((cmp_start[None, :] < slc_start[:, None] + SLC_BLOCK) & (cmp_start[None, :] + CMP_BLOCK > slc_start[:, None])
           & (np.arange(nseg)[None, :] < n_cmp))
    ovl = jnp.asarray(ovl.astype(np.float32), dtype=BF16)

    o1, selm = _nsa_cmp_attn(q, kcmp, vcmp, gates, ovl, groups=groups, hpg=hpg, n_cmp=n_cmp, n_slc=n_slc, top_n=top_n)
    o2 = _nsa_sel_attn(q, ks, vs, selm, k_norm[1], gates, o1, groups=groups, hpg=hpg)
    o3 = _nsa_win_attn(q, kw, vw, k_norm[2], gates, o2, groups=groups, hpg=hpg)
    return _out_proj(o3.reshape(b * s, d), w_o.astype(BF16), x2, next_gain, name="nsa_out_proj")


def _chunk_cumsum(x, tri):
    w = x.shape[1]
    h1 = x.astype(BF16)
    r1 = x - h1.astype(F32)
    h2 = r1.astype(BF16)
    h3 = (r1 - h2.astype(F32)).astype(BF16)
    g3 = jnp.dot(tri, jnp.concatenate([h1, h2, h3], axis=1), preferred_element_type=F32)
    return g3[:, :w] + g3[:, w:2 * w] + g3[:, 2 * w:]


def _hgrn_kernel(q_ref, f_ref, i_ref, g_ref, lbl_ref, on_ref, o_ref, *, seq, chunk, layer, hps, unroll):
    lbl = lbl_ref[...]
    e = jnp.exp(lbl - jnp.max(lbl, axis=0, keepdims=True))
    p = e / jnp.sum(e, axis=0, keepdims=True)
    csum = p[0:1]
    for d in range(1, layer + 1):
        csum = csum + p[d:d + 1]
    lb = csum - p[0:1]
    o_gain = on_ref[...]
    causal = (lax.broadcasted_iota(jnp.int32, (chunk, chunk), 0) >= lax.broadcasted_iota(jnp.int32, (chunk, chunk), 1))
    tri = jnp.where(causal, 1.0, 0.0).astype(BF16)
    heads = [slice(k * LANES, (k + 1) * LANES) for k in range(hps)]

    def body(it, states):
        states = list(states)
        chunks = []
        for u in range(unroll):
            rows = pl.ds(pl.multiple_of((it * unroll + u) * chunk, chunk), chunk)
            f = lb + (1.0 - lb) * jax.nn.sigmoid(f_ref[rows, :])
            kc = 1.0 - f
            g_cum = _chunk_cumsum(jnp.log(f), tri)
            g_last = g_cum[chunk - 1:chunk, :]
            q_dec = (q_ref[rows, :] * jnp.exp(g_cum)).astype(BF16)
            k_inv = (kc * jnp.exp(-g_cum)).astype(BF16)
            k_tail = (kc * jnp.exp(g_last - g_cum)).astype(BF16)
            chunks.append((rows, q_dec, k_inv, k_tail, i_ref[rows, :].astype(BF16), jnp.exp(g_last)))
        a_all = [[lax.dot_general(q_dec[:, h], k_inv[:, h], NT_DIMS, preferred_element_type=F32) for h in heads]
                 for (_, q_dec, k_inv, _, _, _) in chunks]
        ds_all = [[lax.dot_general(v[:, h], k_tail[:, h], TN_DIMS, preferred_element_type=F32) for h in heads]
                  for (_, _, _, k_tail, v, _) in chunks]
        o_all = [[jnp.dot(jnp.where(causal, a, 0.0).astype(BF16), v[:, h], preferred_element_type=F32)
                  for a, h in zip(a_row, heads)]
                 for a_row, (_, _, _, _, v, _) in zip(a_all, chunks)]
        for u, (rows, q_dec, _, _, _, decay) in enumerate(chunks):
            for k, h in enumerate(heads):
                o = o_all[u][k] + lax.dot_general(q_dec[:, h], states[k].astype(BF16), NT_DIMS,
                                                  preferred_element_type=F32)
                states[k] = states[k] * decay[:, h] + ds_all[u][k]
                y = o * lax.rsqrt(jnp.mean(o * o, axis=-1, keepdims=True) + RMS_EPS) * o_gain
                gz = g_ref[rows, h]
                o_ref[rows, h] = (y * (gz * jax.nn.sigmoid(gz))).astype(o_ref.dtype)
        return tuple(states)

    init = tuple(jnp.zeros((LANES, LANES), F32) for _ in range(hps))
    lax.fori_loop(0, seq // (chunk * unroll), body, init)


def _hgrn_core(proj, lb_logits, o_norm, *, heads, layer, hps=4, unroll=8):
    b, s, _ = proj.shape
    depth = lb_logits.shape[0]
    hps = math.gcd(hps, heads)
    width = hps * LANES
    groups = heads // hps
    assert s % (HGRN_CHUNK * unroll) == 0

    def col_spec(part):
        return pl.BlockSpec((None, s, width), lambda bi, hi: (bi, 0, part * groups + hi))

    return pl.pallas_call(
        functools.partial(_hgrn_kernel, seq=s, chunk=HGRN_CHUNK, layer=layer, hps=hps, unroll=unroll),
        grid=(b, groups),
        in_specs=[col_spec(0), col_spec(1), col_spec(2), col_spec(3),
                  pl.BlockSpec((depth, width), lambda bi, hi: (0, hi)),
                  pl.BlockSpec((1, LANES), lambda bi, hi: (0, 0))],
        out_specs=pl.BlockSpec((None, s, width), lambda bi, hi: (bi, 0, hi)),
        out_shape=jax.ShapeDtypeStruct((b, s, heads * LANES), BF16),
        compiler_params=_params("parallel", "parallel"),
        name="hgrn_core",
    )(proj, proj, proj, proj, lb_logits, o_norm.reshape(1, LANES))


def _hgrn_mixer(x2, xg, ss, b, s, w_in, lb_logits, o_norm, w_o, layer, next_gain):
    heads = HGRN_HEADS
    d = x2.shape[1]
    assert d == heads * LANES and w_in.shape[1] == 4 * d and s % HGRN_CHUNK == 0
    proj = _mm(xg, ss, w_in, out_dtype=F32, name="hgrn_in_proj")
    o = _hgrn_core(proj.reshape(b, s, 4 * d), lb_logits, o_norm, heads=heads, layer=layer)
    return _out_proj(o.reshape(b * s, d), w_o.astype(BF16), x2, next_gain, name="hgrn_out_proj")


def kernel(x, ffn_norm, ffn_w_gate, ffn_w_up, ffn_w_down, mix_norm, nsa_w_in, nsa_q_norm, nsa_k_norm, nsa_cmp_pos,
           nsa_cmp_w1, nsa_cmp_w2, nsa_w_o, hgrn_w_in, hgrn_lb_logits, hgrn_o_norm, hgrn_w_o):
    b, s, d = x.shape
    depth = ffn_norm.shape[0]
    x2 = x.reshape(b * s, d)
    xg, ss = _prenorm(x2, ffn_norm[0, 0])
    for layer in range(depth):
        slot = layer // N_MIXERS
        x2, xg, ss = _ffn(x2, xg, ss, ffn_w_gate, ffn_w_up, ffn_w_down, (layer, 0), mix_norm[layer])
        if layer % N_MIXERS == 0:
            x2, xg, ss = _nsa_mixer(x2, xg, ss, b, s, nsa_w_in[slot], nsa_q_norm[slot], nsa_k_norm[slot],
                                    nsa_cmp_pos[slot], nsa_cmp_w1[slot], nsa_cmp_w2[slot], nsa_w_o[slot],
                                    ffn_norm[layer, 1])
        else:
            x2, xg, ss = _hgrn_mixer(x2, xg, ss, b, s, hgrn_w_in[slot], hgrn_lb_logits, hgrn_o_norm[slot],
                                     hgrn_w_o[slot], layer, ffn_norm[layer, 1])
        if layer + 1 < depth:
            x2, xg, ss = _ffn(x2, xg, ss, ffn_w_gate, ffn_w_up, ffn_w_down, (layer, 1), ffn_norm[layer + 1, 0])
        else:
            x2 = _ffn(x2, xg, ss, ffn_w_gate, ffn_w_up, ffn_w_down, (layer, 1), None)
    return x2.reshape(b, s, d)
```

```python
import functools
import math

import jax
import jax.numpy as jnp
import numpy as np
from jax import lax
from jax.experimental import pallas as pl
from jax.experimental.pallas import tpu as pltpu

F32 = jnp.float32
BF16 = jnp.bfloat16

RMS_EPS = 1e-6
NEG_INF = -1e30
FORCED_SCORE = 1e30
FFN_RES_WEIGHT = 0.5
N_MIXERS = 2

NSA_HEADS = 32
NSA_GROUPS = 4
CMP_BLOCK = 32
CMP_STRIDE = 16
SLC_BLOCK = 64
SLC_TOPN = 16
_SLC_SHIFT = SLC_BLOCK.bit_length() - 1
assert 1 << _SLC_SHIFT == SLC_BLOCK
WINDOW = 512
ROPE_THETA = 10000.0

HGRN_HEADS = 32
HGRN_CHUNK = 64

LANES = 128
SOFTMAX_FLOOR = 1e-30
BOUND_SLACK = 1.02
VMEM_LIMIT_BYTES = 56 * 1024 * 1024

NT_DIMS = (((1,), (1,)), ((), ()))
TN_DIMS = (((0,), (0,)), ((), ()))


def _params(*sem):
    return pltpu.CompilerParams(dimension_semantics=sem, vmem_limit_bytes=VMEM_LIMIT_BYTES)


def _pick(n, prefs):
    for p in prefs:
        if n % p == 0:
            return p
    raise ValueError(f"no tile in {prefs} divides {n}")


def _fold_lanes(x):
    out = x[:, :LANES]
    for t in range(1, x.shape[1] // LANES):
        out = out + x[:, t * LANES:(t + 1) * LANES]
    return out


def _inv_rms(ss, width):
    total = jnp.sum(jnp.sum(ss, axis=0), axis=-1, keepdims=True)
    return lax.rsqrt(total * (1.0 / width) + RMS_EPS)


def _prenorm_kernel(x_ref, g_ref, xg_ref, ss_ref):
    x = x_ref[...]
    xg_ref[...] = (x * g_ref[...]).astype(xg_ref.dtype)
    ss_ref[0] = _fold_lanes(x * x)


def _prenorm(x, g):
    m, d = x.shape
    tr = _pick(m, (256, 128, 8))
    return pl.pallas_call(
        _prenorm_kernel,
        grid=(m // tr,),
        in_specs=[pl.BlockSpec((tr, d), lambda i: (i, 0)), pl.BlockSpec((1, d), lambda i: (0, 0))],
        out_specs=[pl.BlockSpec((tr, d), lambda i: (i, 0)), pl.BlockSpec((1, tr, LANES), lambda i: (0, i, 0))],
        out_shape=[jax.ShapeDtypeStruct((m, d), BF16), jax.ShapeDtypeStruct((1, m, LANES), F32)],
        compiler_params=_params("parallel"),
        name="prenorm",
    )(x, g.reshape(1, d))


def _weight_spec(w, lead, bn, first_tile=0):
    k = w.shape[-2]
    return pl.BlockSpec((None,) * len(lead) + (k, bn), lambda i, j: lead + (0, first_tile + j))


def _lhs_spec(bm, k, single_buffer):
    if single_buffer:
        return pl.BlockSpec((bm, k), lambda i, j: (i, 0), pipeline_mode=pl.Buffered(1))
    return pl.BlockSpec((bm, k), lambda i, j: (i, 0))


def _stat_spec(planes, bm):
    return pl.BlockSpec((planes, bm, LANES), lambda i, j: (0, i, 0))


def _gateup_kernel(xg_ref, ss_ref, wg_ref, wu_ref, *rest, width, cast_down):
    if cast_down:
        wd_ref, o_ref, wd_out_ref, r_ref = rest
        wd_out_ref[...] = wd_ref[...].astype(wd_out_ref.dtype)
    else:
        o_ref, r_ref = rest

    @pl.when(pl.program_id(1) == 0)
    def _():
        r_ref[...] = jnp.broadcast_to(_inv_rms(ss_ref[...], width), r_ref.shape)

    xg = xg_ref[...]
    r = jnp.concatenate([r_ref[...]] * (o_ref.shape[1] // LANES), axis=1)
    a = jnp.dot(xg, wg_ref[...].astype(BF16), preferred_element_type=F32) * r
    b = jnp.dot(xg, wu_ref[...].astype(BF16), preferred_element_type=F32) * r
    o_ref[...] = (a * jax.nn.sigmoid(a) * b).astype(o_ref.dtype)


def _gateup(xg, ss, wg, wu, wd, lead):
    m, k = xg.shape
    n = wg.shape[-1]
    bm = _pick(m, (2048, 1024, 512, 256, 128))
    bn = _pick(n, (256, 128))
    steps_j = n // bn
    steps = (m // bm) * steps_j
    slab = wd.shape[-2] // steps
    cast_down = slab * steps == wd.shape[-2] and slab % 16 == 0
    in_specs = [_lhs_spec(bm, k, True), _stat_spec(ss.shape[0], bm), _weight_spec(wg, lead, bn),
                _weight_spec(wu, lead, bn)]
    out_specs = pl.BlockSpec((bm, bn), lambda i, j: (i, j))
    out_shape = jax.ShapeDtypeStruct((m, n), BF16)
    args = (xg, ss, wg, wu)
    if cast_down:
        d_out = wd.shape[-1]
        in_specs.append(pl.BlockSpec((None,) * len(lead) + (slab, d_out), lambda i, j: lead + (i * steps_j + j, 0)))
        out_specs = [out_specs, pl.BlockSpec((slab, d_out), lambda i, j: (i * steps_j + j, 0))]
        out_shape = [out_shape, jax.ShapeDtypeStruct(wd.shape[-2:], BF16)]
        args = args + (wd,)
    out = pl.pallas_call(
        functools.partial(_gateup_kernel, width=k, cast_down=cast_down),
        grid=(m // bm, steps_j),
        in_specs=in_specs,
        out_specs=out_specs,
        out_shape=out_shape,
        scratch_shapes=[pltpu.VMEM((bm, LANES), F32)],
        compiler_params=_params("parallel", "arbitrary"),
        name="ffn_gateup",
    )(*args)
    return out if cast_down else (out, wd[lead].astype(BF16))


def _out_proj_kernel(a_ref, w_ref, r_ref, gn_ref, o_ref, xg_ref, ss_ref):
    x = r_ref[...] + jnp.dot(a_ref[...], w_ref[...], preferred_element_type=F32)
    o_ref[...] = x
    xg_ref[...] = (x * gn_ref[...]).astype(xg_ref.dtype)

    @pl.when(pl.program_id(1) == 0)
    def _():
        ss_ref[...] = jnp.zeros_like(ss_ref)

    ss_ref[0] += _fold_lanes(x * x)


def _out_proj(a, w, res, next_gain, *, name):
    m, k = a.shape
    n = w.shape[-1]
    bm = _pick(m, (1024, 512, 256, 128))
    bn = _pick(n, (512, 256, 128))
    tile = pl.BlockSpec((bm, bn), lambda i, j: (i, j))
    return pl.pallas_call(
        _out_proj_kernel,
        grid=(m // bm, n // bn),
        in_specs=[_lhs_spec(bm, k, False), _weight_spec(w, (), bn), tile, pl.BlockSpec((1, bn), lambda i, j: (0, j))],
        out_specs=[tile, tile, _stat_spec(1, bm)],
        out_shape=[jax.ShapeDtypeStruct((m, n), F32), jax.ShapeDtypeStruct((m, n), BF16),
                   jax.ShapeDtypeStruct((1, m, LANES), F32)],
        compiler_params=_params("parallel", "arbitrary"),
        name=name,
    )(a, w, res, next_gain.reshape(1, n))


def _mm_kernel(xg_ref, ss_ref, w_ref, o_ref, *, width, sigmoid, live_cols):
    w = w_ref[...]
    if live_cols < w.shape[1]:
        w = jnp.where(lax.broadcasted_iota(jnp.int32, w.shape, 1) < live_cols, w, 0.0)
    acc = jnp.dot(xg_ref[...], w.astype(BF16), preferred_element_type=F32) * _inv_rms(ss_ref[...], width)
    if sigmoid:
        acc = jax.nn.sigmoid(acc)
    o_ref[...] = acc.astype(o_ref.dtype)


def _mm(xg, ss, w, *, col0=0, n=None, out_dtype, sigmoid=False, name):
    m, k = xg.shape
    n = w.shape[-1] if n is None else n
    bn = _pick(n, (512, 256, 128))
    assert col0 % bn == 0
    live_cols = min(bn, w.shape[-1] - col0) if n == bn else bn
    assert col0 + n <= w.shape[-1] or n == bn
    tall = n // bn >= 8
    bm = _pick(m, (2048, 1024, 512, 256, 128) if tall else (1024, 512, 256, 128))
    return pl.pallas_call(
        functools.partial(_mm_kernel, width=k, sigmoid=sigmoid, live_cols=live_cols),
        grid=(m // bm, n // bn),
        in_specs=[_lhs_spec(bm, k, tall), _stat_spec(ss.shape[0], bm), _weight_spec(w, (), bn, col0 // bn)],
        out_specs=pl.BlockSpec((bm, bn), lambda i, j: (i, j)),
        out_shape=jax.ShapeDtypeStruct((m, n), out_dtype),
        compiler_params=_params("parallel", "arbitrary"),
        name=name,
    )(xg, ss, w)


def _ffn_down_kernel(a_ref, w_ref, r_ref, *rest, scale, emit_norm):
    x = r_ref[...] + scale * jnp.dot(a_ref[...], w_ref[...], preferred_element_type=F32)
    if not emit_norm:
        (o_ref,) = rest
        o_ref[...] = x
        return
    gn_ref, o_ref, xg_ref, ss_ref = rest
    o_ref[...] = x
    xg_ref[...] = (x * gn_ref[...]).astype(xg_ref.dtype)
    ss_ref[...] = _fold_lanes(x * x)


def _ffn_down(act, w, res, scale, next_gain):
    m, k = act.shape
    n = w.shape[-1]
    bm = _pick(m, (512, 256, 128))
    bn = _pick(n, (1024, 512, 256, 128))
    tile = pl.BlockSpec((bm, bn), lambda j, i: (i, j))
    emit_norm = next_gain is not None
    in_specs = [pl.BlockSpec((bm, k), lambda j, i: (i, 0)),
                pl.BlockSpec((k, bn), lambda j, i: (0, j), pipeline_mode=pl.Buffered(1)),
                tile]
    out_specs, out_shape, args = tile, jax.ShapeDtypeStruct((m, n), F32), (act, w, res)
    if emit_norm:
        in_specs.append(pl.BlockSpec((1, bn), lambda j, i: (0, j)))
        out_specs = [tile, tile, pl.BlockSpec((None, bm, LANES), lambda j, i: (j, i, 0))]
        out_shape = [out_shape, jax.ShapeDtypeStruct((m, n), BF16), jax.ShapeDtypeStruct((n // bn, m, LANES), F32)]
        args = args + (next_gain.reshape(1, n),)
    return pl.pallas_call(
        functools.partial(_ffn_down_kernel, scale=scale, emit_norm=emit_norm),
        grid=(n // bn, m // bm),
        in_specs=in_specs,
        out_specs=out_specs,
        out_shape=out_shape,
        compiler_params=_params("parallel", "parallel"),
        name="ffn_down",
    )(*args)


def _ffn(x, xg, ss, w_gate, w_up, w_down, lead, next_gain):
    act, w_down_bf16 = _gateup(xg, ss, w_gate, w_up, w_down, lead)
    return _ffn_down(act, w_down_bf16, x, FFN_RES_WEIGHT, next_gain)


def _nsa_prep_kernel(p_ref, cos_ref, sin_ref, qn_ref, kn_ref, q_ref, kc_ref, vc_ref, ks_ref, vs_ref, kw_ref, vw_ref,
                     *, heads, groups):
    cos = cos_ref[...]
    sin = sin_ref[...]
    scale = LANES ** -0.5
    base = heads * LANES
    kv_w = groups * LANES
    ts = p_ref.shape[0]
    pos = pl.program_id(1) * ts + lax.broadcasted_iota(jnp.int32, (ts, LANES), 0)
    lane = lax.broadcasted_iota(jnp.int32, (ts, LANES), 1)
    blk_onehot = jnp.where(lane == jnp.right_shift(pos, _SLC_SHIFT), 1.0, 0.0).astype(ks_ref.dtype)
    lane0_onehot = jnp.where(lane == 0, 1.0, 0.0).astype(kw_ref.dtype)

    def store_q(h):
        def store(y):
            q_ref[:, h * LANES:(h + 1) * LANES] = y.astype(q_ref.dtype)
        return store

    def store_k(k_out, g, upper):
        def store(y):
            if upper is None:
                k_out[g] = y.astype(k_out.dtype)
            else:
                k_out[g, :, :LANES] = y.astype(k_out.dtype)
                k_out[g, :, LANES:] = upper
        return store

    jobs = [(h * LANES, qn_ref[...], scale, store_q(h)) for h in range(heads)]
    for br, (k_out, upper) in enumerate(((kc_ref, None), (ks_ref, blk_onehot), (kw_ref, lane0_onehot))):
        for g in range(groups):
            jobs.append((base + (2 * br) * kv_w + g * LANES, kn_ref[br:br + 1, :], None, store_k(k_out, g, upper)))
    inv = [lax.rsqrt(jnp.mean(jnp.square(p_ref[:, c:c + LANES]), axis=-1, keepdims=True) + RMS_EPS)
           for c, _, _, _ in jobs]
    for (c, gain, post, store), r in zip(jobs, inv):
        y = p_ref[:, c:c + LANES] * r * gain
        y = y * cos + pltpu.roll(y, LANES // 2, 1) * sin
        store(y if post is None else y * post)

    for br, v_out in enumerate((vc_ref, vs_ref, vw_ref)):
        for g in range(groups):
            cv = base + (2 * br + 1) * kv_w + g * LANES
            v = p_ref[:, cv:cv + LANES].astype(v_out.dtype)
            if br == 0:
                v_out[g] = v
            else:
                v_out[g, :, :LANES] = v
                v_out[g, :, LANES:] = jnp.ones_like(v)


def _nsa_prep(proj, cos, sin_signed, q_norm, k_norm, *, heads, groups):
    b, s, n = proj.shape
    ts = _pick(s, (256, 128))
    kv_spec = pl.BlockSpec((None, groups, ts, LANES), lambda bi, i: (bi, 0, i, 0))
    aug_spec = pl.BlockSpec((None, groups, ts, 2 * LANES), lambda bi, i: (bi, 0, i, 0))
    tab_spec = pl.BlockSpec((ts, LANES), lambda bi, i: (i, 0))

    def kv_shape(dt, width=LANES):
        return jax.ShapeDtypeStruct((b, groups, s, width), dt)

    return pl.pallas_call(
        functools.partial(_nsa_prep_kernel, heads=heads, groups=groups),
        grid=(b, s // ts),
        in_specs=[
            pl.BlockSpec((None, ts, n), lambda bi, i: (bi, i, 0)),
            tab_spec,
            tab_spec,
            pl.BlockSpec((1, LANES), lambda bi, i: (0, 0)),
            pl.BlockSpec((3, LANES), lambda bi, i: (0, 0)),
        ],
        out_specs=[pl.BlockSpec((None, ts, heads * LANES), lambda bi, i: (bi, i, 0)),
                   kv_spec, kv_spec, aug_spec, aug_spec, aug_spec, aug_spec],
        out_shape=[jax.ShapeDtypeStruct((b, s, heads * LANES), BF16),
                   kv_shape(F32), kv_shape(F32), kv_shape(BF16, 2 * LANES), kv_shape(BF16, 2 * LANES),
                   kv_shape(BF16, 2 * LANES), kv_shape(BF16, 2 * LANES)],
        compiler_params=_params("parallel", "parallel"),
        name="nsa_prep",
    )(proj, cos, sin_signed, q_norm.reshape(1, LANES), k_norm)


def _gelu_tanh(x):
    return x * (0.5 * (1.0 + jnp.tanh(math.sqrt(2.0 / math.pi) * (x + 0.044715 * (x * x * x)))))


def _compress_kernel(k_ref, v_ref, pe_ref, w1_ref, w2_ref, ko_ref, vo_ref):
    nseg = ko_ref.shape[0]
    for which, (x_ref, o_ref) in enumerate(((k_ref, ko_ref), (v_ref, vo_ref))):
        top = jnp.zeros((nseg, LANES), F32)
        bot = jnp.zeros((nseg, LANES), F32)
        for r in range(CMP_STRIDE):
            rows = x_ref[pl.ds(r, nseg, stride=CMP_STRIDE), :]
            top = top + jnp.dot((rows + pe_ref[which, r:r + 1, :]).astype(BF16), w1_ref[which, r],
                                preferred_element_type=F32)
            bot = bot + jnp.dot((rows + pe_ref[which, CMP_STRIDE + r:CMP_STRIDE + r + 1, :]).astype(BF16),
                                w1_ref[which, CMP_STRIDE + r], preferred_element_type=F32)
        pre = top + pltpu.roll(bot, nseg - 1, 0)
        hid = _gelu_tanh(pre).astype(BF16)
        o_ref[...] = jnp.dot(hid, w2_ref[which], preferred_element_type=F32).astype(o_ref.dtype)


def _nsa_compress(kc, vc, pe, w1, w2):
    b, g, s, _ = kc.shape
    nseg = s // CMP_STRIDE
    x_spec = pl.BlockSpec((None, None, s, LANES), lambda bi, gi: (bi, gi, 0, 0))
    o_spec = pl.BlockSpec((None, None, nseg, LANES), lambda bi, gi: (bi, gi, 0, 0))
    o_shape = jax.ShapeDtypeStruct((b, g, nseg, LANES), BF16)
    return pl.pallas_call(
        _compress_kernel,
        grid=(b, g),
        in_specs=[
            x_spec, x_spec,
            pl.BlockSpec((2, CMP_BLOCK, LANES), lambda bi, gi: (0, 0, 0)),
            pl.BlockSpec((2, CMP_BLOCK, LANES, LANES), lambda bi, gi: (0, 0, 0, 0)),
            pl.BlockSpec((2, LANES, LANES), lambda bi, gi: (0, 0, 0)),
        ],
        out_specs=[o_spec, o_spec],
        out_shape=[o_shape, o_shape],
        compiler_params=_params("parallel", "parallel"),
        name="nsa_compress",
    )(kc, vc, pe, w1, w2)


def _stack_heads(q, hpg):
    return jnp.concatenate([q[:, h * LANES:(h + 1) * LANES] for h in range(hpg)], axis=0)


def _cmp_attn_kernel(q_ref, kc_ref, vc_ref, g_ref, ovl_ref, o_ref, sel_ref, *, tq, hpg, n_cmp, n_slc, top_n):
    i = pl.program_id(2)
    q8 = _stack_heads(q_ref[...], hpg)
    rows = hpg * tq
    ncp = kc_ref.shape[0]
    s = lax.dot_general(q8, kc_ref[...], NT_DIMS, preferred_element_type=F32)
    tpos = i * tq + lax.broadcasted_iota(jnp.int32, (tq, 1), 0)
    tpos8 = jnp.concatenate([tpos] * hpg, axis=0)
    ncol = lax.broadcasted_iota(jnp.int32, (1, ncp), 1)
    valid = jnp.where((ncol * CMP_STRIDE + (CMP_BLOCK - 1) <= tpos8) & (ncol < n_cmp), 1.0, 0.0)
    sm = jnp.where(valid > 0.5, s, NEG_INF)
    m = jnp.max(sm, axis=-1, keepdims=True)
    e = jnp.exp(sm - m) * valid
    l = jnp.sum(e, axis=-1, keepdims=True)
    p = e / jnp.where(l > 0.0, l, 1.0)
    o = jnp.dot(p.astype(BF16), vc_ref[...], preferred_element_type=F32)
    gates = g_ref[...]
    for h in range(hpg):
        o_ref[:, h * LANES:(h + 1) * LANES] = o[h * tq:(h + 1) * tq] * gates[:, h:h + 1]

    psum = p[0:tq]
    for h in range(1, hpg):
        psum = psum + p[h * tq:(h + 1) * tq]
    hi = psum.astype(BF16)
    lo = (psum - hi.astype(F32)).astype(BF16)
    ovl = ovl_ref[...]
    imp = (lax.dot_general(ovl, hi, NT_DIMS, preferred_element_type=F32)
           + lax.dot_general(ovl, lo, NT_DIMS, preferred_element_type=F32))
    jidx = lax.broadcasted_iota(jnp.int32, (n_slc, tq), 0)
    tq_pos = i * tq + lax.broadcasted_iota(jnp.int32, (n_slc, tq), 1)
    cur = jnp.right_shift(tq_pos, _SLC_SHIFT)
    forced = (jidx == 0) | (jidx == cur) | (jidx == cur - 1)
    causal = jidx * SLC_BLOCK <= tq_pos
    imp = jnp.where(forced, FORCED_SCORE, jnp.where(causal, imp, NEG_INF))
    sel_rows = []
    for j in range(n_slc):
        row = imp[j:j + 1, :]
        lower = jnp.where(jidx < j, 1.0, 0.0)
        beats = jnp.where(imp > row, 1.0, jnp.where(imp == row, lower, 0.0))
        rank = jnp.sum(beats, axis=0, keepdims=True)
        sel_rows.append(jnp.where(rank < top_n, 0.0, NEG_INF))
    sel_rows.append(jnp.zeros((LANES - n_slc, tq), F32))
    bias_t = jnp.concatenate(sel_rows, axis=0)
    sel_ref[...] = bias_t.T.astype(sel_ref.dtype)


def _nsa_cmp_attn(q, kcmp, vcmp, gates, ovl, *, groups, hpg, n_cmp, n_slc, top_n):
    b, s, d = q.shape
    ncp = kcmp.shape[2]
    tq = _pick(s, (512, 256, 128))
    qo_spec = pl.BlockSpec((None, tq, hpg * LANES), lambda bi, gi, i: (bi, i, gi))
    c_spec = pl.BlockSpec((None, None, ncp, LANES), lambda bi, gi, i: (bi, gi, 0, 0))
    return pl.pallas_call(
        functools.partial(_cmp_attn_kernel, tq=tq, hpg=hpg, n_cmp=n_cmp, n_slc=n_slc, top_n=top_n),
        grid=(b, groups, s // tq),
        in_specs=[
            qo_spec, c_spec, c_spec,
            pl.BlockSpec((None, tq, LANES), lambda bi, gi, i: (bi, i, gi)),
            pl.BlockSpec((n_slc, ncp), lambda bi, gi, i: (0, 0)),
        ],
        out_specs=[qo_spec, pl.BlockSpec((None, None, tq, LANES), lambda bi, gi, i: (bi, gi, i, 0))],
        out_shape=[jax.ShapeDtypeStruct((b, s, d), F32), jax.ShapeDtypeStruct((b, groups, s, LANES), BF16)],
        compiler_params=_params("parallel", "parallel", "parallel"),
        name="nsa_cmp_attn",
    )(q, kcmp, vcmp, gates, ovl)


def _stack_aug(q, upper, hpg):
    return jnp.concatenate(
        [jnp.concatenate([q[:, h * LANES:(h + 1) * LANES], upper[h]], axis=1) for h in range(hpg)], axis=0)


def _score_bounds(q, k_gain, hpg):
    k_bound = BOUND_SLACK * math.sqrt(LANES) * jnp.max(jnp.abs(k_gain), axis=-1, keepdims=True)
    out = []
    for h in range(hpg):
        qh = q[:, h * LANES:(h + 1) * LANES].astype(F32)
        out.append(jnp.sqrt(jnp.sum(qh * qh, axis=-1, keepdims=True)) * k_bound)
    return out


def _gated_store(o_ref, oin_ref, gates, o, first_gate, tq, hpg):
    for h in range(hpg):
        cols = slice(h * LANES, (h + 1) * LANES)
        gate = gates[:, first_gate + h:first_gate + h + 1]
        o_ref[:, cols] = (oin_ref[:, cols] + o[h * tq:(h + 1) * tq] * gate).astype(o_ref.dtype)


def _sel_attn_kernel(q_ref, k_ref, v_ref, sel_ref, kn_ref, g_ref, oin_ref, o_ref, *, tq, tk, hpg):
    i = pl.program_id(2)
    q = q_ref[...]
    bias = sel_ref[...].astype(F32)
    rows = hpg * tq
    tpos = i * tq + lax.broadcasted_iota(jnp.int32, (tq, 1), 0)
    n_kv = ((i + 1) * tq + tk - 1) // tk
    gates = g_ref[...]

    def causal_bias(start):
        kpos = start + lax.broadcasted_iota(jnp.int32, (1, tk), 1)
        return jnp.concatenate([jnp.where(kpos <= tpos, 0.0, NEG_INF)] * hpg, axis=0)

    q8 = _stack_aug(q, [(bias - m).astype(BF16) for m in _score_bounds(q, kn_ref[...], hpg)], hpg)

    def fast_tile(kv, acc, diagonal):
        start = pl.multiple_of(kv * tk, tk)
        s = lax.dot_general(q8, k_ref[pl.ds(start, tk), :], NT_DIMS, preferred_element_type=F32)
        if diagonal:
            s = s + causal_bias(start)
        return acc + jnp.dot(jnp.exp(s).astype(BF16), v_ref[pl.ds(start, tk), :], preferred_element_type=F32)

    acc = lax.fori_loop(0, n_kv - 1, lambda kv, a: fast_tile(kv, a, False), jnp.zeros((rows, 2 * LANES), F32))
    acc = fast_tile(n_kv - 1, acc, True)
    denom = acc[:, LANES:]
    _gated_store(o_ref, oin_ref, gates, acc[:, :LANES] / denom, hpg, tq, hpg)

    @pl.when(jnp.logical_not(jnp.min(denom) >= SOFTMAX_FLOOR))
    def _():
        q8x = _stack_aug(q, [sel_ref[...]] * hpg, hpg)

        def exact_tile(kv, carry, diagonal):
            m, acc = carry
            start = pl.multiple_of(kv * tk, tk)
            s = lax.dot_general(q8x, k_ref[pl.ds(start, tk), :], NT_DIMS, preferred_element_type=F32)
            if diagonal:
                s = s + causal_bias(start)
            m_new = jnp.maximum(m, jnp.max(s, axis=-1, keepdims=True))
            p = jnp.exp(s - m_new).astype(BF16)
            acc = jnp.exp(m - m_new) * acc + jnp.dot(p, v_ref[pl.ds(start, tk), :], preferred_element_type=F32)
            return m_new, acc

        init = (jnp.full((rows, 1), NEG_INF, F32), jnp.zeros((rows, 2 * LANES), F32))
        carry = lax.fori_loop(0, n_kv - 1, lambda kv, c: exact_tile(kv, c, False), init)
        _, acc_x = exact_tile(n_kv - 1, carry, True)
        _gated_store(o_ref, oin_ref, gates, acc_x[:, :LANES] / acc_x[:, LANES:], hpg, tq, hpg)


def _nsa_sel_attn(q, ks, vs, selb, k_gain, gates, o_in, *, groups, hpg):
    b, s, d = q.shape
    tq = _pick(s, (512, 256, 128))
    tk = _pick(s, (512, 256, 128))
    qo_spec = pl.BlockSpec((None, tq, hpg * LANES), lambda bi, gi, i: (bi, i, gi))
    kv_spec = pl.BlockSpec((None, None, s, 2 * LANES), lambda bi, gi, i: (bi, gi, 0, 0))
    return pl.pallas_call(
        functools.partial(_sel_attn_kernel, tq=tq, tk=tk, hpg=hpg),
        grid=(b, groups, s // tq),
        in_specs=[
            qo_spec, kv_spec, kv_spec,
            pl.BlockSpec((None, None, tq, LANES), lambda bi, gi, i: (bi, gi, i, 0)),
            pl.BlockSpec((1, LANES), lambda bi, gi, i: (0, 0)),
            pl.BlockSpec((None, tq, LANES), lambda bi, gi, i: (bi, i, gi)),
            qo_spec,
        ],
        out_specs=qo_spec,
        out_shape=jax.ShapeDtypeStruct((b, s, d), F32),
        compiler_params=_params("parallel", "parallel", "parallel"),
        name="nsa_sel_attn",
    )(q, ks, vs, selb, k_gain.reshape(1, LANES), gates, o_in)


def _win_attn_kernel(q_ref, k_ref, v_ref, kn_ref, g_ref, oin_ref, o_ref, *, tq, span, hpg):
    i = pl.program_id(2)
    q = q_ref[...]
    start = pl.multiple_of(jnp.maximum(i * tq - WINDOW, 0), tq)
    k = k_ref[pl.ds(start, span), :]
    v = v_ref[pl.ds(start, span), :]
    tpos = i * tq + lax.broadcasted_iota(jnp.int32, (tq, 1), 0)
    kpos = start + lax.broadcasted_iota(jnp.int32, (1, span), 1)
    diff = tpos - kpos
    bias = jnp.concatenate([jnp.where((diff >= 0) & (diff < WINDOW), 0.0, NEG_INF)] * hpg, axis=0)
    gates = g_ref[...]

    q8 = _stack_aug(q, [jnp.broadcast_to(-m, (tq, LANES)).astype(BF16) for m in _score_bounds(q, kn_ref[...], hpg)],
                    hpg)
    s = lax.dot_general(q8, k, NT_DIMS, preferred_element_type=F32) + bias
    acc = jnp.dot(jnp.exp(s).astype(BF16), v, preferred_element_type=F32)
    denom = acc[:, LANES:]
    _gated_store(o_ref, oin_ref, gates, acc[:, :LANES] / denom, 2 * hpg, tq, hpg)

    @pl.when(jnp.logical_not(jnp.min(denom) >= SOFTMAX_FLOOR))
    def _():
        q8x = _stack_aug(q, [jnp.zeros((tq, LANES), BF16)] * hpg, hpg)
        sx = lax.dot_general(q8x, k, NT_DIMS, preferred_element_type=F32) + bias
        p = jnp.exp(sx - jnp.max(sx, axis=-1, keepdims=True))
        acc_x = jnp.dot(p.astype(BF16), v, preferred_element_type=F32)
        _gated_store(o_ref, oin_ref, gates, acc_x[:, :LANES] / acc_x[:, LANES:], 2 * hpg, tq, hpg)


def _nsa_win_attn(q, kw, vw, k_gain, gates, o_in, *, groups, hpg):
    b, s, d = q.shape
    tq = _pick(s, (256, 128))
    span = WINDOW + tq
    assert s >= span and WINDOW % tq == 0
    qo_spec = pl.BlockSpec((None, tq, hpg * LANES), lambda bi, gi, i: (bi, i, gi))
    kv_spec = pl.BlockSpec((None, None, s, 2 * LANES), lambda bi, gi, i: (bi, gi, 0, 0))
    return pl.pallas_call(
        functools.partial(_win_attn_kernel, tq=tq, span=span, hpg=hpg),
        grid=(b, groups, s // tq),
        in_specs=[qo_spec, kv_spec, kv_spec, pl.BlockSpec((1, LANES), lambda bi, gi, i: (0, 0)),
                  pl.BlockSpec((None, tq, LANES), lambda bi, gi, i: (bi, i, gi)), qo_spec],
        out_specs=qo_spec,
        out_shape=jax.ShapeDtypeStruct((b, s, d), BF16),
        compiler_params=_params("parallel", "parallel", "parallel"),
        name="nsa_win_attn",
    )(q, kw, vw, k_gain.reshape(1, LANES), gates, o_in)


def _rope_tables(seq, dim):
    inv = ROPE_THETA ** (-jnp.arange(0, dim, 2, dtype=F32) / dim)
    ang = jnp.arange(seq, dtype=F32)[:, None] * inv[None, :]
    ang = jnp.concatenate([ang, ang], axis=-1)
    sign = jnp.where(jnp.arange(dim) < dim // 2, -1.0, 1.0).astype(F32)
    return jnp.cos(ang), jnp.sin(ang) * sign


def _nsa_mixer(x2, xg, ss, b, s, w_in, q_norm, k_norm, cmp_pos, cmp_w1, cmp_w2, w_o, next_gain):
    heads, groups = NSA_HEADS, NSA_GROUPS
    hpg = heads // groups
    d = x2.shape[1]
    assert d == heads * LANES and CMP_BLOCK == 2 * CMP_STRIDE and 3 * hpg <= LANES
    qd, kvd = heads * LANES, groups * LANES
    n_main = qd + 6 * kvd
    nseg = s // CMP_STRIDE
    n_cmp = (s - CMP_BLOCK) // CMP_STRIDE + 1
    n_slc = s // SLC_BLOCK
    top_n = min(SLC_TOPN, n_slc)
    assert n_cmp == nseg - 1 and n_slc % 8 == 0 and n_slc <= LANES and top_n >= 3

    proj = _mm(xg, ss, w_in, n=n_main, out_dtype=F32, name="nsa_in_proj")
    gates = _mm(xg, ss, w_in, col0=n_main, n=LANES, out_dtype=F32, sigmoid=True, name="nsa_gate_proj")
    gates = gates[:, :3 * heads].reshape(b * s, 3, groups, hpg).transpose(0, 2, 1, 3).reshape(b * s, groups, 3 * hpg)
    gates = jnp.pad(gates, ((0, 0), (0, 0), (0, LANES - 3 * hpg))).reshape(b, s, groups * LANES)

    cos, sin_signed = _rope_tables(s, LANES)
    q, kc, vc, ks, vs, kw, vw = _nsa_prep(proj.reshape(b, s, n_main), cos, sin_signed, q_norm, k_norm,
                                          heads=heads, groups=groups)

    w1 = cmp_w1.reshape(2, CMP_BLOCK, LANES, LANES).astype(BF16)
    kcmp, vcmp = _nsa_compress(kc, vc, cmp_pos, w1, cmp_w2.astype(BF16))

    cmp_start = np.arange(nseg) * CMP_STRIDE
    slc_start = np.arange(n_slc) * SLC_BLOCK
    ovl = ((cmp_start[None, :] < slc_start[:, None] + SLC_BLOCK) & (cmp_start[None, :] + CMP_BLOCK > slc_start[:, None])
           & (np.arange(nseg)[None, :] < n_cmp))
    ovl = jnp.asarray(ovl.astype(np.float32), dtype=BF16)

    o1, selm = _nsa_cmp_attn(q, kcmp, vcmp, gates, ovl, groups=groups, hpg=hpg, n_cmp=n_cmp, n_slc=n_slc, top_n=top_n)
    o2 = _nsa_sel_attn(q, ks, vs, selm, k_norm[1], gates, o1, groups=groups, hpg=hpg)
    o3 = _nsa_win_attn(q, kw, vw, k_norm[2], gates, o2, groups=groups, hpg=hpg)
    return _out_proj(o3.reshape(b * s, d), w_o.astype(BF16), x2, next_gain, name="nsa_out_proj")


def _chunk_cumsum(x, tri):
    w = x.shape[1]
    h1 = x.astype(BF16)
    r1 = x - h1.astype(F32)
    h2 = r1.astype(BF16)
    h3 = (r1 - h2.astype(F32)).astype(BF16)
    g3 = jnp.dot(tri, jnp.concatenate([h1, h2, h3], axis=1), preferred_element_type=F32)
    return g3[:, :w] + g3[:, w:2 * w] + g3[:, 2 * w:]


def _hgrn_kernel(q_ref, f_ref, i_ref, g_ref, lbl_ref, on_ref, o_ref, *, seq, chunk, layer, hps, unroll):
    lbl = lbl_ref[...]
    e = jnp.exp(lbl - jnp.max(lbl, axis=0, keepdims=True))
    p = e / jnp.sum(e, axis=0, keepdims=True)
    csum = p[0:1]
    for d in range(1, layer + 1):
        csum = csum + p[d:d + 1]
    lb = csum - p[0:1]
    o_gain = on_ref[...]
    causal = (lax.broadcasted_iota(jnp.int32, (chunk, chunk), 0) >= lax.broadcasted_iota(jnp.int32, (chunk, chunk), 1))
    tri = jnp.where(causal, 1.0, 0.0).astype(BF16)
    heads = [slice(k * LANES, (k + 1) * LANES) for k in range(hps)]

    def body(it, states):
        states = list(states)
        chunks = []
        for u in range(unroll):
            rows = pl.ds(pl.multiple_of((it * unroll + u) * chunk, chunk), chunk)
            f = lb + (1.0 - lb) * jax.nn.sigmoid(f_ref[rows, :])
            kc = 1.0 - f
            g_cum = _chunk_cumsum(jnp.log(f), tri)
            g_last = g_cum[chunk - 1:chunk, :]
            q_dec = (q_ref[rows, :] * jnp.exp(g_cum)).astype(BF16)
            k_inv = (kc * jnp.exp(-g_cum)).astype(BF16)
            k_tail = (kc * jnp.exp(g_last - g_cum)).astype(BF16)
            chunks.append((rows, q_dec, k_inv, k_tail, i_ref[rows, :].astype(BF16), jnp.exp(g_last)))
        a_all = [[lax.dot_general(q_dec[:, h], k_inv[:, h], NT_DIMS, preferred_element_type=F32) for h in heads]
                 for (_, q_dec, k_inv, _, _, _) in chunks]
        ds_all = [[lax.dot_general(v[:, h], k_tail[:, h], TN_DIMS, preferred_element_type=F32) for h in heads]
                  for (_, _, _, k_tail, v, _) in chunks]
        o_all = [[jnp.dot(jnp.where(causal, a, 0.0).astype(BF16), v[:, h], preferred_element_type=F32)
                  for a, h in zip(a_row, heads)]
                 for a_row, (_, _, _, _, v, _) in zip(a_all, chunks)]
        for u, (rows, q_dec, _, _, _, decay) in enumerate(chunks):
            for k, h in enumerate(heads):
                o = o_all[u][k] + lax.dot_general(q_dec[:, h], states[k].astype(BF16), NT_DIMS,
                                                  preferred_element_type=F32)
                states[k] = states[k] * decay[:, h] + ds_all[u][k]
                y = o * lax.rsqrt(jnp.mean(o * o, axis=-1, keepdims=True) + RMS_EPS) * o_gain
                gz = g_ref[rows, h]
                o_ref[rows, h] = (y * (gz * jax.nn.sigmoid(gz))).astype(o_ref.dtype)
        return tuple(states)

    init = tuple(jnp.zeros((LANES, LANES), F32) for _ in range(hps))
    lax.fori_loop(0, seq // (chunk * unroll), body, init)


def _hgrn_core(proj, lb_logits, o_norm, *, heads, layer, hps=4, unroll=8):
    b, s, _ = proj.shape
    depth = lb_logits.shape[0]
    hps = math.gcd(hps, heads)
    width = hps * LANES
    groups = heads // hps
    assert s % (HGRN_CHUNK * unroll) == 0

    def col_spec(part):
        return pl.BlockSpec((None, s, width), lambda bi, hi: (bi, 0, part * groups + hi))

    return pl.pallas_call(
        functools.partial(_hgrn_kernel, seq=s, chunk=HGRN_CHUNK, layer=layer, hps=hps, unroll=unroll),
        grid=(b, groups),
        in_specs=[col_spec(0), col_spec(1), col_spec(2), col_spec(3),
                  pl.BlockSpec((depth, width), lambda bi, hi: (0, hi)),
                  pl.BlockSpec((1, LANES), lambda bi, hi: (0, 0))],
        out_specs=pl.BlockSpec((None, s, width), lambda bi, hi: (bi, 0, hi)),
        out_shape=jax.ShapeDtypeStruct((b, s, heads * LANES), BF16),
        compiler_params=_params("parallel", "parallel"),
        name="hgrn_core",
    )(proj, proj, proj, proj, lb_logits, o_norm.reshape(1, LANES))


def _hgrn_mixer(x2, xg, ss, b, s, w_in, lb_logits, o_norm, w_o, layer, next_gain):
    heads = HGRN_HEADS
    d = x2.shape[1]
    assert d == heads * LANES and w_in.shape[1] == 4 * d and s % HGRN_CHUNK == 0
    proj = _mm(xg, ss, w_in, out_dtype=F32, name="hgrn_in_proj")
    o = _hgrn_core(proj.reshape(b, s, 4 * d), lb_logits, o_norm, heads=heads, layer=layer)
    return _out_proj(o.reshape(b * s, d), w_o.astype(BF16), x2, next_gain, name="hgrn_out_proj")


def kernel(x, ffn_norm, ffn_w_gate, ffn_w_up, ffn_w_down, mix_norm, nsa_w_in, nsa_q_norm, nsa_k_norm, nsa_cmp_pos,
           nsa_cmp_w1, nsa_cmp_w2, nsa_w_o, hgrn_w_in, hgrn_lb_logits, hgrn_o_norm, hgrn_w_o):
    b, s, d = x.shape
    depth = ffn_norm.shape[0]
    x2 = x.reshape(b * s, d)
    xg, ss = _prenorm(x2, ffn_norm[0, 0])
    for layer in range(depth):
        slot = layer // N_MIXERS
        x2, xg, ss = _ffn(x2, xg, ss, ffn_w_gate, ffn_w_up, ffn_w_down, (layer, 0), mix_norm[layer])
        if layer % N_MIXERS == 0:
            x2, xg, ss = _nsa_mixer(x2, xg, ss, b, s, nsa_w_in[slot], nsa_q_norm[slot], nsa_k_norm[slot],
                                    nsa_cmp_pos[slot], nsa_cmp_w1[slot], nsa_cmp_w2[slot], nsa_w_o[slot],
                                    ffn_norm[layer, 1])
        else:
            x2, xg, ss = _hgrn_mixer(x2, xg, ss, b, s, hgrn_w_in[slot], hgrn_lb_logits, hgrn_o_norm[slot],
                                     hgrn_w_o[slot], layer, ffn_norm[layer, 1])
        if layer + 1 < depth:
            x2, xg, ss = _ffn(x2, xg, ss, ffn_w_gate, ffn_w_up, ffn_w_down, (layer, 1), ffn_norm[layer + 1, 0])
        else:
            x2 = _ffn(x2, xg, ss, ffn_w_gate, ffn_w_up, ffn_w_down, (layer, 1), None)
    return x2.reshape(b, s, d)
```

```python
import functools
import math

import jax
import jax.numpy as jnp
import numpy as np
from jax import lax
from jax.experimental import pallas as pl
from jax.experimental.pallas import tpu as pltpu

F32 = jnp.float32
BF16 = jnp.bfloat16

RMS_EPS = 1e-6
NEG_INF = -1e30
FORCED_SCORE = 1e30
FFN_RES_WEIGHT = 0.5
N_MIXERS = 2

NSA_HEADS = 32
NSA_GROUPS = 4
CMP_BLOCK = 32
CMP_STRIDE = 16
SLC_BLOCK = 64
SLC_TOPN = 16
_SLC_SHIFT = SLC_BLOCK.bit_length() - 1
assert 1 << _SLC_SHIFT == SLC_BLOCK
WINDOW = 512
ROPE_THETA = 10000.0

HGRN_HEADS = 32
HGRN_CHUNK = 64

LANES = 128
SOFTMAX_FLOOR = 1e-30
BOUND_SLACK = 1.02
VMEM_LIMIT_BYTES = 56 * 1024 * 1024

NT_DIMS = (((1,), (1,)), ((), ()))
TN_DIMS = (((0,), (0,)), ((), ()))


def _params(*sem):
    return pltpu.CompilerParams(dimension_semantics=sem, vmem_limit_bytes=VMEM_LIMIT_BYTES)


def _pick(n, prefs):
    for p in prefs:
        if n % p == 0:
            return p
    raise ValueError(f"no tile in {prefs} divides {n}")


def _fold_lanes(x):
    out = x[:, :LANES]
    for t in range(1, x.shape[1] // LANES):
        out = out + x[:, t * LANES:(t + 1) * LANES]
    return out


def _inv_rms(ss, width):
    total = jnp.sum(jnp.sum(ss, axis=0), axis=-1, keepdims=True)
    return lax.rsqrt(total * (1.0 / width) + RMS_EPS)


def _prenorm_kernel(x_ref, g_ref, xg_ref, ss_ref):
    x = x_ref[...]
    xg_ref[...] = (x * g_ref[...]).astype(xg_ref.dtype)
    ss_ref[0] = _fold_lanes(x * x)


def _prenorm(x, g):
    m, d = x.shape
    tr = _pick(m, (256, 128, 8))
    return pl.pallas_call(
        _prenorm_kernel,
        grid=(m // tr,),
        in_specs=[pl.BlockSpec((tr, d), lambda i: (i, 0)), pl.BlockSpec((1, d), lambda i: (0, 0))],
        out_specs=[pl.BlockSpec((tr, d), lambda i: (i, 0)), pl.BlockSpec((1, tr, LANES), lambda i: (0, i, 0))],
        out_shape=[jax.ShapeDtypeStruct((m, d), BF16), jax.ShapeDtypeStruct((1, m, LANES), F32)],
        compiler_params=_params("parallel"),
        name="prenorm",
    )(x, g.reshape(1, d))


def _weight_spec(w, lead, bn, first_tile=0):
    k = w.shape[-2]
    return pl.BlockSpec((None,) * len(lead) + (k, bn), lambda i, j: lead + (0, first_tile + j))


def _lhs_spec(bm, k, single_buffer):
    if single_buffer:
        return pl.BlockSpec((bm, k), lambda i, j: (i, 0), pipeline_mode=pl.Buffered(1))
    return pl.BlockSpec((bm, k), lambda i, j: (i, 0))


def _stat_spec(planes, bm):
    return pl.BlockSpec((planes, bm, LANES), lambda i, j: (0, i, 0))


def _gateup_kernel(xg_ref, ss_ref, wg_ref, wu_ref, *rest, width, cast_down):
    if cast_down:
        wd_ref, o_ref, wd_out_ref, r_ref = rest
        wd_out_ref[...] = wd_ref[...].astype(wd_out_ref.dtype)
    else:
        o_ref, r_ref = rest

    @pl.when(pl.program_id(1) == 0)
    def _():
        r_ref[...] = jnp.broadcast_to(_inv_rms(ss_ref[...], width), r_ref.shape)

    xg = xg_ref[...]
    r = jnp.concatenate([r_ref[...]] * (o_ref.shape[1] // LANES), axis=1)
    a = jnp.dot(xg, wg_ref[...].astype(BF16), preferred_element_type=F32) * r
    b = jnp.dot(xg, wu_ref[...].astype(BF16), preferred_element_type=F32) * r
    o_ref[...] = (a * jax.nn.sigmoid(a) * b).astype(o_ref.dtype)


def _gateup(xg, ss, wg, wu, wd, lead):
    m, k = xg.shape
    n = wg.shape[-1]
    bm = _pick(m, (2048, 1024, 512, 256, 128))
    bn = _pick(n, (256, 128))
    steps_j = n // bn
    steps = (m // bm) * steps_j
    slab = wd.shape[-2] // steps
    cast_down = slab * steps == wd.shape[-2] and slab % 16 == 0
    in_specs = [_lhs_spec(bm, k, True), _stat_spec(ss.shape[0], bm), _weight_spec(wg, lead, bn),
                _weight_spec(wu, lead, bn)]
    out_specs = pl.BlockSpec((bm, bn), lambda i, j: (i, j))
    out_shape = jax.ShapeDtypeStruct((m, n), BF16)
    args = (xg, ss, wg, wu)
    if cast_down:
        d_out = wd.shape[-1]
        in_specs.append(pl.BlockSpec((None,) * len(lead) + (slab, d_out), lambda i, j: lead + (i * steps_j + j, 0)))
        out_specs = [out_specs, pl.BlockSpec((slab, d_out), lambda i, j: (i * steps_j + j, 0))]
        out_shape = [out_shape, jax.ShapeDtypeStruct(wd.shape[-2:], BF16)]
        args = args + (wd,)
    out = pl.pallas_call(
        functools.partial(_gateup_kernel, width=k, cast_down=cast_down),
        grid=(m // bm, steps_j),
        in_specs=in_specs,
        out_specs=out_specs,
        out_shape=out_shape,
        scratch_shapes=[pltpu.VMEM((bm, LANES), F32)],
        compiler_params=_params("parallel", "arbitrary"),
        name="ffn_gateup",
    )(*args)
    return out if cast_down else (out, wd[lead].astype(BF16))


def _out_proj_kernel(a_ref, w_ref, r_ref, gn_ref, o_ref, xg_ref, ss_ref):
    x = r_ref[...] + jnp.dot(a_ref[...], w_ref[...], preferred_element_type=F32)
    o_ref[...] = x
    xg_ref[...] = (x * gn_ref[...]).astype(xg_ref.dtype)

    @pl.when(pl.program_id(1) == 0)
    def _():
        ss_ref[...] = jnp.zeros_like(ss_ref)

    ss_ref[0] += _fold_lanes(x * x)


def _out_proj(a, w, res, next_gain, *, name):
    m, k = a.shape
    n = w.shape[-1]
    bm = _pick(m, (1024, 512, 256, 128))
    bn = _pick(n, (512, 256, 128))
    tile = pl.BlockSpec((bm, bn), lambda i, j: (i, j))
    return pl.pallas_call(
        _out_proj_kernel,
        grid=(m // bm, n // bn),
        in_specs=[_lhs_spec(bm, k, False), _weight_spec(w, (), bn), tile, pl.BlockSpec((1, bn), lambda i, j: (0, j))],
        out_specs=[tile, tile, _stat_spec(1, bm)],
        out_shape=[jax.ShapeDtypeStruct((m, n), F32), jax.ShapeDtypeStruct((m, n), BF16),
                   jax.ShapeDtypeStruct((1, m, LANES), F32)],
        compiler_params=_params("parallel", "arbitrary"),
        name=name,
    )(a, w, res, next_gain.reshape(1, n))


def _mm_kernel(xg_ref, ss_ref, w_ref, o_ref, *, width, sigmoid, live_cols):
    w = w_ref[...]
    if live_cols < w.shape[1]:
        w = jnp.where(lax.broadcasted_iota(jnp.int32, w.shape, 1) < live_cols, w, 0.0)
    acc = jnp.dot(xg_ref[...], w.astype(BF16), preferred_element_type=F32) * _inv_rms(ss_ref[...], width)
    if sigmoid:
        acc = jax.nn.sigmoid(acc)
    o_ref[...] = acc.astype(o_ref.dtype)


def _mm(xg, ss, w, *, col0=0, n=None, out_dtype, sigmoid=False, name):
    m, k = xg.shape
    n = w.shape[-1] if n is None else n
    bn = _pick(n, (512, 256, 128))
    assert col0 % bn == 0
    live_cols = min(bn, w.shape[-1] - col0) if n == bn else bn
    assert col0 + n <= w.shape[-1] or n == bn
    tall = n // bn >= 8
    bm = _pick(m, (2048, 1024, 512, 256, 128) if tall else (1024, 512, 256, 128))
    return pl.pallas_call(
        functools.partial(_mm_kernel, width=k, sigmoid=sigmoid, live_cols=live_cols),
        grid=(m // bm, n // bn),
        in_specs=[_lhs_spec(bm, k, tall), _stat_spec(ss.shape[0], bm), _weight_spec(w, (), bn, col0 // bn)],
        out_specs=pl.BlockSpec((bm, bn), lambda i, j: (i, j)),
        out_shape=jax.ShapeDtypeStruct((m, n), out_dtype),
        compiler_params=_params("parallel", "arbitrary"),
        name=name,
    )(xg, ss, w)


def _ffn_down_kernel(a_ref, w_ref, r_ref, *rest, scale, emit_norm):
    x = r_ref[...] + scale * jnp.dot(a_ref[...], w_ref[...], preferred_element_type=F32)
    if not emit_norm:
        (o_ref,) = rest
        o_ref[...] = x
        return
    gn_ref, o_ref, xg_ref, ss_ref = rest
    o_ref[...] = x
    xg_ref[...] = (x * gn_ref[...]).astype(xg_ref.dtype)
    ss_ref[...] = _fold_lanes(x * x)


def _ffn_down(act, w, res, scale, next_gain):
    m, k = act.shape
    n = w.shape[-1]
    bm = _pick(m, (512, 256, 128))
    bn = _pick(n, (1024, 512, 256, 128))
    tile = pl.BlockSpec((bm, bn), lambda j, i: (i, j))
    emit_norm = next_gain is not None
    in_specs = [pl.BlockSpec((bm, k), lambda j, i: (i, 0)),
                pl.BlockSpec((k, bn), lambda j, i: (0, j), pipeline_mode=pl.Buffered(1)),
                tile]
    out_specs, out_shape, args = tile, jax.ShapeDtypeStruct((m, n), F32), (act, w, res)
    if emit_norm:
        in_specs.append(pl.BlockSpec((1, bn), lambda j, i: (0, j)))
        out_specs = [tile, tile, pl.BlockSpec((None, bm, LANES), lambda j, i: (j, i, 0))]
        out_shape = [out_shape, jax.ShapeDtypeStruct((m, n), BF16), jax.ShapeDtypeStruct((n // bn, m, LANES), F32)]
        args = args + (next_gain.reshape(1, n),)
    return pl.pallas_call(
        functools.partial(_ffn_down_kernel, scale=scale, emit_norm=emit_norm),
        grid=(n // bn, m // bm),
        in_specs=in_specs,
        out_specs=out_specs,
        out_shape=out_shape,
        compiler_params=_params("parallel", "parallel"),
        name="ffn_down",
    )(*args)


def _ffn(x, xg, ss, w_gate, w_up, w_down, lead, next_gain):
    act, w_down_bf16 = _gateup(xg, ss, w_gate, w_up, w_down, lead)
    return _ffn_down(act, w_down_bf16, x, FFN_RES_WEIGHT, next_gain)


def _nsa_prep_kernel(p_ref, cos_ref, sin_ref, qn_ref, kn_ref, q_ref, kc_ref, vc_ref, ks_ref, vs_ref, kw_ref, vw_ref,
                     *, heads, groups):
    cos = cos_ref[...]
    sin = sin_ref[...]
    scale = LANES ** -0.5
    base = heads * LANES
    kv_w = groups * LANES
    ts = p_ref.shape[0]
    pos = pl.program_id(1) * ts + lax.broadcasted_iota(jnp.int32, (ts, LANES), 0)
    lane = lax.broadcasted_iota(jnp.int32, (ts, LANES), 1)
    blk_onehot = jnp.where(lane == jnp.right_shift(pos, _SLC_SHIFT), 1.0, 0.0).astype(ks_ref.dtype)
    lane0_onehot = jnp.where(lane == 0, 1.0, 0.0).astype(kw_ref.dtype)

    def store_q(h):
        def store(y):
            q_ref[:, h * LANES:(h + 1) * LANES] = y.astype(q_ref.dtype)
        return store

    def store_k(k_out, g, upper):
        def store(y):
            if upper is None:
                k_out[g] = y.astype(k_out.dtype)
            else:
                k_out[g, :, :LANES] = y.astype(k_out.dtype)
                k_out[g, :, LANES:] = upper
        return store

    jobs = [(h * LANES, qn_ref[...], scale, store_q(h)) for h in range(heads)]
    for br, (k_out, upper) in enumerate(((kc_ref, None), (ks_ref, blk_onehot), (kw_ref, lane0_onehot))):
        for g in range(groups):
            jobs.append((base + (2 * br) * kv_w + g * LANES, kn_ref[br:br + 1, :], None, store_k(k_out, g, upper)))
    inv = [lax.rsqrt(jnp.mean(jnp.square(p_ref[:, c:c + LANES]), axis=-1, keepdims=True) + RMS_EPS)
           for c, _, _, _ in jobs]
    for (c, gain, post, store), r in zip(jobs, inv):
        y = p_ref[:, c:c + LANES] * r * gain
        y = y * cos + pltpu.roll(y, LANES // 2, 1) * sin
        store(y if post is None else y * post)

    for br, v_out in enumerate((vc_ref, vs_ref, vw_ref)):
        for g in range(groups):
            cv = base + (2 * br + 1) * kv_w + g * LANES
            v = p_ref[:, cv:cv + LANES].astype(v_out.dtype)
            if br == 0:
                v_out[g] = v
            else:
                v_out[g, :, :LANES] = v
                v_out[g, :, LANES:] = jnp.ones_like(v)


def _nsa_prep(proj, cos, sin_signed, q_norm, k_norm, *, heads, groups):
    b, s, n = proj.shape
    ts = _pick(s, (256, 128))
    kv_spec = pl.BlockSpec((None, groups, ts, LANES), lambda bi, i: (bi, 0, i, 0))
    aug_spec = pl.BlockSpec((None, groups, ts, 2 * LANES), lambda bi, i: (bi, 0, i, 0))
    tab_spec = pl.BlockSpec((ts, LANES), lambda bi, i: (i, 0))

    def kv_shape(dt, width=LANES):
        return jax.ShapeDtypeStruct((b, groups, s, width), dt)

    return pl.pallas_call(
        functools.partial(_nsa_prep_kernel, heads=heads, groups=groups),
        grid=(b, s // ts),
        in_specs=[
            pl.BlockSpec((None, ts, n), lambda bi, i: (bi, i, 0)),
            tab_spec,
            tab_spec,
            pl.BlockSpec((1, LANES), lambda bi, i: (0, 0)),
            pl.BlockSpec((3, LANES), lambda bi, i: (0, 0)),
        ],
        out_specs=[pl.BlockSpec((None, ts, heads * LANES), lambda bi, i: (bi, i, 0)),
                   kv_spec, kv_spec, aug_spec, aug_spec, aug_spec, aug_spec],
        out_shape=[jax.ShapeDtypeStruct((b, s, heads * LANES), BF16),
                   kv_shape(F32), kv_shape(F32), kv_shape(BF16, 2 * LANES), kv_shape(BF16, 2 * LANES),
                   kv_shape(BF16, 2 * LANES), kv_shape(BF16, 2 * LANES)],
        compiler_params=_params("parallel", "parallel"),
        name="nsa_prep",
    )(proj, cos, sin_signed, q_norm.reshape(1, LANES), k_norm)


def _gelu_tanh(x):
    return x * (0.5 * (1.0 + jnp.tanh(math.sqrt(2.0 / math.pi) * (x + 0.044715 * (x * x * x)))))


def _compress_kernel(k_ref, v_ref, pe_ref, w1_ref, w2_ref, ko_ref, vo_ref):
    nseg = ko_ref.shape[0]
    for which, (x_ref, o_ref) in enumerate(((k_ref, ko_ref), (v_ref, vo_ref))):
        top = jnp.zeros((nseg, LANES), F32)
        bot = jnp.zeros((nseg, LANES), F32)
        for r in range(CMP_STRIDE):
            rows = x_ref[pl.ds(r, nseg, stride=CMP_STRIDE), :]
            top = top + jnp.dot((rows + pe_ref[which, r:r + 1, :]).astype(BF16), w1_ref[which, r],
                                preferred_element_type=F32)
            bot = bot + jnp.dot((rows + pe_ref[which, CMP_STRIDE + r:CMP_STRIDE + r + 1, :]).astype(BF16),
                                w1_ref[which, CMP_STRIDE + r], preferred_element_type=F32)
        pre = top + pltpu.roll(bot, nseg - 1, 0)
        hid = _gelu_tanh(pre).astype(BF16)
        o_ref[...] = jnp.dot(hid, w2_ref[which], preferred_element_type=F32).astype(o_ref.dtype)


def _nsa_compress(kc, vc, pe, w1, w2):
    b, g, s, _ = kc.shape
    nseg = s // CMP_STRIDE
    x_spec = pl.BlockSpec((None, None, s, LANES), lambda bi, gi: (bi, gi, 0, 0))
    o_spec = pl.BlockSpec((None, None, nseg, LANES), lambda bi, gi: (bi, gi, 0, 0))
    o_shape = jax.ShapeDtypeStruct((b, g, nseg, LANES), BF16)
    return pl.pallas_call(
        _compress_kernel,
        grid=(b, g),
        in_specs=[
            x_spec, x_spec,
            pl.BlockSpec((2, CMP_BLOCK, LANES), lambda bi, gi: (0, 0, 0)),
            pl.BlockSpec((2, CMP_BLOCK, LANES, LANES), lambda bi, gi: (0, 0, 0, 0)),
            pl.BlockSpec((2, LANES, LANES), lambda bi, gi: (0, 0, 0)),
        ],
        out_specs=[o_spec, o_spec],
        out_shape=[o_shape, o_shape],
        compiler_params=_params("parallel", "parallel"),
        name="nsa_compress",
    )(kc, vc, pe, w1, w2)


def _stack_heads(q, hpg):
    return jnp.concatenate([q[:, h * LANES:(h + 1) * LANES] for h in range(hpg)], axis=0)


def _cmp_attn_kernel(q_ref, kc_ref, vc_ref, g_ref, ovl_ref, o_ref, sel_ref, *, tq, hpg, n_cmp, n_slc, top_n):
    i = pl.program_id(2)
    q8 = _stack_heads(q_ref[...], hpg)
    rows = hpg * tq
    ncp = kc_ref.shape[0]
    s = lax.dot_general(q8, kc_ref[...], NT_DIMS, preferred_element_type=F32)
    tpos = i * tq + lax.broadcasted_iota(jnp.int32, (tq, 1), 0)
    tpos8 = jnp.concatenate([tpos] * hpg, axis=0)
    ncol = lax.broadcasted_iota(jnp.int32, (1, ncp), 1)
    valid = jnp.where((ncol * CMP_STRIDE + (CMP_BLOCK - 1) <= tpos8) & (ncol < n_cmp), 1.0, 0.0)
    sm = jnp.where(valid > 0.5, s, NEG_INF)
    m = jnp.max(sm, axis=-1, keepdims=True)
    e = jnp.exp(sm - m) * valid
    l = jnp.sum(e, axis=-1, keepdims=True)
    p = e / jnp.where(l > 0.0, l, 1.0)
    o = jnp.dot(p.astype(BF16), vc_ref[...], preferred_element_type=F32)
    gates = g_ref[...]
    for h in range(hpg):
        o_ref[:, h * LANES:(h + 1) * LANES] = o[h * tq:(h + 1) * tq] * gates[:, h:h + 1]

    psum = p[0:tq]
    for h in range(1, hpg):
        psum = psum + p[h * tq:(h + 1) * tq]
    hi = psum.astype(BF16)
    lo = (psum - hi.astype(F32)).astype(BF16)
    ovl = ovl_ref[...]
    imp = (lax.dot_general(ovl, hi, NT_DIMS, preferred_element_type=F32)
           + lax.dot_general(ovl, lo, NT_DIMS, preferred_element_type=F32))
    jidx = lax.broadcasted_iota(jnp.int32, (n_slc, tq), 0)
    tq_pos = i * tq + lax.broadcasted_iota(jnp.int32, (n_slc, tq), 1)
    cur = jnp.right_shift(tq_pos, _SLC_SHIFT)
    forced = (jidx == 0) | (jidx == cur) | (jidx == cur - 1)
    causal = jidx * SLC_BLOCK <= tq_pos
    imp = jnp.where(forced, FORCED_SCORE, jnp.where(causal, imp, NEG_INF))
    sel_rows = []
    for j in range(n_slc):
        row = imp[j:j + 1, :]
        lower = jnp.where(jidx < j, 1.0, 0.0)
        beats = jnp.where(imp > row, 1.0, jnp.where(imp == row, lower, 0.0))
        rank = jnp.sum(beats, axis=0, keepdims=True)
        sel_rows.append(jnp.where(rank < top_n, 0.0, NEG_INF))
    sel_rows.append(jnp.zeros((LANES - n_slc, tq), F32))
    bias_t = jnp.concatenate(sel_rows, axis=0)
    sel_ref[...] = bias_t.T.astype(sel_ref.dtype)


def _nsa_cmp_attn(q, kcmp, vcmp, gates, ovl, *, groups, hpg, n_cmp, n_slc, top_n):
    b, s, d = q.shape
    ncp = kcmp.shape[2]
    tq = _pick(s, (512, 256, 128))
    qo_spec = pl.BlockSpec((None, tq, hpg * LANES), lambda bi, gi, i: (bi, i, gi))
    c_spec = pl.BlockSpec((None, None, ncp, LANES), lambda bi, gi, i: (bi, gi, 0, 0))
    return pl.pallas_call(
        functools.partial(_cmp_attn_kernel, tq=tq, hpg=hpg, n_cmp=n_cmp, n_slc=n_slc, top_n=top_n),
        grid=(b, groups, s // tq),
        in_specs=[
            qo_spec, c_spec, c_spec,
            pl.BlockSpec((None, tq, LANES), lambda bi, gi, i: (bi, i, gi)),
            pl.BlockSpec((n_slc, ncp), lambda bi, gi, i: (0, 0)),
        ],
        out_specs=[qo_spec, pl.BlockSpec((None, None, tq, LANES), lambda bi, gi, i: (bi, gi, i, 0))],
        out_shape=[jax.ShapeDtypeStruct((b, s, d), F32), jax.ShapeDtypeStruct((b, groups, s, LANES), BF16)],
        compiler_params=_params("parallel", "parallel", "parallel"),
        name="nsa_cmp_attn",
    )(q, kcmp, vcmp, gates, ovl)


def _stack_aug(q, upper, hpg):
    return jnp.concatenate(
        [jnp.concatenate([q[:, h * LANES:(h + 1) * LANES], upper[h]], axis=1) for h in range(hpg)], axis=0)


def _score_bounds(q, k_gain, hpg):
    k_bound = BOUND_SLACK * math.sqrt(LANES) * jnp.max(jnp.abs(k_gain), axis=-1, keepdims=True)
    out = []
    for h in range(hpg):
        qh = q[:, h * LANES:(h + 1) * LANES].astype(F32)
        out.append(jnp.sqrt(jnp.sum(qh * qh, axis=-1, keepdims=True)) * k_bound)
    return out


def _gated_store(o_ref, oin_ref, gates, o, first_gate, tq, hpg):
    for h in range(hpg):
        cols = slice(h * LANES, (h + 1) * LANES)
        gate = gates[:, first_gate + h:first_gate + h + 1]
        o_ref[:, cols] = (oin_ref[:, cols] + o[h * tq:(h + 1) * tq] * gate).astype(o_ref.dtype)


def _sel_attn_kernel(q_ref, k_ref, v_ref, sel_ref, kn_ref, g_ref, oin_ref, o_ref, *, tq, tk, hpg):
    i = pl.program_id(2)
    q = q_ref[...]
    bias = sel_ref[...].astype(F32)
    rows = hpg * tq
    tpos = i * tq + lax.broadcasted_iota(jnp.int32, (tq, 1), 0)
    n_kv = ((i + 1) * tq + tk - 1) // tk
    gates = g_ref[...]

    def causal_bias(start):
        kpos = start + lax.broadcasted_iota(jnp.int32, (1, tk), 1)
        return jnp.concatenate([jnp.where(kpos <= tpos, 0.0, NEG_INF)] * hpg, axis=0)

    q8 = _stack_aug(q, [(bias - m).astype(BF16) for m in _score_bounds(q, kn_ref[...], hpg)], hpg)

    def fast_tile(kv, acc, diagonal):
        start = pl.multiple_of(kv * tk, tk)
        s = lax.dot_general(q8, k_ref[pl.ds(start, tk), :], NT_DIMS, preferred_element_type=F32)
        if diagonal:
            s = s + causal_bias(start)
        return acc + jnp.dot(jnp.exp(s).astype(BF16), v_ref[pl.ds(start, tk), :], preferred_element_type=F32)

    acc = lax.fori_loop(0, n_kv - 1, lambda kv, a: fast_tile(kv, a, False), jnp.zeros((rows, 2 * LANES), F32))
    acc = fast_tile(n_kv - 1, acc, True)
    denom = acc[:, LANES:]
    _gated_store(o_ref, oin_ref, gates, acc[:, :LANES] / denom, hpg, tq, hpg)

    @pl.when(jnp.logical_not(jnp.min(denom) >= SOFTMAX_FLOOR))
    def _():
        q8x = _stack_aug(q, [sel_ref[...]] * hpg, hpg)

        def exact_tile(kv, carry, diagonal):
            m, acc = carry
            start = pl.multiple_of(kv * tk, tk)
            s = lax.dot_general(q8x, k_ref[pl.ds(start, tk), :], NT_DIMS, preferred_element_type=F32)
            if diagonal:
                s = s + causal_bias(start)
            m_new = jnp.maximum(m, jnp.max(s, axis=-1, keepdims=True))
            p = jnp.exp(s - m_new).astype(BF16)
            acc = jnp.exp(m - m_new) * acc + jnp.dot(p, v_ref[pl.ds(start, tk), :], preferred_element_type=F32)
            return m_new, acc

        init = (jnp.full((rows, 1), NEG_INF, F32), jnp.zeros((rows, 2 * LANES), F32))
        carry = lax.fori_loop(0, n_kv - 1, lambda kv, c: exact_tile(kv, c, False), init)
        _, acc_x = exact_tile(n_kv - 1, carry, True)
        _gated_store(o_ref, oin_ref, gates, acc_x[:, :LANES] / acc_x[:, LANES:], hpg, tq, hpg)


def _nsa_sel_attn(q, ks, vs, selb, k_gain, gates, o_in, *, groups, hpg):
    b, s, d = q.shape
    tq = _pick(s, (512, 256, 128))
    tk = _pick(s, (512, 256, 128))
    qo_spec = pl.BlockSpec((None, tq, hpg * LANES), lambda bi, gi, i: (bi, i, gi))
    kv_spec = pl.BlockSpec((None, None, s, 2 * LANES), lambda bi, gi, i: (bi, gi, 0, 0))
    return pl.pallas_call(
        functools.partial(_sel_attn_kernel, tq=tq, tk=tk, hpg=hpg),
        grid=(b, groups, s // tq),
        in_specs=[
            qo_spec, kv_spec, kv_spec,
            pl.BlockSpec((None, None, tq, LANES), lambda bi, gi, i: (bi, gi, i, 0)),
            pl.BlockSpec((1, LANES), lambda bi, gi, i: (0, 0)),
            pl.BlockSpec((None, tq, LANES), lambda bi, gi, i: (bi, i, gi)),
            qo_spec,
        ],
        out_specs=qo_spec,
        out_shape=jax.ShapeDtypeStruct((b, s, d), F32),
        compiler_params=_params("parallel", "parallel", "parallel"),
        name="nsa_sel_attn",
    )(q, ks, vs, selb, k_gain.reshape(1, LANES), gates, o_in)


def _win_attn_kernel(q_ref, k_ref, v_ref, kn_ref, g_ref, oin_ref, o_ref, *, tq, span, hpg):
    i = pl.program_id(2)
    q = q_ref[...]
    gates = g_ref[...]
    wq = span - WINDOW
    bounds = _score_bounds(q, kn_ref[...], hpg)

    def window(sub, exact):
        r0 = sub * wq
        start = pl.multiple_of(jnp.maximum(i * tq + r0 - WINDOW, 0), wq)
        k = k_ref[pl.ds(start, span), :]
        v = v_ref[pl.ds(start, span), :]
        tpos = i * tq + r0 + lax.broadcasted_iota(jnp.int32, (wq, 1), 0)
        kpos = start + lax.broadcasted_iota(jnp.int32, (1, span), 1)
        diff = tpos - kpos
        bias = jnp.concatenate([jnp.where((diff >= 0) & (diff < WINDOW), 0.0, NEG_INF)] * hpg, axis=0)
        if exact:
            upper = [jnp.zeros((wq, LANES), BF16)] * hpg
        else:
            upper = [jnp.broadcast_to(-m[r0:r0 + wq], (wq, LANES)).astype(BF16) for m in bounds]
        s = lax.dot_general(_stack_aug(q[r0:r0 + wq], upper, hpg), k, NT_DIMS, preferred_element_type=F32) + bias
        if exact:
            s = s - jnp.max(s, axis=-1, keepdims=True)
        acc = jnp.dot(jnp.exp(s).astype(BF16), v, preferred_element_type=F32)
        return acc[:, :LANES], acc[:, LANES:]

    def store(results):
        for sub, (num, den) in enumerate(results):
            r0 = sub * wq
            o = num / den
            for h in range(hpg):
                cols = slice(h * LANES, (h + 1) * LANES)
                gate = gates[r0:r0 + wq, 2 * hpg + h:2 * hpg + h + 1]
                o_ref[r0:r0 + wq, cols] = (oin_ref[r0:r0 + wq, cols] + o[h * wq:(h + 1) * wq] * gate).astype(o_ref.dtype)

    fast = [window(sub, False) for sub in range(tq // wq)]
    store(fast)
    worst = fast[0][1]
    for _, den in fast[1:]:
        worst = jnp.minimum(worst, den)

    @pl.when(jnp.logical_not(jnp.min(worst) >= SOFTMAX_FLOOR))
    def _():
        store([window(sub, True) for sub in range(tq // wq)])


def _nsa_win_attn(q, kw, vw, k_gain, gates, o_in, *, groups, hpg):
    b, s, d = q.shape
    tq = _pick(s, (512, 256, 128))
    wq = LANES
    span = WINDOW + wq
    assert s >= span and WINDOW % wq == 0 and tq % wq == 0
    qo_spec = pl.BlockSpec((None, tq, hpg * LANES), lambda bi, gi, i: (bi, i, gi))
    kv_spec = pl.BlockSpec((None, None, s, 2 * LANES), lambda bi, gi, i: (bi, gi, 0, 0))
    return pl.pallas_call(
        functools.partial(_win_attn_kernel, tq=tq, span=span, hpg=hpg),
        grid=(b, groups, s // tq),
        in_specs=[qo_spec, kv_spec, kv_spec, pl.BlockSpec((1, LANES), lambda bi, gi, i: (0, 0)),
                  pl.BlockSpec((None, tq, LANES), lambda bi, gi, i: (bi, i, gi)), qo_spec],
        out_specs=qo_spec,
        out_shape=jax.ShapeDtypeStruct((b, s, d), BF16),
        compiler_params=_params("parallel", "parallel", "parallel"),
        name="nsa_win_attn",
    )(q, kw, vw, k_gain.reshape(1, LANES), gates, o_in)


def _rope_tables(seq, dim):
    inv = ROPE_THETA ** (-jnp.arange(0, dim, 2, dtype=F32) / dim)
    ang = jnp.arange(seq, dtype=F32)[:, None] * inv[None, :]
    ang = jnp.concatenate([ang, ang], axis=-1)
    sign = jnp.where(jnp.arange(dim) < dim // 2, -1.0, 1.0).astype(F32)
    return jnp.cos(ang), jnp.sin(ang) * sign


def _nsa_mixer(x2, xg, ss, b, s, w_in, q_norm, k_norm, cmp_pos, cmp_w1, cmp_w2, w_o, next_gain):
    heads, groups = NSA_HEADS, NSA_GROUPS
    hpg = heads // groups
    d = x2.shape[1]
    assert d == heads * LANES and CMP_BLOCK == 2 * CMP_STRIDE and 3 * hpg <= LANES
    qd, kvd = heads * LANES, groups * LANES
    n_main = qd + 6 * kvd
    nseg = s // CMP_STRIDE
    n_cmp = (s - CMP_BLOCK) // CMP_STRIDE + 1
    n_slc = s // SLC_BLOCK
    top_n = min(SLC_TOPN, n_slc)
    assert n_cmp == nseg - 1 and n_slc % 8 == 0 and n_slc <= LANES and top_n >= 3

    proj = _mm(xg, ss, w_in, n=n_main, out_dtype=F32, name="nsa_in_proj")
    gates = _mm(xg, ss, w_in, col0=n_main, n=LANES, out_dtype=F32, sigmoid=True, name="nsa_gate_proj")
    gates = gates[:, :3 * heads].reshape(b * s, 3, groups, hpg).transpose(0, 2, 1, 3).reshape(b * s, groups, 3 * hpg)
    gates = jnp.pad(gates, ((0, 0), (0, 0), (0, LANES - 3 * hpg))).reshape(b, s, groups * LANES)

    cos, sin_signed = _rope_tables(s, LANES)
    q, kc, vc, ks, vs, kw, vw = _nsa_prep(proj.reshape(b, s, n_main), cos, sin_signed, q_norm, k_norm,
                                          heads=heads, groups=groups)

    w1 = cmp_w1.reshape(2, CMP_BLOCK, LANES, LANES).astype(BF16)
    kcmp, vcmp = _nsa_compress(kc, vc, cmp_pos, w1, cmp_w2.astype(BF16))

    cmp_start = np.arange(nseg) * CMP_STRIDE
    slc_start = np.arange(n_slc) * SLC_BLOCK
    ovl = ((cmp_start[None, :] < slc_start[:, None] + SLC_BLOCK) & (cmp_start[None, :] + CMP_BLOCK > slc_start[:, None])
           & (np.arange(nseg)[None, :] < n_cmp))
    ovl = jnp.asarray(ovl.astype(np.float32), dtype=BF16)

    o1, selm = _nsa_cmp_attn(q, kcmp, vcmp, gates, ovl, groups=groups, hpg=hpg, n_cmp=n_cmp, n_slc=n_slc, top_n=top_n)
    o2 = _nsa_sel_attn(q, ks, vs, selm, k_norm[1], gates, o1, groups=groups, hpg=hpg)
    o3 = _nsa_win_attn(q, kw, vw, k_norm[2], gates, o2, groups=groups, hpg=hpg)
    return _out_proj(o3.reshape(b * s, d), w_o.astype(BF16), x2, next_gain, name="nsa_out_proj")


def _chunk_cumsum(x, tri):
    w = x.shape[1]
    h1 = x.astype(BF16)
    r1 = x - h1.astype(F32)
    h2 = r1.astype(BF16)
    h3 = (r1 - h2.astype(F32)).astype(BF16)
    g3 = jnp.dot(tri, jnp.concatenate([h1, h2, h3], axis=1), preferred_element_type=F32)
    return g3[:, :w] + g3[:, w:2 * w] + g3[:, 2 * w:]


def _hgrn_kernel(q_ref, f_ref, i_ref, g_ref, lbl_ref, on_ref, o_ref, *, seq, chunk, layer, hps, unroll):
    lbl = lbl_ref[...]
    e = jnp.exp(lbl - jnp.max(lbl, axis=0, keepdims=True))
    p = e / jnp.sum(e, axis=0, keepdims=True)
    csum = p[0:1]
    for d in range(1, layer + 1):
        csum = csum + p[d:d + 1]
    lb = csum - p[0:1]
    o_gain = on_ref[...]
    causal = (lax.broadcasted_iota(jnp.int32, (chunk, chunk), 0) >= lax.broadcasted_iota(jnp.int32, (chunk, chunk), 1))
    tri = jnp.where(causal, 1.0, 0.0).astype(BF16)
    heads = [slice(k * LANES, (k + 1) * LANES) for k in range(hps)]

    def body(it, states):
        states = list(states)
        chunks = []
        for u in range(unroll):
            rows = pl.ds(pl.multiple_of((it * unroll + u) * chunk, chunk), chunk)
            f = lb + (1.0 - lb) * jax.nn.sigmoid(f_ref[rows, :])
            kc = 1.0 - f
            g_cum = _chunk_cumsum(jnp.log(f), tri)
            g_last = g_cum[chunk - 1:chunk, :]
            q_dec = (q_ref[rows, :] * jnp.exp(g_cum)).astype(BF16)
            k_inv = (kc * jnp.exp(-g_cum)).astype(BF16)
            k_tail = (kc * jnp.exp(g_last - g_cum)).astype(BF16)
            chunks.append((rows, q_dec, k_inv, k_tail, i_ref[rows, :].astype(BF16), jnp.exp(g_last)))
        a_all = [[lax.dot_general(q_dec[:, h], k_inv[:, h], NT_DIMS, preferred_element_type=F32) for h in heads]
                 for (_, q_dec, k_inv, _, _, _) in chunks]
        ds_all = [[lax.dot_general(v[:, h], k_tail[:, h], TN_DIMS, preferred_element_type=F32) for h in heads]
                  for (_, _, _, k_tail, v, _) in chunks]
        o_all = [[jnp.dot(jnp.where(causal, a, 0.0).astype(BF16), v[:, h], preferred_element_type=F32)
                  for a, h in zip(a_row, heads)]
                 for a_row, (_, _, _, _, v, _) in zip(a_all, chunks)]
        for u, (rows, q_dec, _, _, _, decay) in enumerate(chunks):
            for k, h in enumerate(heads):
                o = o_all[u][k] + lax.dot_general(q_dec[:, h], states[k].astype(BF16), NT_DIMS,
                                                  preferred_element_type=F32)
                states[k] = states[k] * decay[:, h] + ds_all[u][k]
                y = o * lax.rsqrt(jnp.mean(o * o, axis=-1, keepdims=True) + RMS_EPS) * o_gain
                gz = g_ref[rows, h]
                o_ref[rows, h] = (y * (gz * jax.nn.sigmoid(gz))).astype(o_ref.dtype)
        return tuple(states)

    init = tuple(jnp.zeros((LANES, LANES), F32) for _ in range(hps))
    lax.fori_loop(0, seq // (chunk * unroll), body, init)


def _hgrn_core(proj, lb_logits, o_norm, *, heads, layer, hps=4, unroll=8):
    b, s, _ = proj.shape
    depth = lb_logits.shape[0]
    hps = math.gcd(hps, heads)
    width = hps * LANES
    groups = heads // hps
    assert s % (HGRN_CHUNK * unroll) == 0

    def col_spec(part):
        return pl.BlockSpec((None, s, width), lambda bi, hi: (bi, 0, part * groups + hi))

    return pl.pallas_call(
        functools.partial(_hgrn_kernel, seq=s, chunk=HGRN_CHUNK, layer=layer, hps=hps, unroll=unroll),
        grid=(b, groups),
        in_specs=[col_spec(0), col_spec(1), col_spec(2), col_spec(3),
                  pl.BlockSpec((depth, width), lambda bi, hi: (0, hi)),
                  pl.BlockSpec((1, LANES), lambda bi, hi: (0, 0))],
        out_specs=pl.BlockSpec((None, s, width), lambda bi, hi: (bi, 0, hi)),
        out_shape=jax.ShapeDtypeStruct((b, s, heads * LANES), BF16),
        compiler_params=_params("parallel", "parallel"),
        name="hgrn_core",
    )(proj, proj, proj, proj, lb_logits, o_norm.reshape(1, LANES))


def _hgrn_mixer(x2, xg, ss, b, s, w_in, lb_logits, o_norm, w_o, layer, next_gain):
    heads = HGRN_HEADS
    d = x2.shape[1]
    assert d == heads * LANES and w_in.shape[1] == 4 * d and s % HGRN_CHUNK == 0
    proj = _mm(xg, ss, w_in, out_dtype=F32, name="hgrn_in_proj")
    o = _hgrn_core(proj.reshape(b, s, 4 * d), lb_logits, o_norm, heads=heads, layer=layer)
    return _out_proj(o.reshape(b * s, d), w_o.astype(BF16), x2, next_gain, name="hgrn_out_proj")


def kernel(x, ffn_norm, ffn_w_gate, ffn_w_up, ffn_w_down, mix_norm, nsa_w_in, nsa_q_norm, nsa_k_norm, nsa_cmp_pos,
           nsa_cmp_w1, nsa_cmp_w2, nsa_w_o, hgrn_w_in, hgrn_lb_logits, hgrn_o_norm, hgrn_w_o):
    b, s, d = x.shape
    depth = ffn_norm.shape[0]
    x2 = x.reshape(b * s, d)
    xg, ss = _prenorm(x2, ffn_norm[0, 0])
    for layer in range(depth):
        slot = layer // N_MIXERS
        x2, xg, ss = _ffn(x2, xg, ss, ffn_w_gate, ffn_w_up, ffn_w_down, (layer, 0), mix_norm[layer])
        if layer % N_MIXERS == 0:
            x2, xg, ss = _nsa_mixer(x2, xg, ss, b, s, nsa_w_in[slot], nsa_q_norm[slot], nsa_k_norm[slot],
                                    nsa_cmp_pos[slot], nsa_cmp_w1[slot], nsa_cmp_w2[slot], nsa_w_o[slot],
                                    ffn_norm[layer, 1])
        else:
            x2, xg, ss = _hgrn_mixer(x2, xg, ss, b, s, hgrn_w_in[slot], hgrn_lb_logits, hgrn_o_norm[slot],
                                     hgrn_w_o[slot], layer, ffn_norm[layer, 1])
        if layer + 1 < depth:
            x2, xg, ss = _ffn(x2, xg, ss, ffn_w_gate, ffn_w_up, ffn_w_down, (layer, 1), ffn_norm[layer + 1, 0])
        else:
            x2 = _ffn(x2, xg, ss, ffn_w_gate, ffn_w_up, ffn_w_down, (layer, 1), None)
    return x2.reshape(b, s, d)
```

```python
import functools
import math

import jax
import jax.numpy as jnp
import numpy as np
from jax import lax
from jax.experimental import pallas as pl
from jax.experimental.pallas import tpu as pltpu

F32 = jnp.float32
BF16 = jnp.bfloat16

RMS_EPS = 1e-6
NEG_INF = -1e30
FORCED_SCORE = 1e30
FFN_RES_WEIGHT = 0.5
N_MIXERS = 2

NSA_HEADS = 32
NSA_GROUPS = 4
CMP_BLOCK = 32
CMP_STRIDE = 16
SLC_BLOCK = 64
SLC_TOPN = 16
_SLC_SHIFT = SLC_BLOCK.bit_length() - 1
assert 1 << _SLC_SHIFT == SLC_BLOCK
WINDOW = 512
ROPE_THETA = 10000.0

HGRN_HEADS = 32
HGRN_CHUNK = 64

LANES = 128
SOFTMAX_FLOOR = 1e-30
BOUND_SLACK = 1.02
VMEM_LIMIT_BYTES = 56 * 1024 * 1024

NT_DIMS = (((1,), (1,)), ((), ()))
TN_DIMS = (((0,), (0,)), ((), ()))


def _params(*sem):
    return pltpu.CompilerParams(dimension_semantics=sem, vmem_limit_bytes=VMEM_LIMIT_BYTES)


def _pick(n, prefs):
    for p in prefs:
        if n % p == 0:
            return p
    raise ValueError(f"no tile in {prefs} divides {n}")


def _fold_lanes(x):
    out = x[:, :LANES]
    for t in range(1, x.shape[1] // LANES):
        out = out + x[:, t * LANES:(t + 1) * LANES]
    return out


def _inv_rms(ss, width):
    total = jnp.sum(jnp.sum(ss, axis=0), axis=-1, keepdims=True)
    return lax.rsqrt(total * (1.0 / width) + RMS_EPS)


def _prenorm_kernel(x_ref, g_ref, xg_ref, ss_ref):
    x = x_ref[...]
    xg_ref[...] = (x * g_ref[...]).astype(xg_ref.dtype)
    ss_ref[0] = _fold_lanes(x * x)


def _prenorm(x, g):
    m, d = x.shape
    tr = _pick(m, (256, 128, 8))
    return pl.pallas_call(
        _prenorm_kernel,
        grid=(m // tr,),
        in_specs=[pl.BlockSpec((tr, d), lambda i: (i, 0)), pl.BlockSpec((1, d), lambda i: (0, 0))],
        out_specs=[pl.BlockSpec((tr, d), lambda i: (i, 0)), pl.BlockSpec((1, tr, LANES), lambda i: (0, i, 0))],
        out_shape=[jax.ShapeDtypeStruct((m, d), BF16), jax.ShapeDtypeStruct((1, m, LANES), F32)],
        compiler_params=_params("parallel"),
        name="prenorm",
    )(x, g.reshape(1, d))


def _weight_spec(w, lead, bn, first_tile=0):
    k = w.shape[-2]
    return pl.BlockSpec((None,) * len(lead) + (k, bn), lambda i, j: lead + (0, first_tile + j))


def _lhs_spec(bm, k, single_buffer):
    if single_buffer:
        return pl.BlockSpec((bm, k), lambda i, j: (i, 0), pipeline_mode=pl.Buffered(1))
    return pl.BlockSpec((bm, k), lambda i, j: (i, 0))


def _stat_spec(planes, bm):
    return pl.BlockSpec((planes, bm, LANES), lambda i, j: (0, i, 0))


def _gateup_kernel(xg_ref, ss_ref, wg_ref, wu_ref, *rest, width, cast_down):
    if cast_down:
        wd_ref, o_ref, wd_out_ref, r_ref = rest
        wd_out_ref[...] = wd_ref[...].astype(wd_out_ref.dtype)
    else:
        o_ref, r_ref = rest

    @pl.when(pl.program_id(1) == 0)
    def _():
        r_ref[...] = jnp.broadcast_to(_inv_rms(ss_ref[...], width), r_ref.shape)

    xg = xg_ref[...]
    r = jnp.concatenate([r_ref[...]] * (o_ref.shape[1] // LANES), axis=1)
    a = jnp.dot(xg, wg_ref[...].astype(BF16), preferred_element_type=F32) * r
    b = jnp.dot(xg, wu_ref[...].astype(BF16), preferred_element_type=F32) * r
    o_ref[...] = (a * jax.nn.sigmoid(a) * b).astype(o_ref.dtype)


def _gateup(xg, ss, wg, wu, wd, lead):
    m, k = xg.shape
    n = wg.shape[-1]
    bm = _pick(m, (2048, 1024, 512, 256, 128))
    bn = _pick(n, (256, 128))
    steps_j = n // bn
    steps = (m // bm) * steps_j
    slab = wd.shape[-2] // steps
    cast_down = slab * steps == wd.shape[-2] and slab % 16 == 0
    in_specs = [_lhs_spec(bm, k, True), _stat_spec(ss.shape[0], bm), _weight_spec(wg, lead, bn),
                _weight_spec(wu, lead, bn)]
    out_specs = pl.BlockSpec((bm, bn), lambda i, j: (i, j))
    out_shape = jax.ShapeDtypeStruct((m, n), BF16)
    args = (xg, ss, wg, wu)
    if cast_down:
        d_out = wd.shape[-1]
        in_specs.append(pl.BlockSpec((None,) * len(lead) + (slab, d_out), lambda i, j: lead + (i * steps_j + j, 0)))
        out_specs = [out_specs, pl.BlockSpec((slab, d_out), lambda i, j: (i * steps_j + j, 0))]
        out_shape = [out_shape, jax.ShapeDtypeStruct(wd.shape[-2:], BF16)]
        args = args + (wd,)
    out = pl.pallas_call(
        functools.partial(_gateup_kernel, width=k, cast_down=cast_down),
        grid=(m // bm, steps_j),
        in_specs=in_specs,
        out_specs=out_specs,
        out_shape=out_shape,
        scratch_shapes=[pltpu.VMEM((bm, LANES), F32)],
        compiler_params=_params("parallel", "arbitrary"),
        name="ffn_gateup",
    )(*args)
    return out if cast_down else (out, wd[lead].astype(BF16))


def _out_proj_kernel(a_ref, w_ref, r_ref, gn_ref, o_ref, xg_ref, ss_ref):
    x = r_ref[...] + jnp.dot(a_ref[...], w_ref[...], preferred_element_type=F32)
    o_ref[...] = x
    xg_ref[...] = (x * gn_ref[...]).astype(xg_ref.dtype)

    @pl.when(pl.program_id(1) == 0)
    def _():
        ss_ref[...] = jnp.zeros_like(ss_ref)

    ss_ref[0] += _fold_lanes(x * x)


def _out_proj(a, w, res, next_gain, *, name):
    m, k = a.shape
    n = w.shape[-1]
    bm = _pick(m, (1024, 512, 256, 128))
    bn = _pick(n, (512, 256, 128))
    tile = pl.BlockSpec((bm, bn), lambda i, j: (i, j))
    return pl.pallas_call(
        _out_proj_kernel,
        grid=(m // bm, n // bn),
        in_specs=[_lhs_spec(bm, k, False), _weight_spec(w, (), bn), tile, pl.BlockSpec((1, bn), lambda i, j: (0, j))],
        out_specs=[tile, tile, _stat_spec(1, bm)],
        out_shape=[jax.ShapeDtypeStruct((m, n), F32), jax.ShapeDtypeStruct((m, n), BF16),
                   jax.ShapeDtypeStruct((1, m, LANES), F32)],
        compiler_params=_params("parallel", "arbitrary"),
        name=name,
    )(a, w, res, next_gain.reshape(1, n))


def _mm_kernel(xg_ref, ss_ref, w_ref, o_ref, *, width, sigmoid, live_cols):
    w = w_ref[...]
    if live_cols < w.shape[1]:
        w = jnp.where(lax.broadcasted_iota(jnp.int32, w.shape, 1) < live_cols, w, 0.0)
    acc = jnp.dot(xg_ref[...], w.astype(BF16), preferred_element_type=F32) * _inv_rms(ss_ref[...], width)
    if sigmoid:
        acc = jax.nn.sigmoid(acc)
    o_ref[...] = acc.astype(o_ref.dtype)


def _mm(xg, ss, w, *, col0=0, n=None, out_dtype, sigmoid=False, name):
    m, k = xg.shape
    n = w.shape[-1] if n is None else n
    bn = _pick(n, (512, 256, 128))
    assert col0 % bn == 0
    live_cols = min(bn, w.shape[-1] - col0) if n == bn else bn
    assert col0 + n <= w.shape[-1] or n == bn
    tall = n // bn >= 8
    bm = _pick(m, (2048, 1024, 512, 256, 128) if tall else (1024, 512, 256, 128))
    return pl.pallas_call(
        functools.partial(_mm_kernel, width=k, sigmoid=sigmoid, live_cols=live_cols),
        grid=(m // bm, n // bn),
        in_specs=[_lhs_spec(bm, k, tall), _stat_spec(ss.shape[0], bm), _weight_spec(w, (), bn, col0 // bn)],
        out_specs=pl.BlockSpec((bm, bn), lambda i, j: (i, j)),
        out_shape=jax.ShapeDtypeStruct((m, n), out_dtype),
        compiler_params=_params("parallel", "arbitrary"),
        name=name,
    )(xg, ss, w)


def _ffn_down_kernel(a_ref, w_ref, r_ref, *rest, scale, emit_norm):
    x = r_ref[...] + scale * jnp.dot(a_ref[...], w_ref[...], preferred_element_type=F32)
    if not emit_norm:
        (o_ref,) = rest
        o_ref[...] = x
        return
    gn_ref, o_ref, xg_ref, ss_ref = rest
    o_ref[...] = x
    xg_ref[...] = (x * gn_ref[...]).astype(xg_ref.dtype)
    ss_ref[...] = _fold_lanes(x * x)


def _ffn_down(act, w, res, scale, next_gain):
    m, k = act.shape
    n = w.shape[-1]
    bm = _pick(m, (512, 256, 128))
    bn = _pick(n, (1024, 512, 256, 128))
    tile = pl.BlockSpec((bm, bn), lambda j, i: (i, j))
    emit_norm = next_gain is not None
    in_specs = [pl.BlockSpec((bm, k), lambda j, i: (i, 0)),
                pl.BlockSpec((k, bn), lambda j, i: (0, j), pipeline_mode=pl.Buffered(1)),
                tile]
    out_specs, out_shape, args = tile, jax.ShapeDtypeStruct((m, n), F32), (act, w, res)
    if emit_norm:
        in_specs.append(pl.BlockSpec((1, bn), lambda j, i: (0, j)))
        out_specs = [tile, tile, pl.BlockSpec((None, bm, LANES), lambda j, i: (j, i, 0))]
        out_shape = [out_shape, jax.ShapeDtypeStruct((m, n), BF16), jax.ShapeDtypeStruct((n // bn, m, LANES), F32)]
        args = args + (next_gain.reshape(1, n),)
    return pl.pallas_call(
        functools.partial(_ffn_down_kernel, scale=scale, emit_norm=emit_norm),
        grid=(n // bn, m // bm),
        in_specs=in_specs,
        out_specs=out_specs,
        out_shape=out_shape,
        compiler_params=_params("parallel", "parallel"),
        name="ffn_down",
    )(*args)


def _ffn(x, xg, ss, w_gate, w_up, w_down, lead, next_gain):
    act, w_down_bf16 = _gateup(xg, ss, w_gate, w_up, w_down, lead)
    return _ffn_down(act, w_down_bf16, x, FFN_RES_WEIGHT, next_gain)


def _nsa_prep_kernel(p_ref, cos_ref, sin_ref, qn_ref, kn_ref, q_ref, kc_ref, vc_ref, ks_ref, vs_ref, kw_ref, vw_ref,
                     *, heads, groups):
    cos = cos_ref[...]
    sin = sin_ref[...]
    scale = LANES ** -0.5
    base = heads * LANES
    kv_w = groups * LANES
    ts = p_ref.shape[0]
    pos = pl.program_id(1) * ts + lax.broadcasted_iota(jnp.int32, (ts, LANES), 0)
    lane = lax.broadcasted_iota(jnp.int32, (ts, LANES), 1)
    blk_onehot = jnp.where(lane == jnp.right_shift(pos, _SLC_SHIFT), 1.0, 0.0).astype(ks_ref.dtype)
    lane0_onehot = jnp.where(lane == 0, 1.0, 0.0).astype(kw_ref.dtype)

    def store_q(h):
        def store(y):
            q_ref[:, h * LANES:(h + 1) * LANES] = y.astype(q_ref.dtype)
        return store

    def store_k(k_out, g, upper):
        def store(y):
            if upper is None:
                k_out[g] = y.astype(k_out.dtype)
            else:
                k_out[g, :, :LANES] = y.astype(k_out.dtype)
                k_out[g, :, LANES:] = upper
        return store

    jobs = [(h * LANES, qn_ref[...], scale, store_q(h)) for h in range(heads)]
    for br, (k_out, upper) in enumerate(((kc_ref, None), (ks_ref, blk_onehot), (kw_ref, lane0_onehot))):
        for g in range(groups):
            jobs.append((base + (2 * br) * kv_w + g * LANES, kn_ref[br:br + 1, :], None, store_k(k_out, g, upper)))
    inv = [lax.rsqrt(jnp.mean(jnp.square(p_ref[:, c:c + LANES]), axis=-1, keepdims=True) + RMS_EPS)
           for c, _, _, _ in jobs]
    for (c, gain, post, store), r in zip(jobs, inv):
        y = p_ref[:, c:c + LANES] * r * gain
        y = y * cos + pltpu.roll(y, LANES // 2, 1) * sin
        store(y if post is None else y * post)

    for br, v_out in enumerate((vc_ref, vs_ref, vw_ref)):
        for g in range(groups):
            cv = base + (2 * br + 1) * kv_w + g * LANES
            v = p_ref[:, cv:cv + LANES].astype(v_out.dtype)
            if br == 0:
                v_out[g] = v
            else:
                v_out[g, :, :LANES] = v
                v_out[g, :, LANES:] = jnp.ones_like(v)


def _nsa_prep(proj, cos, sin_signed, q_norm, k_norm, *, heads, groups):
    b, s, n = proj.shape
    ts = _pick(s, (256, 128))
    kv_spec = pl.BlockSpec((None, groups, ts, LANES), lambda bi, i: (bi, 0, i, 0))
    aug_spec = pl.BlockSpec((None, groups, ts, 2 * LANES), lambda bi, i: (bi, 0, i, 0))
    tab_spec = pl.BlockSpec((ts, LANES), lambda bi, i: (i, 0))

    def kv_shape(dt, width=LANES):
        return jax.ShapeDtypeStruct((b, groups, s, width), dt)

    return pl.pallas_call(
        functools.partial(_nsa_prep_kernel, heads=heads, groups=groups),
        grid=(b, s // ts),
        in_specs=[
            pl.BlockSpec((None, ts, n), lambda bi, i: (bi, i, 0)),
            tab_spec,
            tab_spec,
            pl.BlockSpec((1, LANES), lambda bi, i: (0, 0)),
            pl.BlockSpec((3, LANES), lambda bi, i: (0, 0)),
        ],
        out_specs=[pl.BlockSpec((None, ts, heads * LANES), lambda bi, i: (bi, i, 0)),
                   kv_spec, kv_spec, aug_spec, aug_spec, aug_spec, aug_spec],
        out_shape=[jax.ShapeDtypeStruct((b, s, heads * LANES), BF16),
                   kv_shape(F32), kv_shape(F32), kv_shape(BF16, 2 * LANES), kv_shape(BF16, 2 * LANES),
                   kv_shape(BF16, 2 * LANES), kv_shape(BF16, 2 * LANES)],
        compiler_params=_params("parallel", "parallel"),
        name="nsa_prep",
    )(proj, cos, sin_signed, q_norm.reshape(1, LANES), k_norm)


def _gelu_tanh(x):
    return x * (0.5 * (1.0 + jnp.tanh(math.sqrt(2.0 / math.pi) * (x + 0.044715 * (x * x * x)))))


def _compress_kernel(k_ref, v_ref, pe_ref, w1_ref, w2_ref, ko_ref, vo_ref):
    nseg = ko_ref.shape[0]
    for which, (x_ref, o_ref) in enumerate(((k_ref, ko_ref), (v_ref, vo_ref))):
        top = jnp.zeros((nseg, LANES), F32)
        bot = jnp.zeros((nseg, LANES), F32)
        for r in range(CMP_STRIDE):
            rows = x_ref[pl.ds(r, nseg, stride=CMP_STRIDE), :]
            top = top + jnp.dot((rows + pe_ref[which, r:r + 1, :]).astype(BF16), w1_ref[which, r],
                                preferred_element_type=F32)
            bot = bot + jnp.dot((rows + pe_ref[which, CMP_STRIDE + r:CMP_STRIDE + r + 1, :]).astype(BF16),
                                w1_ref[which, CMP_STRIDE + r], preferred_element_type=F32)
        pre = top + pltpu.roll(bot, nseg - 1, 0)
        hid = _gelu_tanh(pre).astype(BF16)
        o_ref[...] = jnp.dot(hid, w2_ref[which], preferred_element_type=F32).astype(o_ref.dtype)


def _nsa_compress(kc, vc, pe, w1, w2):
    b, g, s, _ = kc.shape
    nseg = s // CMP_STRIDE
    x_spec = pl.BlockSpec((None, None, s, LANES), lambda bi, gi: (bi, gi, 0, 0))
    o_spec = pl.BlockSpec((None, None, nseg, LANES), lambda bi, gi: (bi, gi, 0, 0))
    o_shape = jax.ShapeDtypeStruct((b, g, nseg, LANES), BF16)
    return pl.pallas_call(
        _compress_kernel,
        grid=(b, g),
        in_specs=[
            x_spec, x_spec,
            pl.BlockSpec((2, CMP_BLOCK, LANES), lambda bi, gi: (0, 0, 0)),
            pl.BlockSpec((2, CMP_BLOCK, LANES, LANES), lambda bi, gi: (0, 0, 0, 0)),
            pl.BlockSpec((2, LANES, LANES), lambda bi, gi: (0, 0, 0)),
        ],
        out_specs=[o_spec, o_spec],
        out_shape=[o_shape, o_shape],
        compiler_params=_params("parallel", "parallel"),
        name="nsa_compress",
    )(kc, vc, pe, w1, w2)


def _stack_heads(q, hpg):
    return jnp.concatenate([q[:, h * LANES:(h + 1) * LANES] for h in range(hpg)], axis=0)


def _cmp_attn_kernel(q_ref, kc_ref, vc_ref, g_ref, ovl_ref, o_ref, sel_ref, *, tq, hpg, n_cmp, n_slc, top_n):
    i = pl.program_id(2)
    q8 = _stack_heads(q_ref[...], hpg)
    rows = hpg * tq
    ncp = kc_ref.shape[0]
    s = lax.dot_general(q8, kc_ref[...], NT_DIMS, preferred_element_type=F32)
    tpos = i * tq + lax.broadcasted_iota(jnp.int32, (tq, 1), 0)
    tpos8 = jnp.concatenate([tpos] * hpg, axis=0)
    ncol = lax.broadcasted_iota(jnp.int32, (1, ncp), 1)
    valid = jnp.where((ncol * CMP_STRIDE + (CMP_BLOCK - 1) <= tpos8) & (ncol < n_cmp), 1.0, 0.0)
    sm = jnp.where(valid > 0.5, s, NEG_INF)
    m = jnp.max(sm, axis=-1, keepdims=True)
    e = jnp.exp(sm - m) * valid
    l = jnp.sum(e, axis=-1, keepdims=True)
    p = e / jnp.where(l > 0.0, l, 1.0)
    o = jnp.dot(p.astype(BF16), vc_ref[...], preferred_element_type=F32)
    gates = g_ref[...]
    for h in range(hpg):
        o_ref[:, h * LANES:(h + 1) * LANES] = o[h * tq:(h + 1) * tq] * gates[:, h:h + 1]

    psum = p[0:tq]
    for h in range(1, hpg):
        psum = psum + p[h * tq:(h + 1) * tq]
    hi = psum.astype(BF16)
    lo = (psum - hi.astype(F32)).astype(BF16)
    ovl = ovl_ref[...]
    imp = (lax.dot_general(ovl, hi, NT_DIMS, preferred_element_type=F32)
           + lax.dot_general(ovl, lo, NT_DIMS, preferred_element_type=F32))
    jidx = lax.broadcasted_iota(jnp.int32, (n_slc, tq), 0)
    tq_pos = i * tq + lax.broadcasted_iota(jnp.int32, (n_slc, tq), 1)
    cur = jnp.right_shift(tq_pos, _SLC_SHIFT)
    forced = (jidx == 0) | (jidx == cur) | (jidx == cur - 1)
    causal = jidx * SLC_BLOCK <= tq_pos
    imp = jnp.where(forced, FORCED_SCORE, jnp.where(causal, imp, NEG_INF))
    sel_rows = []
    for j in range(n_slc):
        row = imp[j:j + 1, :]
        lower = jnp.where(jidx < j, 1.0, 0.0)
        beats = jnp.where(imp > row, 1.0, jnp.where(imp == row, lower, 0.0))
        rank = jnp.sum(beats, axis=0, keepdims=True)
        sel_rows.append(jnp.where(rank < top_n, 0.0, NEG_INF))
    sel_rows.append(jnp.zeros((LANES - n_slc, tq), F32))
    bias_t = jnp.concatenate(sel_rows, axis=0)
    sel_ref[...] = bias_t.T.astype(sel_ref.dtype)


def _nsa_cmp_attn(q, kcmp, vcmp, gates, ovl, *, groups, hpg, n_cmp, n_slc, top_n):
    b, s, d = q.shape
    ncp = kcmp.shape[2]
    tq = _pick(s, (512, 256, 128))
    qo_spec = pl.BlockSpec((None, tq, hpg * LANES), lambda bi, gi, i: (bi, i, gi))
    c_spec = pl.BlockSpec((None, None, ncp, LANES), lambda bi, gi, i: (bi, gi, 0, 0))
    return pl.pallas_call(
        functools.partial(_cmp_attn_kernel, tq=tq, hpg=hpg, n_cmp=n_cmp, n_slc=n_slc, top_n=top_n),
        grid=(b, groups, s // tq),
        in_specs=[
            qo_spec, c_spec, c_spec,
            pl.BlockSpec((None, tq, LANES), lambda bi, gi, i: (bi, i, gi)),
            pl.BlockSpec((n_slc, ncp), lambda bi, gi, i: (0, 0)),
        ],
        out_specs=[qo_spec, pl.BlockSpec((None, None, tq, LANES), lambda bi, gi, i: (bi, gi, i, 0))],
        out_shape=[jax.ShapeDtypeStruct((b, s, d), F32), jax.ShapeDtypeStruct((b, groups, s, LANES), BF16)],
        compiler_params=_params("parallel", "parallel", "parallel"),
        name="nsa_cmp_attn",
    )(q, kcmp, vcmp, gates, ovl)


def _stack_aug(q, upper, hpg):
    return jnp.concatenate(
        [jnp.concatenate([q[:, h * LANES:(h + 1) * LANES], upper[h]], axis=1) for h in range(hpg)], axis=0)


def _score_bounds(q, k_gain, hpg):
    k_bound = BOUND_SLACK * math.sqrt(LANES) * jnp.max(jnp.abs(k_gain), axis=-1, keepdims=True)
    out = []
    for h in range(hpg):
        qh = q[:, h * LANES:(h + 1) * LANES].astype(F32)
        out.append(jnp.sqrt(jnp.sum(qh * qh, axis=-1, keepdims=True)) * k_bound)
    return out


def _gated_store(o_ref, oin_ref, gates, o, first_gate, tq, hpg):
    for h in range(hpg):
        cols = slice(h * LANES, (h + 1) * LANES)
        gate = gates[:, first_gate + h:first_gate + h + 1]
        o_ref[:, cols] = (oin_ref[:, cols] + o[h * tq:(h + 1) * tq] * gate).astype(o_ref.dtype)


def _sel_attn_kernel(q_ref, k_ref, v_ref, sel_ref, kn_ref, g_ref, oin_ref, o_ref, *, tq, tk, hpg):
    i = pl.program_id(2)
    q = q_ref[...]
    bias = sel_ref[...].astype(F32)
    rows = hpg * tq
    tpos = i * tq + lax.broadcasted_iota(jnp.int32, (tq, 1), 0)
    n_kv = ((i + 1) * tq + tk - 1) // tk
    gates = g_ref[...]

    def causal_bias(start):
        kpos = start + lax.broadcasted_iota(jnp.int32, (1, tk), 1)
        return jnp.concatenate([jnp.where(kpos <= tpos, 0.0, NEG_INF)] * hpg, axis=0)

    q8 = _stack_aug(q, [(bias - m).astype(BF16) for m in _score_bounds(q, kn_ref[...], hpg)], hpg)

    def fast_tile(kv, acc, diagonal):
        start = pl.multiple_of(kv * tk, tk)
        s = lax.dot_general(q8, k_ref[pl.ds(start, tk), :], NT_DIMS, preferred_element_type=F32)
        if diagonal:
            s = s + causal_bias(start)
        return acc + jnp.dot(jnp.exp(s).astype(BF16), v_ref[pl.ds(start, tk), :], preferred_element_type=F32)

    acc = lax.fori_loop(0, n_kv - 1, lambda kv, a: fast_tile(kv, a, False), jnp.zeros((rows, 2 * LANES), F32))
    if tq == tk:
        dstart = pl.multiple_of((n_kv - 1) * tk, tk)
        pieces = [[None] * (tq // LANES) for _ in range(hpg)]
        for sub in range(tq // LANES):
            nk = (sub + 1) * LANES
            q_sub = jnp.concatenate([q8[h * tq + sub * LANES:h * tq + (sub + 1) * LANES] for h in range(hpg)], axis=0)
            t_in = sub * LANES + lax.broadcasted_iota(jnp.int32, (LANES, 1), 0)
            k_in = lax.broadcasted_iota(jnp.int32, (1, nk), 1)
            mask = jnp.concatenate([jnp.where(k_in <= t_in, 0.0, NEG_INF)] * hpg, axis=0)
            s = lax.dot_general(q_sub, k_ref[pl.ds(dstart, nk), :], NT_DIMS, preferred_element_type=F32) + mask
            part = jnp.dot(jnp.exp(s).astype(BF16), v_ref[pl.ds(dstart, nk), :], preferred_element_type=F32)
            for h in range(hpg):
                pieces[h][sub] = part[h * LANES:(h + 1) * LANES]
        acc = acc + jnp.concatenate([jnp.concatenate(p, axis=0) for p in pieces], axis=0)
    else:
        acc = fast_tile(n_kv - 1, acc, True)
    denom = acc[:, LANES:]
    _gated_store(o_ref, oin_ref, gates, acc[:, :LANES] / denom, hpg, tq, hpg)

    @pl.when(jnp.logical_not(jnp.min(denom) >= SOFTMAX_FLOOR))
    def _():
        q8x = _stack_aug(q, [sel_ref[...]] * hpg, hpg)

        def exact_tile(kv, carry, diagonal):
            m, acc = carry
            start = pl.multiple_of(kv * tk, tk)
            s = lax.dot_general(q8x, k_ref[pl.ds(start, tk), :], NT_DIMS, preferred_element_type=F32)
            if diagonal:
                s = s + causal_bias(start)
            m_new = jnp.maximum(m, jnp.max(s, axis=-1, keepdims=True))
            p = jnp.exp(s - m_new).astype(BF16)
            acc = jnp.exp(m - m_new) * acc + jnp.dot(p, v_ref[pl.ds(start, tk), :], preferred_element_type=F32)
            return m_new, acc

        init = (jnp.full((rows, 1), NEG_INF, F32), jnp.zeros((rows, 2 * LANES), F32))
        carry = lax.fori_loop(0, n_kv - 1, lambda kv, c: exact_tile(kv, c, False), init)
        _, acc_x = exact_tile(n_kv - 1, carry, True)
        _gated_store(o_ref, oin_ref, gates, acc_x[:, :LANES] / acc_x[:, LANES:], hpg, tq, hpg)


def _nsa_sel_attn(q, ks, vs, selb, k_gain, gates, o_in, *, groups, hpg):
    b, s, d = q.shape
    tq = _pick(s, (512, 256, 128))
    tk = _pick(s, (512, 256, 128))
    qo_spec = pl.BlockSpec((None, tq, hpg * LANES), lambda bi, gi, i: (bi, i, gi))
    kv_spec = pl.BlockSpec((None, None, s, 2 * LANES), lambda bi, gi, i: (bi, gi, 0, 0))
    return pl.pallas_call(
        functools.partial(_sel_attn_kernel, tq=tq, tk=tk, hpg=hpg),
        grid=(b, groups, s // tq),
        in_specs=[
            qo_spec, kv_spec, kv_spec,
            pl.BlockSpec((None, None, tq, LANES), lambda bi, gi, i: (bi, gi, i, 0)),
            pl.BlockSpec((1, LANES), lambda bi, gi, i: (0, 0)),
            pl.BlockSpec((None, tq, LANES), lambda bi, gi, i: (bi, i, gi)),
            qo_spec,
        ],
        out_specs=qo_spec,
        out_shape=jax.ShapeDtypeStruct((b, s, d), F32),
        compiler_params=_params("parallel", "parallel", "parallel"),
        name="nsa_sel_attn",
    )(q, ks, vs, selb, k_gain.reshape(1, LANES), gates, o_in)


def _win_attn_kernel(q_ref, k_ref, v_ref, kn_ref, g_ref, oin_ref, o_ref, *, tq, span, hpg):
    i = pl.program_id(2)
    q = q_ref[...]
    gates = g_ref[...]
    wq = span - WINDOW
    bounds = _score_bounds(q, kn_ref[...], hpg)

    def window(sub, exact):
        r0 = sub * wq
        start = pl.multiple_of(jnp.maximum(i * tq + r0 - WINDOW, 0), wq)
        k = k_ref[pl.ds(start, span), :]
        v = v_ref[pl.ds(start, span), :]
        tpos = i * tq + r0 + lax.broadcasted_iota(jnp.int32, (wq, 1), 0)
        kpos = start + lax.broadcasted_iota(jnp.int32, (1, span), 1)
        diff = tpos - kpos
        bias = jnp.concatenate([jnp.where((diff >= 0) & (diff < WINDOW), 0.0, NEG_INF)] * hpg, axis=0)
        if exact:
            upper = [jnp.zeros((wq, LANES), BF16)] * hpg
        else:
            upper = [jnp.broadcast_to(-m[r0:r0 + wq], (wq, LANES)).astype(BF16) for m in bounds]
        s = lax.dot_general(_stack_aug(q[r0:r0 + wq], upper, hpg), k, NT_DIMS, preferred_element_type=F32) + bias
        if exact:
            s = s - jnp.max(s, axis=-1, keepdims=True)
        acc = jnp.dot(jnp.exp(s).astype(BF16), v, preferred_element_type=F32)
        return acc[:, :LANES], acc[:, LANES:]

    def store(results):
        for sub, (num, den) in enumerate(results):
            r0 = sub * wq
            o = num / den
            for h in range(hpg):
                cols = slice(h * LANES, (h + 1) * LANES)
                gate = gates[r0:r0 + wq, 2 * hpg + h:2 * hpg + h + 1]
                o_ref[r0:r0 + wq, cols] = (oin_ref[r0:r0 + wq, cols] + o[h * wq:(h + 1) * wq] * gate).astype(o_ref.dtype)

    fast = [window(sub, False) for sub in range(tq // wq)]
    store(fast)
    worst = fast[0][1]
    for _, den in fast[1:]:
        worst = jnp.minimum(worst, den)

    @pl.when(jnp.logical_not(jnp.min(worst) >= SOFTMAX_FLOOR))
    def _():
        store([window(sub, True) for sub in range(tq // wq)])


def _nsa_win_attn(q, kw, vw, k_gain, gates, o_in, *, groups, hpg):
    b, s, d = q.shape
    tq = _pick(s, (512, 256, 128))
    wq = LANES
    span = WINDOW + wq
    assert s >= span and WINDOW % wq == 0 and tq % wq == 0
    qo_spec = pl.BlockSpec((None, tq, hpg * LANES), lambda bi, gi, i: (bi, i, gi))
    kv_spec = pl.BlockSpec((None, None, s, 2 * LANES), lambda bi, gi, i: (bi, gi, 0, 0))
    return pl.pallas_call(
        functools.partial(_win_attn_kernel, tq=tq, span=span, hpg=hpg),
        grid=(b, groups, s // tq),
        in_specs=[qo_spec, kv_spec, kv_spec, pl.BlockSpec((1, LANES), lambda bi, gi, i: (0, 0)),
                  pl.BlockSpec((None, tq, LANES), lambda bi, gi, i: (bi, i, gi)), qo_spec],
        out_specs=qo_spec,
        out_shape=jax.ShapeDtypeStruct((b, s, d), BF16),
        compiler_params=_params("parallel", "parallel", "parallel"),
        name="nsa_win_attn",
    )(q, kw, vw, k_gain.reshape(1, LANES), gates, o_in)


def _rope_tables(seq, dim):
    inv = ROPE_THETA ** (-jnp.arange(0, dim, 2, dtype=F32) / dim)
    ang = jnp.arange(seq, dtype=F32)[:, None] * inv[None, :]
    ang = jnp.concatenate([ang, ang], axis=-1)
    sign = jnp.where(jnp.arange(dim) < dim // 2, -1.0, 1.0).astype(F32)
    return jnp.cos(ang), jnp.sin(ang) * sign


def _nsa_mixer(x2, xg, ss, b, s, w_in, q_norm, k_norm, cmp_pos, cmp_w1, cmp_w2, w_o, next_gain):
    heads, groups = NSA_HEADS, NSA_GROUPS
    hpg = heads // groups
    d = x2.shape[1]
    assert d == heads * LANES and CMP_BLOCK == 2 * CMP_STRIDE and 3 * hpg <= LANES
    qd, kvd = heads * LANES, groups * LANES
    n_main = qd + 6 * kvd
    nseg = s // CMP_STRIDE
    n_cmp = (s - CMP_BLOCK) // CMP_STRIDE + 1
    n_slc = s // SLC_BLOCK
    top_n = min(SLC_TOPN, n_slc)
    assert n_cmp == nseg - 1 and n_slc % 8 == 0 and n_slc <= LANES and top_n >= 3

    proj = _mm(xg, ss, w_in, n=n_main, out_dtype=F32, name="nsa_in_proj")
    gates = _mm(xg, ss, w_in, col0=n_main, n=LANES, out_dtype=F32, sigmoid=True, name="nsa_gate_proj")
    gates = gates[:, :3 * heads].reshape(b * s, 3, groups, hpg).transpose(0, 2, 1, 3).reshape(b * s, groups, 3 * hpg)
    gates = jnp.pad(gates, ((0, 0), (0, 0), (0, LANES - 3 * hpg))).reshape(b, s, groups * LANES)

    cos, sin_signed = _rope_tables(s, LANES)
    q, kc, vc, ks, vs, kw, vw = _nsa_prep(proj.reshape(b, s, n_main), cos, sin_signed, q_norm, k_norm,
                                          heads=heads, groups=groups)

    w1 = cmp_w1.reshape(2, CMP_BLOCK, LANES, LANES).astype(BF16)
    kcmp, vcmp = _nsa_compress(kc, vc, cmp_pos, w1, cmp_w2.astype(BF16))

    cmp_start = np.arange(nseg) * CMP_STRIDE
    slc_start = np.arange(n_slc) * SLC_BLOCK
    ovl = ((cmp_start[None, :] < slc_start[:, None] + SLC_BLOCK) & (cmp_start[None, :] + CMP_BLOCK > slc_start[:, None])
           & (np.arange(nseg)[None, :] < n_cmp))
    ovl = jnp.asarray(ovl.astype(np.float32), dtype=BF16)

    o1, selm = _nsa_cmp_attn(q, kcmp, vcmp, gates, ovl, groups=groups, hpg=hpg, n_cmp=n_cmp, n_slc=n_slc, top_n=top_n)
    o2 = _nsa_sel_attn(q, ks, vs, selm, k_norm[1], gates, o1, groups=groups, hpg=hpg)
    o3 = _nsa_win_attn(q, kw, vw, k_norm[2], gates, o2, groups=groups, hpg=hpg)
    return _out_proj(o3.reshape(b * s, d), w_o.astype(BF16), x2, next_gain, name="nsa_out_proj")


def _chunk_cumsum(x, tri):
    w = x.shape[1]
    h1 = x.astype(BF16)
    r1 = x - h1.astype(F32)
    h2 = r1.astype(BF16)
    h3 = (r1 - h2.astype(F32)).astype(BF16)
    g3 = jnp.dot(tri, jnp.concatenate([h1, h2, h3], axis=1), preferred_element_type=F32)
    return g3[:, :w] + g3[:, w:2 * w] + g3[:, 2 * w:]


def _hgrn_kernel(q_ref, f_ref, i_ref, g_ref, lbl_ref, on_ref, o_ref, *, seq, chunk, layer, hps, unroll):
    lbl = lbl_ref[...]
    e = jnp.exp(lbl - jnp.max(lbl, axis=0, keepdims=True))
    p = e / jnp.sum(e, axis=0, keepdims=True)
    csum = p[0:1]
    for d in range(1, layer + 1):
        csum = csum + p[d:d + 1]
    lb = csum - p[0:1]
    o_gain = on_ref[...]
    causal = (lax.broadcasted_iota(jnp.int32, (chunk, chunk), 0) >= lax.broadcasted_iota(jnp.int32, (chunk, chunk), 1))
    tri = jnp.where(causal, 1.0, 0.0).astype(BF16)
    heads = [slice(k * LANES, (k + 1) * LANES) for k in range(hps)]

    def body(it, states):
        states = list(states)
        chunks = []
        for u in range(unroll):
            rows = pl.ds(pl.multiple_of((it * unroll + u) * chunk, chunk), chunk)
            f = lb + (1.0 - lb) * jax.nn.sigmoid(f_ref[rows, :])
            kc = 1.0 - f
            g_cum = _chunk_cumsum(jnp.log(f), tri)
            g_last = g_cum[chunk - 1:chunk, :]
            q_dec = (q_ref[rows, :] * jnp.exp(g_cum)).astype(BF16)
            k_inv = (kc * jnp.exp(-g_cum)).astype(BF16)
            k_tail = (kc * jnp.exp(g_last - g_cum)).astype(BF16)
            chunks.append((rows, q_dec, k_inv, k_tail, i_ref[rows, :].astype(BF16), jnp.exp(g_last)))
        a_all = [[lax.dot_general(q_dec[:, h], k_inv[:, h], NT_DIMS, preferred_element_type=F32) for h in heads]
                 for (_, q_dec, k_inv, _, _, _) in chunks]
        ds_all = [[lax.dot_general(v[:, h], k_tail[:, h], TN_DIMS, preferred_element_type=F32) for h in heads]
                  for (_, _, _, k_tail, v, _) in chunks]
        o_all = [[jnp.dot(jnp.where(causal, a, 0.0).astype(BF16), v[:, h], preferred_element_type=F32)
                  for a, h in zip(a_row, heads)]
                 for a_row, (_, _, _, _, v, _) in zip(a_all, chunks)]
        for u, (rows, q_dec, _, _, _, decay) in enumerate(chunks):
            for k, h in enumerate(heads):
                o = o_all[u][k] + lax.dot_general(q_dec[:, h], states[k].astype(BF16), NT_DIMS,
                                                  preferred_element_type=F32)
                states[k] = states[k] * decay[:, h] + ds_all[u][k]
                y = o * lax.rsqrt(jnp.mean(o * o, axis=-1, keepdims=True) + RMS_EPS) * o_gain
                gz = g_ref[rows, h]
                o_ref[rows, h] = (y * (gz * jax.nn.sigmoid(gz))).astype(o_ref.dtype)
        return tuple(states)

    init = tuple(jnp.zeros((LANES, LANES), F32) for _ in range(hps))
    lax.fori_loop(0, seq // (chunk * unroll), body, init)


def _hgrn_core(proj, lb_logits, o_norm, *, heads, layer, hps=4, unroll=8):
    b, s, _ = proj.shape
    depth = lb_logits.shape[0]
    hps = math.gcd(hps, heads)
    width = hps * LANES
    groups = heads // hps
    assert s % (HGRN_CHUNK * unroll) == 0

    def col_spec(part):
        return pl.BlockSpec((None, s, width), lambda bi, hi: (bi, 0, part * groups + hi))

    return pl.pallas_call(
        functools.partial(_hgrn_kernel, seq=s, chunk=HGRN_CHUNK, layer=layer, hps=hps, unroll=unroll),
        grid=(b, groups),
        in_specs=[col_spec(0), col_spec(1), col_spec(2), col_spec(3),
                  pl.BlockSpec((depth, width), lambda bi, hi: (0, hi)),
                  pl.BlockSpec((1, LANES), lambda bi, hi: (0, 0))],
        out_specs=pl.BlockSpec((None, s, width), lambda bi, hi: (bi, 0, hi)),
        out_shape=jax.ShapeDtypeStruct((b, s, heads * LANES), BF16),
        compiler_params=_params("parallel", "parallel"),
        name="hgrn_core",
    )(proj, proj, proj, proj, lb_logits, o_norm.reshape(1, LANES))


def _hgrn_mixer(x2, xg, ss, b, s, w_in, lb_logits, o_norm, w_o, layer, next_gain):
    heads = HGRN_HEADS
    d = x2.shape[1]
    assert d == heads * LANES and w_in.shape[1] == 4 * d and s % HGRN_CHUNK == 0
    proj = _mm(xg, ss, w_in, out_dtype=F32, name="hgrn_in_proj")
    o = _hgrn_core(proj.reshape(b, s, 4 * d), lb_logits, o_norm, heads=heads, layer=layer)
    return _out_proj(o.reshape(b * s, d), w_o.astype(BF16), x2, next_gain, name="hgrn_out_proj")


def kernel(x, ffn_norm, ffn_w_gate, ffn_w_up, ffn_w_down, mix_norm, nsa_w_in, nsa_q_norm, nsa_k_norm, nsa_cmp_pos,
           nsa_cmp_w1, nsa_cmp_w2, nsa_w_o, hgrn_w_in, hgrn_lb_logits, hgrn_o_norm, hgrn_w_o):
    b, s, d = x.shape
    depth = ffn_norm.shape[0]
    x2 = x.reshape(b * s, d)
    xg, ss = _prenorm(x2, ffn_norm[0, 0])
    for layer in range(depth):
        slot = layer // N_MIXERS
        x2, xg, ss = _ffn(x2, xg, ss, ffn_w_gate, ffn_w_up, ffn_w_down, (layer, 0), mix_norm[layer])
        if layer % N_MIXERS == 0:
            x2, xg, ss = _nsa_mixer(x2, xg, ss, b, s, nsa_w_in[slot], nsa_q_norm[slot], nsa_k_norm[slot],
                                    nsa_cmp_pos[slot], nsa_cmp_w1[slot], nsa_cmp_w2[slot], nsa_w_o[slot],
                                    ffn_norm[layer, 1])
        else:
            x2, xg, ss = _hgrn_mixer(x2, xg, ss, b, s, hgrn_w_in[slot], hgrn_lb_logits, hgrn_o_norm[slot],
                                     hgrn_w_o[slot], layer, ffn_norm[layer, 1])
        if layer + 1 < depth:
            x2, xg, ss = _ffn(x2, xg, ss, ffn_w_gate, ffn_w_up, ffn_w_down, (layer, 1), ffn_norm[layer + 1, 0])
        else:
            x2 = _ffn(x2, xg, ss, ffn_w_gate, ffn_w_up, ffn_w_down, (layer, 1), None)
    return x2.reshape(b, s, d)
```
